```python
import math
import jax, jax.numpy as jnp
from jax import lax
import numpy as np

D_MODEL = 1024
BATCH = 2
SEQ = 8192
DEPTH = 2

CTX_LEN = 256
GRID_W = 64
N_MIXERS = 2
EPS = 1e-6

FN_EXPAND = 2
FN_WIDTH = FN_EXPAND * D_MODEL
FN_GROUPS = 8
FN_GROUP_DIM = FN_WIDTH // FN_GROUPS

DN_HEADS = 8
DN_DK = 128
DN_DV = 256
DN_QK_WIDTH = DN_HEADS * DN_DK
DN_V_WIDTH = DN_HEADS * DN_DV
DN_QKV_WIDTH = 2 * DN_QK_WIDTH + DN_V_WIDTH
DN_PROJ_WIDTH = DN_QKV_WIDTH + DN_V_WIDTH + 4 * DN_HEADS
DN_CONV = 3
DN_CHUNK = 64

N_FOURIER_LAYERS = (DEPTH + 1) // 2
N_DELTA_LAYERS = DEPTH // 2

kernel_name = "hybrid_fourier_gdn_prefix_backbone"


def rmsnorm(x, g):
    xf = x.astype(jnp.float32)
    y = xf * lax.rsqrt(jnp.mean(xf * xf, axis=-1, keepdims=True) + EPS)
    return (y * g.astype(jnp.float32)).astype(x.dtype)


def l2norm(x):
    return x * lax.rsqrt(jnp.sum(x * x, axis=-1, keepdims=True) + EPS)


def adaln(cond, w, b):
    m = jax.nn.silu(cond) @ w + b
    return jnp.split(m, 3, axis=-1)


def fourier_mix(h, w_in, w_grp, w_out):
    u, z = jnp.split(h @ w_in, 2, axis=-1)
    b_, l_ = u.shape[:2]
    u = u.reshape(b_, l_, FN_GROUPS, FN_GROUP_DIM).astype(jnp.float32)
    f = jnp.real(jnp.fft.fft2(u, axes=(1, 3), norm="ortho")).astype(h.dtype)
    f = jnp.einsum("blgc,gcd->blgd", f, w_grp).reshape(b_, l_, FN_WIDTH)
    return (f * jax.nn.silu(z)) @ w_out


def short_conv_silu(u, w, rows, cols):
    b_, l_, ch = u.shape
    y = lax.conv_general_dilated(
        u.reshape(b_, rows, cols, ch), w[:, :, None, :].astype(u.dtype),
        window_strides=(1, 1), padding="SAME",
        dimension_numbers=("NHWC", "HWIO", "NHWC"), feature_group_count=ch)
    return jax.nn.silu(y.reshape(b_, l_, ch))


def gated_delta_chunked(q, k, v, g, beta, s0, with_output):
    b_, h_, l_, dk = q.shape
    dv = v.shape[-1]
    c_ = DN_CHUNK
    n_ = l_ // c_
    q = q.reshape(b_, h_, n_, c_, dk)
    k = k.reshape(b_, h_, n_, c_, dk)
    v = v.reshape(b_, h_, n_, c_, dv)
    g = jnp.cumsum(g.reshape(b_, h_, n_, c_), axis=-1)
    beta = beta.reshape(b_, h_, n_, c_)
    lower = jnp.tril(jnp.ones((c_, c_), dtype=bool))
    strict = jnp.tril(jnp.ones((c_, c_), dtype=bool), -1)
    diff = g[..., :, None] - g[..., None, :]
    decay = jnp.where(lower, jnp.exp(jnp.where(lower, diff, 0.0)), 0.0)
    kb = k * beta[..., None]
    kk = jnp.einsum("bhnid,bhnjd->bhnij", kb, k)
    a_mat = jnp.eye(c_, dtype=q.dtype) + jnp.where(strict, kk * decay, 0.0)
    rhs = jnp.concatenate([v * beta[..., None], kb * jnp.exp(g)[..., None]], axis=-1)
    sol = lax.linalg.triangular_solve(a_mat, rhs, left_side=True, lower=True, unit_diagonal=True)
    u, w = sol[..., :dv], sol[..., dv:]
    g_last = g[..., -1]
    k_tail = k * jnp.exp(g_last[..., None] - g)[..., None]
    to_scan = lambda t: jnp.moveaxis(t, 2, 0)
    if with_output:
        qk = jnp.einsum("bhnid,bhnjd->bhnij", q, k) * decay
        q_head = q * jnp.exp(g)[..., None]
        xs = tuple(map(to_scan, (u, w, k_tail, g_last, q_head, qk)))
    else:
        xs = tuple(map(to_scan, (u, w, k_tail, g_last)))

    def step(state, xs_c):
        u_c, w_c, kt_c, gl_c = xs_c[:4]
        v_new = u_c - jnp.einsum("bhcd,bhde->bhce", w_c, state)
        s_next = state * jnp.exp(gl_c)[..., None, None] + jnp.einsum("bhcd,bhce->bhde", kt_c, v_new)
        if with_output:
            qh_c, qk_c = xs_c[4:]
            o = jnp.einsum("bhcd,bhde->bhce", qh_c, state) + jnp.einsum("bhij,bhje->bhie", qk_c, v_new)
            return s_next, o
        return s_next, None

    s_final, o = lax.scan(step, s0, xs)
    if with_output:
        o = jnp.moveaxis(o, 0, 2).reshape(b_, h_, l_, dv)
    return o, s_final


def delta_project(h, w_in, conv_w, a_log, dt_bias, rows, cols):
    b_, l_, _ = h.shape
    p = h @ w_in
    qkv, z, a, bb = jnp.split(p, [DN_QKV_WIDTH, DN_QKV_WIDTH + DN_V_WIDTH,
                                  DN_QKV_WIDTH + DN_V_WIDTH + 2 * DN_HEADS], axis=-1)
    qkv = short_conv_silu(qkv, conv_w, rows, cols).astype(jnp.float32)
    q, k, v = jnp.split(qkv, [DN_QK_WIDTH, 2 * DN_QK_WIDTH], axis=-1)
    q = l2norm(q.reshape(b_, l_, DN_HEADS, DN_DK)).transpose(0, 2, 1, 3) * (DN_DK ** -0.5)
    k = l2norm(k.reshape(b_, l_, DN_HEADS, DN_DK)).transpose(0, 2, 1, 3)
    v = v.reshape(b_, l_, DN_HEADS, DN_DV).transpose(0, 2, 1, 3)
    a = a.astype(jnp.float32).reshape(b_, l_, 2, DN_HEADS)
    bb = bb.astype(jnp.float32).reshape(b_, l_, 2, DN_HEADS)
    g = -jnp.exp(a_log.astype(jnp.float32)) * jax.nn.softplus(a + dt_bias.astype(jnp.float32))
    beta = jax.nn.sigmoid(bb)
    g = jnp.transpose(g, (2, 0, 3, 1))
    beta = jnp.transpose(beta, (2, 0, 3, 1))
    return q, k, v, z, g, beta


def delta_output(o, z, norm_g, w_out):
    b_, h_, l_, dv = o.shape
    o = o * lax.rsqrt(jnp.mean(o * o, axis=-1, keepdims=True) + EPS) * norm_g.astype(jnp.float32)
    o = o.transpose(0, 2, 1, 3).reshape(b_, l_, h_ * dv).astype(z.dtype)
    return (o * jax.nn.silu(z)) @ w_out


def delta_mix(h_lat, h_ctx, w_in, conv_w, a_log, dt_bias, norm_g, w_out, rows, ctx_output):
    b_ = h_lat.shape[0]
    ql, kl, vl, zl, gl, bl = delta_project(h_lat, w_in, conv_w, a_log, dt_bias, rows, GRID_W)
    qc, kc, vc, zc, gc, bc = delta_project(h_ctx, w_in, conv_w, a_log, dt_bias, 1, h_ctx.shape[1])
    s0 = jnp.zeros((b_, DN_HEADS, DN_DK, DN_DV), jnp.float32)
    o_lat = 0.0
    o_ctx = 0.0
    for d in range(2):
        f = (lambda t: jnp.flip(t, axis=2)) if d == 1 else (lambda t: t)
        oc, s_ctx = gated_delta_chunked(f(qc), f(kc), f(vc), f(gc[d]), f(bc[d]), s0, ctx_output)
        ol, _ = gated_delta_chunked(f(ql), f(kl), f(vl), f(gl[d]), f(bl[d]), s_ctx, True)
        o_lat = o_lat + f(ol)
        if ctx_output:
            o_ctx = o_ctx + f(oc)
    out_lat = delta_output(o_lat, zl, norm_g, w_out)
    out_ctx = delta_output(o_ctx, zc, norm_g, w_out) if ctx_output else None
    return out_lat, out_ctx


def setup_inputs(seed: int = 0) -> dict:
    key = jax.random.key(seed)
    ks = jax.random.split(key, 20)
    f32 = jnp.float32
    D = D_MODEL
    nrm = lambda k, shape, s: jax.random.normal(k, shape, f32) * s
    log_dt = jax.random.uniform(ks[15], (N_DELTA_LAYERS, 2, DN_HEADS), f32,
                                math.log(1e-3), math.log(1e-1))
    dt = jnp.exp(log_dt)
    return {
        "x": nrm(ks[0], (BATCH, SEQ, D), 1.0),
        "c": nrm(ks[1], (BATCH, D), 1.0),
        "ctx": nrm(ks[2], (BATCH, CTX_LEN, D), 1.0),
        "c_ctx": nrm(ks[3], (D,), 1.0),
        "mod_w": nrm(ks[4], (DEPTH, D, 3 * D), 0.5 * D ** -0.5),
        "mod_b": nrm(ks[5], (DEPTH, 3 * D), 0.01),
        "norm_g": 1.0 + nrm(ks[6], (DEPTH, D), 0.01),
        "final_g": 1.0 + nrm(ks[7], (D,), 0.01),
        "fn_w_in": nrm(ks[8], (N_FOURIER_LAYERS, D, 2 * FN_WIDTH), D ** -0.5),
        "fn_w_grp": nrm(ks[9], (N_FOURIER_LAYERS, FN_GROUPS, FN_GROUP_DIM, FN_GROUP_DIM), FN_GROUP_DIM ** -0.5),
        "fn_w_out": nrm(ks[10], (N_FOURIER_LAYERS, FN_WIDTH, D), FN_WIDTH ** -0.5),
        "dn_w_in": nrm(ks[11], (N_DELTA_LAYERS, D, DN_PROJ_WIDTH), D ** -0.5),
        "dn_conv": nrm(ks[12], (N_DELTA_LAYERS, DN_CONV, DN_CONV, DN_QKV_WIDTH), 1.0 / DN_CONV),
        "dn_a_log": jnp.log(jax.random.uniform(ks[13], (N_DELTA_LAYERS, 2, DN_HEADS), f32, 1.0, 16.0)),
        "dn_dt_bias": dt + jnp.log(-jnp.expm1(-dt)),
        "dn_norm_g": 1.0 + nrm(ks[14], (N_DELTA_LAYERS, DN_DV), 0.01),
        "dn_w_out": nrm(ks[16], (N_DELTA_LAYERS, DN_V_WIDTH, D), DN_V_WIDTH ** -0.5),
    }


def reference(x, c, ctx, c_ctx, mod_w, mod_b, norm_g, final_g, fn_w_in, fn_w_grp, fn_w_out,
              dn_w_in, dn_conv, dn_a_log, dn_dt_bias, dn_norm_g, dn_w_out):
    rows = x.shape[1] // GRID_W
    h_lat, h_ctx = x, ctx
    cond_lat = c[:, None, :]
    cond_ctx = c_ctx[None, None, :]
    for i in range(DEPTH):
        last = i == DEPTH - 1
        mixer = i % N_MIXERS
        j = i // N_MIXERS
        sh_l, sc_l, gt_l = adaln(cond_lat, mod_w[i], mod_b[i])
        n_lat = rmsnorm(h_lat, norm_g[i]) * (1.0 + sc_l) + sh_l
        need_ctx = (not last) or mixer == 1
        if need_ctx:
            sh_c, sc_c, gt_c = adaln(cond_ctx, mod_w[i], mod_b[i])
            n_ctx = rmsnorm(h_ctx, norm_g[i]) * (1.0 + sc_c) + sh_c
        if mixer == 0:
            out_lat = fourier_mix(n_lat, fn_w_in[j], fn_w_grp[j], fn_w_out[j])
            out_ctx = fourier_mix(n_ctx, fn_w_in[j], fn_w_grp[j], fn_w_out[j]) if not last else None
        else:
            out_lat, out_ctx = delta_mix(n_lat, n_ctx, dn_w_in[j], dn_conv[j], dn_a_log[j],
                                         dn_dt_bias[j], dn_norm_g[j], dn_w_out[j], rows,
                                         not last)
        h_lat = h_lat + gt_l * out_lat
        if not last:
            h_ctx = h_ctx + gt_c * out_ctx
    return rmsnorm(h_lat, final_g)
```

```python
import functools
import math

import numpy as np
import jax
import jax.numpy as jnp
from jax import lax
from jax.experimental import pallas as pl
from jax.experimental.pallas import tpu as pltpu

F32 = jnp.float32
BF16 = jnp.bfloat16
HI = lax.Precision.HIGHEST
EPS = 1e-6

GRID_W = 64
FN_GROUPS = 8
DN_HEADS = 8
DN_DK = 128
DN_DV = 256
SCAN_CHUNK = 64
DFT_N2 = 128
GB_LANES = 128
VMEM_LIMIT_BYTES = 48 * 1024 * 1024


def _cparams(*sem):
    return pltpu.CompilerParams(dimension_semantics=sem, vmem_limit_bytes=VMEM_LIMIT_BYTES)


def _const_spec(shape):
    zeros = (0,) * len(shape)
    return pl.BlockSpec(shape, lambda *_: zeros)


def _silu(v):
    return v * jax.nn.sigmoid(v)


def _mods_kernel(cond_ref, w_ref, b_ref, o_ref):
    s = _silu(cond_ref[...])
    o_ref[0] = jnp.dot(s, w_ref[0], preferred_element_type=F32, precision=HI) + b_ref[0]


def _mods(cond, mod_w, mod_b):
    depth, d, d3 = mod_w.shape
    rows = cond.shape[0]
    tn = 512
    return pl.pallas_call(
        _mods_kernel,
        out_shape=jax.ShapeDtypeStruct((depth, rows, d3), F32),
        grid=(depth, d3 // tn),
        in_specs=[pl.BlockSpec((rows, d), lambda i, j: (0, 0)),
                  pl.BlockSpec((1, d, tn), lambda i, j: (i, 0, j)),
                  pl.BlockSpec((1, 1, tn), lambda i, j: (i, 0, j))],
        out_specs=pl.BlockSpec((1, rows, tn), lambda i, j: (i, 0, j)),
        compiler_params=_cparams("parallel", "parallel"), name="adaln_mods",
    )(cond, mod_w, mod_b.reshape(depth, 1, d3))


def _nmm_kernel(*refs, splits, d, gb_heads):
    if gb_heads:
        x_ref, g_ref, mod_ref, w_ref, abp_ref = refs[:5]
        outs = refs[5:]
    else:
        x_ref, g_ref, mod_ref, w_ref = refs[:4]
        outs = refs[4:]
    x = x_ref[0]
    y = x * lax.rsqrt(jnp.mean(x * x, axis=-1, keepdims=True) + EPS) * g_ref[...]
    shift = mod_ref[0, :, 0:d]
    scale = mod_ref[0, :, d:2 * d]
    n = (y * (1.0 + scale) + shift).astype(BF16)
    c0 = 0
    for idx, (o_ref, width) in enumerate(zip(outs, splits)):
        step = min(width, 1024)
        for s in range(0, width, step):
            r = jnp.dot(n, w_ref[:, c0 + s:c0 + s + step], preferred_element_type=F32)
            if gb_heads and idx == len(splits) - 1:
                a_log = abp_ref[0:1, :]
                t = r + abp_ref[1:2, :]
                softplus = jnp.maximum(t, 0.0) + jnp.log(1.0 + jnp.exp(-jnp.abs(t)))
                lane = lax.broadcasted_iota(jnp.int32, r.shape, 1)
                r = jnp.where(lane < 2 * gb_heads, -jnp.exp(a_log) * softplus, jax.nn.sigmoid(r))
            o_ref[0, :, s:s + step] = r.astype(o_ref.dtype)
        c0 += width


def _nmm(x, g, mod, w, splits, dtypes, abp=None, gb_heads=0, tm=512):
    b, l, d = x.shape
    tm = min(tm, l)
    ntot = w.shape[1]
    bm = mod.shape[0]
    mod_map = (lambda i, j: (i, 0, 0)) if bm > 1 else (lambda i, j: (0, 0, 0))
    in_specs = [pl.BlockSpec((1, tm, d), lambda i, j: (i, j, 0)),
                _const_spec((1, d)),
                pl.BlockSpec((1, 1, 3 * d), mod_map),
                _const_spec((d, ntot))]
    args = [x, g.reshape(1, d), mod, w]
    if gb_heads:
        in_specs.append(_const_spec(abp.shape))
        args.append(abp)
    return pl.pallas_call(
        functools.partial(_nmm_kernel, splits=splits, d=d, gb_heads=gb_heads),
        out_shape=[jax.ShapeDtypeStruct((b, l, wd), dt) for wd, dt in zip(splits, dtypes)],
        grid=(b, l // tm),
        in_specs=in_specs,
        out_specs=[pl.BlockSpec((1, tm, wd), lambda i, j: (i, j, 0)) for wd in splits],
        compiler_params=_cparams("parallel", "parallel"), name="norm_proj",
    )(*args)


def _dft_cos_sin(n):
    k = np.arange(n)
    ang = 2.0 * np.pi * ((k[:, None] * k[None, :]) % n) / n
    return np.cos(ang), np.sin(ang)


def _dft1_kernel(m_ref, u_ref, cos_ref, sin_ref, o_ref, *, n1, c, tn2):
    r = jnp.dot(m_ref[...], u_ref[0], preferred_element_type=F32)
    for j in range(tn2):
        yr = r[:n1, j * c:(j + 1) * c]
        yi = r[n1:, j * c:(j + 1) * c]
        cs = cos_ref[j]
        sn = sin_ref[j]
        o_ref[0, 0, j] = (yr * cs + yi * sn).astype(o_ref.dtype)
        o_ref[0, 1, j] = (yi * cs - yr * sn).astype(o_ref.dtype)


def _dft1(u, n1, n2, tn2=4):
    b, l, c = u.shape
    cm, sm = _dft_cos_sin(n1)
    m1 = jnp.asarray(np.concatenate([cm, -sm], axis=0), F32).astype(BF16)
    ang = 2.0 * np.pi * (np.arange(n2)[:, None] * np.arange(n1)[None, :]) / l
    tw_cos = jnp.asarray(np.cos(ang)[:, :, None], F32)
    tw_sin = jnp.asarray(np.sin(ang)[:, :, None], F32)
    return pl.pallas_call(
        functools.partial(_dft1_kernel, n1=n1, c=c, tn2=tn2),
        out_shape=jax.ShapeDtypeStruct((b, 2, n2, n1, c), BF16),
        grid=(b, n2 // tn2),
        in_specs=[_const_spec((2 * n1, n1)),
                  pl.BlockSpec((1, n1, tn2 * c), lambda i, j: (i, 0, j)),
                  pl.BlockSpec((tn2, n1, 1), lambda i, j: (j, 0, 0)),
                  pl.BlockSpec((tn2, n1, 1), lambda i, j: (j, 0, 0))],
        out_specs=pl.BlockSpec((1, 2, tn2, n1, c), lambda i, j: (i, 0, j, 0, 0)),
        compiler_params=_cparams("parallel", "parallel"), name="dft_stage1",
    )(m1, u.reshape(b, n1, n2 * c), tw_cos, tw_sin)


def _dft2_kernel(m_ref, y_ref, xr_ref, xi_ref, *, half):
    r = jnp.dot(m_ref[...], y_ref[0], preferred_element_type=F32)
    xr_ref[0] = r[:half].astype(xr_ref.dtype)
    xi_ref[0] = r[half:].astype(xi_ref.dtype)


def _dft2(m, y, tc=2048):
    b, k, cols = y.shape
    half = m.shape[0] // 2
    return pl.pallas_call(
        functools.partial(_dft2_kernel, half=half),
        out_shape=[jax.ShapeDtypeStruct((b, half, cols), BF16)] * 2,
        grid=(b, cols // tc),
        in_specs=[_const_spec(m.shape),
                  pl.BlockSpec((1, k, tc), lambda i, j: (i, 0, j))],
        out_specs=[pl.BlockSpec((1, half, tc), lambda i, j: (i, 0, j))] * 2,
        compiler_params=_cparams("parallel", "parallel"), name="dft_stage2",
    )(m, y)


def _dft_real_matrix(n):
    cm, sm = _dft_cos_sin(n)
    return jnp.asarray(np.concatenate([cm, -sm], axis=0), F32).astype(BF16)


def _dft_complex_matrix(n):
    cm, sm = _dft_cos_sin(n)
    return jnp.asarray(np.block([[cm, sm], [-sm, cm]]), F32).astype(BF16)


def _grp_kernel(m_ref, w_ref, gc_ref, gs_ref, *, n):
    r = jnp.dot(m_ref[...], w_ref[0], preferred_element_type=F32, precision=HI)
    gc_ref[0] = r[:n].astype(gc_ref.dtype)
    gs_ref[0] = r[n:].astype(gs_ref.dtype)


def _grp(w_grp):
    groups, n, _ = w_grp.shape
    cm, sm = _dft_cos_sin(n)
    m = jnp.asarray(np.concatenate([cm, sm], axis=0), F32)
    return pl.pallas_call(
        functools.partial(_grp_kernel, n=n),
        out_shape=[jax.ShapeDtypeStruct((groups, n, n), BF16)] * 2,
        grid=(groups,),
        in_specs=[_const_spec((2 * n, n)),
                  pl.BlockSpec((1, n, n), lambda g: (g, 0, 0))],
        out_specs=[pl.BlockSpec((1, n, n), lambda g: (g, 0, 0))] * 2,
        compiler_params=_cparams("parallel"), name="group_weights",
    )(m, w_grp)


def _fn_out_kernel(xr_ref, xi_ref, z_ref, res_ref, mod_ref, gc_ref, gs_ref, w_ref, o_ref, *, groups, gd, d, scale):
    acc = None
    for g in range(groups):
        sl = slice(g * gd, (g + 1) * gd)
        f = (jnp.dot(xr_ref[0, :, sl], gc_ref[g], preferred_element_type=F32)
             + jnp.dot(xi_ref[0, :, sl], gs_ref[g], preferred_element_type=F32))
        y = (f * scale * _silu(z_ref[0, :, sl].astype(F32))).astype(BF16)
        p = jnp.dot(y, w_ref[sl, :], preferred_element_type=F32)
        acc = p if acc is None else acc + p
    gate = mod_ref[0, :, 2 * d:3 * d]
    o_ref[0] = res_ref[0] + gate * acc


def _fn_out(xr, xi, z, res, mod, gc, gs, w_out, scale, tm=512):
    b, l, c = xr.shape
    d = res.shape[-1]
    tm = min(tm, l)
    groups, gd, _ = gc.shape
    bm = mod.shape[0]
    mod_map = (lambda i, j: (i, 0, 0)) if bm > 1 else (lambda i, j: (0, 0, 0))
    tok = lambda wd: pl.BlockSpec((1, tm, wd), lambda i, j: (i, j, 0))
    return pl.pallas_call(
        functools.partial(_fn_out_kernel, groups=groups, gd=gd, d=d, scale=scale),
        out_shape=jax.ShapeDtypeStruct((b, l, d), F32),
        grid=(b, l // tm),
        in_specs=[tok(c), tok(c), tok(c), tok(d),
                  pl.BlockSpec((1, 1, 3 * d), mod_map),
                  _const_spec(gc.shape), _const_spec(gs.shape), _const_spec(w_out.shape)],
        out_specs=tok(d),
        compiler_params=_cparams("parallel", "parallel"), name="fourier_out",
    )(xr, xi, z, res, mod, gc, gs, w_out)


def _conv_kernel(prev_ref, main_ref, next_ref, w_ref, ml_ref, mr_ref, o_ref, *, cols, tr, hd, normalize, scale):
    i = pl.program_id(1)
    last = pl.num_programs(1) - 1
    prev = prev_ref[0].astype(F32) * (i > 0).astype(F32)
    nxt = next_ref[0].astype(F32) * (i < last).astype(F32)
    e = jnp.concatenate([prev, main_ref[0].astype(F32), nxt], axis=0)
    t_all = e.shape[0]
    shifted = (pltpu.roll(e, 1, axis=0) * ml_ref[...], e, pltpu.roll(e, t_all - 1, axis=0) * mr_ref[...])
    n = tr * cols
    acc = None
    for dr in range(3):
        for dc in range(3):
            term = shifted[dc][dr * cols:dr * cols + n] * w_ref[dr, dc:dc + 1, :]
            acc = term if acc is None else acc + term
    y = _silu(acc)
    for j in range(y.shape[1] // hd):
        yh = y[:, j * hd:(j + 1) * hd]
        if normalize:
            yh = yh * lax.rsqrt(jnp.sum(yh * yh, axis=-1, keepdims=True) + EPS) * scale
        o_ref[0, j] = yh.astype(o_ref.dtype)


def _conv(qkv, w, rows, cols, ch_start, heads, hd, normalize, scale, out_dtype, cb=512):
    b, l, ctot = qkv.shape
    tr = min(rows, max(1, 512 // cols))
    n = tr * cols
    t_all = (tr + 2) * cols
    cofs = ch_start // cb
    ncb = heads * hd // cb
    col = np.arange(t_all) % cols
    ml = jnp.asarray(np.broadcast_to((col != 0)[:, None], (t_all, cb)), F32)
    mr = jnp.asarray(np.broadcast_to((col != cols - 1)[:, None], (t_all, cb)), F32)
    return pl.pallas_call(
        functools.partial(_conv_kernel, cols=cols, tr=tr, hd=hd, normalize=normalize, scale=scale),
        out_shape=jax.ShapeDtypeStruct((b, heads, l, hd), out_dtype),
        grid=(b, rows // tr, ncb),
        in_specs=[pl.BlockSpec((1, cols, cb), lambda bi, i, c: (bi, jnp.maximum(i * tr - 1, 0), cofs + c)),
                  pl.BlockSpec((1, n, cb), lambda bi, i, c: (bi, i, cofs + c)),
                  pl.BlockSpec((1, cols, cb), lambda bi, i, c: (bi, jnp.minimum((i + 1) * tr, rows - 1), cofs + c)),
                  pl.BlockSpec((3, 3, cb), lambda bi, i, c: (0, 0, cofs + c)),
                  _const_spec((t_all, cb)), _const_spec((t_all, cb))],
        out_specs=pl.BlockSpec((1, cb // hd, n, hd), lambda bi, i, c: (bi, c, i, 0)),
        compiler_params=_cparams("parallel", "parallel", "parallel"), name="conv_silu",
    )(qkv, qkv, qkv, w, ml, mr)


def _scan_kernel(qf_ref, kf_ref, vf_ref, gf_ref, qb_ref, kb_ref, vb_ref, gb_ref, s0_ref,
                 of_ref, ob_ref, sfin_ref, s_scr, *, nheads, cs):
    h = pl.program_id(1)
    n = pl.program_id(2)

    @pl.when(n == 0)
    def _():
        s_scr[...] = s0_ref[0, 0]

    ri = lax.broadcasted_iota(jnp.int32, (cs, cs), 0)
    ci = lax.broadcasted_iota(jnp.int32, (cs, cs), 1)
    eye = ri == ci
    lane = lax.broadcasted_iota(jnp.int32, (cs, GB_LANES), 1)
    nt = (((1,), (1,)), ((), ()))
    tn = (((0,), (0,)), ((), ()))
    levels = max(int(math.ceil(math.log2(cs))) - 1, 0)
    dirs = ((qf_ref, kf_ref, vf_ref, gf_ref, of_ref), (qb_ref, kb_ref, vb_ref, gb_ref, ob_ref))
    for d, (q_ref, k_ref, v_ref, g_ref, o_ref) in enumerate(dirs):
        incl = (ri >= ci) if d == 0 else (ri <= ci)
        strict = (ri > ci) if d == 0 else (ri < ci)
        cum = (ri <= ci) if d == 0 else (ri >= ci)
        gbv = g_ref[0]
        g_col = jnp.sum(jnp.where(lane == d * nheads + h, gbv, 0.0), axis=1, keepdims=True)
        b_col = jnp.sum(jnp.where(lane == (2 + d) * nheads + h, gbv, 0.0), axis=1, keepdims=True)
        gc_row = jnp.sum(jnp.where(cum, jnp.broadcast_to(g_col, (cs, cs)), 0.0), axis=0, keepdims=True)
        gc_col = jnp.sum(jnp.where(eye, jnp.broadcast_to(gc_row, (cs, cs)), 0.0), axis=1, keepdims=True)
        decay = jnp.where(incl, jnp.exp(jnp.where(incl, gc_col - gc_row, 0.0)), 0.0)
        q = q_ref[0, 0].astype(F32)
        k = k_ref[0, 0].astype(F32)
        v = v_ref[0, 0].astype(F32)
        kb = k * b_col
        kk = lax.dot_general(kb, k, nt, preferred_element_type=F32, precision=HI)
        nm = jnp.where(strict, kk * decay, 0.0)
        p = jnp.where(eye, 1.0, 0.0) - nm
        m = nm
        for _ in range(levels):
            m = jnp.dot(m, m, preferred_element_type=F32, precision=HI)
            p = p + jnp.dot(p, m, preferred_element_type=F32, precision=HI)
        egc = jnp.exp(gc_col)
        g_tot = jnp.sum(g_col, axis=0, keepdims=True)
        u = jnp.dot(p, v * b_col, preferred_element_type=F32, precision=HI)
        w = jnp.dot(p, kb * egc, preferred_element_type=F32, precision=HI)
        kt = k * jnp.exp(g_tot - gc_col)
        qh = q * egc
        qk = lax.dot_general(q.astype(BF16), k.astype(BF16), nt, preferred_element_type=F32) * decay
        s = s_scr[d]
        sb = s.astype(BF16)
        v_new = u - jnp.dot(w.astype(BF16), sb, preferred_element_type=F32)
        vb = v_new.astype(BF16)
        o = (jnp.dot(qh.astype(BF16), sb, preferred_element_type=F32)
             + jnp.dot(qk.astype(BF16), vb, preferred_element_type=F32))
        s_scr[d] = s * jnp.exp(g_tot) + lax.dot_general(kt.astype(BF16), vb, tn, preferred_element_type=F32)
        o_ref[0, 0] = o.astype(o_ref.dtype)

    @pl.when(n == pl.num_programs(2) - 1)
    def _():
        sfin_ref[0, 0] = s_scr[...]


def _scan(q, k, v, gb, s0, cs=SCAN_CHUNK):
    b, nh, l, dk = q.shape
    dv = v.shape[-1]
    nc = l // cs
    fwd = lambda wd: pl.BlockSpec((1, 1, cs, wd), lambda bi, hi, n: (bi, hi, n, 0))
    bwd = lambda wd: pl.BlockSpec((1, 1, cs, wd), lambda bi, hi, n: (bi, hi, nc - 1 - n, 0))
    st = pl.BlockSpec((1, 1, 2, dk, dv), lambda bi, hi, n: (bi, hi, 0, 0, 0))
    return pl.pallas_call(
        functools.partial(_scan_kernel, nheads=nh, cs=cs),
        out_shape=[jax.ShapeDtypeStruct((b, nh, l, dv), F32),
                   jax.ShapeDtypeStruct((b, nh, l, dv), F32),
                   jax.ShapeDtypeStruct((b, nh, 2, dk, dv), F32)],
        grid=(b, nh, nc),
        in_specs=[fwd(dk), fwd(dk), fwd(dv), pl.BlockSpec((1, cs, GB_LANES), lambda bi, hi, n: (bi, n, 0)),
                  bwd(dk), bwd(dk), bwd(dv), pl.BlockSpec((1, cs, GB_LANES), lambda bi, hi, n: (bi, nc - 1 - n, 0)),
                  st],
        out_specs=[fwd(dv), bwd(dv), st],
        scratch_shapes=[pltpu.VMEM((2, dk, dv), F32)],
        compiler_params=_cparams("parallel", "parallel", "arbitrary"), name="delta_scan",
    )(q, k, v, gb, q, k, v, gb, s0)


def _dn_out_kernel(of_ref, ob_ref, z_ref, res_ref, mod_ref, ng_ref, w_ref, fg_ref, o_ref, *, nheads, dv, d):
    acc = None
    for hd in range(nheads):
        sl = slice(hd * dv, (hd + 1) * dv)
        o = of_ref[0, hd].astype(F32) + ob_ref[0, hd].astype(F32)
        o = o * lax.rsqrt(jnp.mean(o * o, axis=-1, keepdims=True) + EPS) * ng_ref[...]
        y = (o * _silu(z_ref[0, :, sl].astype(F32))).astype(BF16)
        p = jnp.dot(y, w_ref[sl, :], preferred_element_type=F32)
        acc = p if acc is None else acc + p
    gate = mod_ref[0, :, 2 * d:3 * d]
    hn = res_ref[0] + gate * acc
    o_ref[0] = hn * lax.rsqrt(jnp.mean(hn * hn, axis=-1, keepdims=True) + EPS) * fg_ref[...]


def _dn_out(o_f, o_b, z, res, mod, norm_g, w_out, final_g, tm=512):
    b, nh, l, dv = o_f.shape
    d = res.shape[-1]
    tm = min(tm, l)
    tok = lambda wd: pl.BlockSpec((1, tm, wd), lambda i, j: (i, j, 0))
    hs = pl.BlockSpec((1, nh, tm, dv), lambda i, j: (i, 0, j, 0))
    return pl.pallas_call(
        functools.partial(_dn_out_kernel, nheads=nh, dv=dv, d=d),
        out_shape=jax.ShapeDtypeStruct((b, l, d), F32),
        grid=(b, l // tm),
        in_specs=[hs, hs, tok(nh * dv), tok(d),
                  pl.BlockSpec((1, 1, 3 * d), lambda i, j: (i, 0, 0)),
                  _const_spec((1, dv)), _const_spec(w_out.shape), _const_spec((1, d))],
        out_specs=tok(d),
        compiler_params=_cparams("parallel", "parallel"), name="delta_out",
    )(o_f, o_b, z, res, mod, norm_g.reshape(1, dv), w_out, final_g.reshape(1, d))


def _fourier_layer(h_lat, h_ctx, norm_g, mod_lat, mod_ctx, w_in, w_grp, w_out):
    b, l, _ = h_lat.shape
    lc = h_ctx.shape[1]
    width = w_in.shape[1] // 2
    gd = w_grp.shape[-1]
    w_in = w_in.astype(BF16)
    w_out = w_out.astype(BF16)
    gc, gs = _grp(w_grp)

    u, z = _nmm(h_lat, norm_g, mod_lat, w_in, (width, width), (BF16, BF16))
    n2 = DFT_N2
    n1 = l // n2
    y = _dft1(u, n1, n2)
    xr, xi = _dft2(_dft_complex_matrix(n2), y.reshape(b, 2 * n2, n1 * width))
    xr = xr.reshape(b, l, width)
    xi = xi.reshape(b, l, width)
    out_lat = _fn_out(xr, xi, z, h_lat, mod_lat, gc, gs, w_out, 1.0 / math.sqrt(l * gd))

    uc, zc = _nmm(h_ctx, norm_g, mod_ctx, w_in, (width, width), (BF16, BF16))
    xrc, xic = _dft2(_dft_real_matrix(lc), uc)
    out_ctx = _fn_out(xrc, xic, zc, h_ctx, mod_ctx, gc, gs, w_out, 1.0 / math.sqrt(lc * gd))
    return out_lat, out_ctx


def _delta_project(h, norm_g, mod, w_cat, abp, conv_w, rows, cols):
    qk_w = DN_HEADS * DN_DK
    v_w = DN_HEADS * DN_DV
    qkv, z, gb = _nmm(h, norm_g, mod, w_cat, (2 * qk_w + v_w, v_w, GB_LANES), (BF16, BF16, F32),
                      abp=abp, gb_heads=DN_HEADS)
    q = _conv(qkv, conv_w, rows, cols, 0, DN_HEADS, DN_DK, True, DN_DK ** -0.5, F32)
    k = _conv(qkv, conv_w, rows, cols, qk_w, DN_HEADS, DN_DK, True, 1.0, F32)
    v = _conv(qkv, conv_w, rows, cols, 2 * qk_w, DN_HEADS, DN_DV, False, 1.0, F32)
    return q, k, v, z, gb


def _delta_layer_final(h_lat, h_ctx, norm_g, mod_lat, mod_ctx, w_in, conv_w, a_log, dt_bias, dn_norm_g, w_out,
                       final_g):
    b, l, d = h_lat.shape
    lc = h_ctx.shape[1]
    main_w = 2 * DN_HEADS * DN_DK + 2 * DN_HEADS * DN_DV
    n_ab = 4 * DN_HEADS
    w_cat = jnp.concatenate(
        [w_in[:, :main_w], w_in[:, main_w:], jnp.zeros((d, GB_LANES - n_ab), w_in.dtype)], axis=1).astype(BF16)
    pad = jnp.zeros((GB_LANES - 2 * DN_HEADS,), F32)
    abp = jnp.stack([jnp.concatenate([a_log.reshape(-1).astype(F32), pad]),
                     jnp.concatenate([dt_bias.reshape(-1).astype(F32), pad])])
    qc, kc, vc, _, gbc = _delta_project(h_ctx, norm_g, mod_ctx, w_cat, abp, conv_w, 1, lc)
    ql, kl, vl, zl, gbl = _delta_project(h_lat, norm_g, mod_lat, w_cat, abp, conv_w, l // GRID_W, GRID_W)
    s0 = jnp.zeros((b, DN_HEADS, 2, DN_DK, DN_DV), F32)
    _, _, s_ctx = _scan(qc, kc, vc, gbc, s0)
    o_f, o_b, _ = _scan(ql, kl, vl, gbl, s_ctx)
    return _dn_out(o_f, o_b, zl, h_lat, mod_lat, dn_norm_g, w_out.astype(BF16), final_g)


def kernel(x, c, ctx, c_ctx, mod_w, mod_b, norm_g, final_g, fn_w_in, fn_w_grp, fn_w_out,
           dn_w_in, dn_conv, dn_a_log, dn_dt_bias, dn_norm_g, dn_w_out):
    b, l, d = x.shape
    cond = jnp.concatenate([c, c_ctx[None, :], jnp.zeros((8 - b - 1, d), c.dtype)], axis=0)
    mods = _mods(cond, mod_w, mod_b)
    mod_lat = [mods[i, :b].reshape(b, 1, 3 * d) for i in range(2)]
    mod_ctx = [mods[i, b:b + 1].reshape(1, 1, 3 * d) for i in range(2)]
    h_lat, h_ctx = _fourier_layer(x, ctx, norm_g[0], mod_lat[0], mod_ctx[0], fn_w_in[0], fn_w_grp[0], fn_w_out[0])
    return _delta_layer_final(h_lat, h_ctx, norm_g[1], mod_lat[1], mod_ctx[1], dn_w_in[0], dn_conv[0],
                              dn_a_log[0], dn_dt_bias[0], dn_norm_g[0], dn_w_out[0], final_g)
```

```python
import functools
import math

import numpy as np
import jax
import jax.numpy as jnp
from jax import lax
from jax.experimental import pallas as pl
from jax.experimental.pallas import tpu as pltpu

F32 = jnp.float32
BF16 = jnp.bfloat16
HI = lax.Precision.HIGHEST
EPS = 1e-6

GRID_W = 64
FN_GROUPS = 8
DN_HEADS = 8
DN_DK = 128
DN_DV = 256
SCAN_CHUNK = 128
DFT_N2 = 128
GB_LANES = 128
VMEM_LIMIT_BYTES = 48 * 1024 * 1024


def _cparams(*sem):
    return pltpu.CompilerParams(dimension_semantics=sem, vmem_limit_bytes=VMEM_LIMIT_BYTES)


def _const_spec(shape):
    zeros = (0,) * len(shape)
    return pl.BlockSpec(shape, lambda *_: zeros)


def _silu(v):
    return v * jax.nn.sigmoid(v)


def _mods_kernel(cond_ref, w_ref, b_ref, o_ref):
    s = _silu(cond_ref[...])
    o_ref[0] = jnp.dot(s, w_ref[0], preferred_element_type=F32, precision=HI) + b_ref[0]


def _mods(cond, mod_w, mod_b):
    depth, d, d3 = mod_w.shape
    rows = cond.shape[0]
    tn = 512
    return pl.pallas_call(
        _mods_kernel,
        out_shape=jax.ShapeDtypeStruct((depth, rows, d3), F32),
        grid=(depth, d3 // tn),
        in_specs=[pl.BlockSpec((rows, d), lambda i, j: (0, 0)),
                  pl.BlockSpec((1, d, tn), lambda i, j: (i, 0, j)),
                  pl.BlockSpec((1, 1, tn), lambda i, j: (i, 0, j))],
        out_specs=pl.BlockSpec((1, rows, tn), lambda i, j: (i, 0, j)),
        compiler_params=_cparams("parallel", "parallel"), name="adaln_mods",
    )(cond, mod_w, mod_b.reshape(depth, 1, d3))


def _nmm_kernel(*refs, splits, d, gb_heads):
    if gb_heads:
        x_ref, g_ref, mod_ref, w_ref, abp_ref = refs[:5]
        outs = refs[5:]
    else:
        x_ref, g_ref, mod_ref, w_ref = refs[:4]
        outs = refs[4:]
    x = x_ref[0]
    y = x * lax.rsqrt(jnp.mean(x * x, axis=-1, keepdims=True) + EPS) * g_ref[...]
    shift = mod_ref[0, :, 0:d]
    scale = mod_ref[0, :, d:2 * d]
    n = (y * (1.0 + scale) + shift).astype(BF16)
    c0 = 0
    for idx, (o_ref, width) in enumerate(zip(outs, splits)):
        step = min(width, 1024)
        for s in range(0, width, step):
            r = jnp.dot(n, w_ref[:, c0 + s:c0 + s + step], preferred_element_type=F32)
            if gb_heads and idx == len(splits) - 1:
                a_log = abp_ref[0:1, :]
                t = r + abp_ref[1:2, :]
                softplus = jnp.maximum(t, 0.0) + jnp.log(1.0 + jnp.exp(-jnp.abs(t)))
                lane = lax.broadcasted_iota(jnp.int32, r.shape, 1)
                r = jnp.where(lane < 2 * gb_heads, -jnp.exp(a_log) * softplus, jax.nn.sigmoid(r))
            o_ref[0, :, s:s + step] = r.astype(o_ref.dtype)
        c0 += width


def _nmm(x, g, mod, w, splits, dtypes, abp=None, gb_heads=0, tm=512):
    b, l, d = x.shape
    tm = min(tm, l)
    ntot = w.shape[1]
    bm = mod.shape[0]
    mod_map = (lambda i, j: (i, 0, 0)) if bm > 1 else (lambda i, j: (0, 0, 0))
    in_specs = [pl.BlockSpec((1, tm, d), lambda i, j: (i, j, 0)),
                _const_spec((1, d)),
                pl.BlockSpec((1, 1, 3 * d), mod_map),
                _const_spec((d, ntot))]
    args = [x, g.reshape(1, d), mod, w]
    if gb_heads:
        in_specs.append(_const_spec(abp.shape))
        args.append(abp)
    return pl.pallas_call(
        functools.partial(_nmm_kernel, splits=splits, d=d, gb_heads=gb_heads),
        out_shape=[jax.ShapeDtypeStruct((b, l, wd), dt) for wd, dt in zip(splits, dtypes)],
        grid=(b, l // tm),
        in_specs=in_specs,
        out_specs=[pl.BlockSpec((1, tm, wd), lambda i, j: (i, j, 0)) for wd in splits],
        compiler_params=_cparams("parallel", "parallel"), name="norm_proj",
    )(*args)


def _dft_cos_sin(n):
    k = np.arange(n)
    ang = 2.0 * np.pi * ((k[:, None] * k[None, :]) % n) / n
    return np.cos(ang), np.sin(ang)


def _dft1_kernel(m_ref, u_ref, cos_ref, sin_ref, o_ref, *, n1, c, tn2):
    r = jnp.dot(m_ref[...], u_ref[0], preferred_element_type=F32)
    for j in range(tn2):
        yr = r[:n1, j * c:(j + 1) * c]
        yi = r[n1:, j * c:(j + 1) * c]
        cs = cos_ref[j]
        sn = sin_ref[j]
        o_ref[0, 0, j] = (yr * cs + yi * sn).astype(o_ref.dtype)
        o_ref[0, 1, j] = (yi * cs - yr * sn).astype(o_ref.dtype)


def _dft1(u, n1, n2, tn2=4):
    b, l, c = u.shape
    cm, sm = _dft_cos_sin(n1)
    m1 = jnp.asarray(np.concatenate([cm, -sm], axis=0), F32).astype(BF16)
    ang = 2.0 * np.pi * (np.arange(n2)[:, None] * np.arange(n1)[None, :]) / l
    tw_cos = jnp.asarray(np.cos(ang)[:, :, None], F32)
    tw_sin = jnp.asarray(np.sin(ang)[:, :, None], F32)
    return pl.pallas_call(
        functools.partial(_dft1_kernel, n1=n1, c=c, tn2=tn2),
        out_shape=jax.ShapeDtypeStruct((b, 2, n2, n1, c), BF16),
        grid=(b, n2 // tn2),
        in_specs=[_const_spec((2 * n1, n1)),
                  pl.BlockSpec((1, n1, tn2 * c), lambda i, j: (i, 0, j)),
                  pl.BlockSpec((tn2, n1, 1), lambda i, j: (j, 0, 0)),
                  pl.BlockSpec((tn2, n1, 1), lambda i, j: (j, 0, 0))],
        out_specs=pl.BlockSpec((1, 2, tn2, n1, c), lambda i, j: (i, 0, j, 0, 0)),
        compiler_params=_cparams("parallel", "parallel"), name="dft_stage1",
    )(m1, u.reshape(b, n1, n2 * c), tw_cos, tw_sin)


def _dft2_kernel(m_ref, y_ref, xr_ref, xi_ref, *, half):
    r = jnp.dot(m_ref[...], y_ref[0], preferred_element_type=F32)
    xr_ref[0] = r[:half].astype(xr_ref.dtype)
    xi_ref[0] = r[half:].astype(xi_ref.dtype)


def _dft2(m, y, tc=2048):
    b, k, cols = y.shape
    half = m.shape[0] // 2
    return pl.pallas_call(
        functools.partial(_dft2_kernel, half=half),
        out_shape=[jax.ShapeDtypeStruct((b, half, cols), BF16)] * 2,
        grid=(b, cols // tc),
        in_specs=[_const_spec(m.shape),
                  pl.BlockSpec((1, k, tc), lambda i, j: (i, 0, j))],
        out_specs=[pl.BlockSpec((1, half, tc), lambda i, j: (i, 0, j))] * 2,
        compiler_params=_cparams("parallel", "parallel"), name="dft_stage2",
    )(m, y)


def _dft_real_matrix(n):
    cm, sm = _dft_cos_sin(n)
    return jnp.asarray(np.concatenate([cm, -sm], axis=0), F32).astype(BF16)


def _dft_complex_matrix(n):
    cm, sm = _dft_cos_sin(n)
    return jnp.asarray(np.block([[cm, sm], [-sm, cm]]), F32).astype(BF16)


def _grp_kernel(m_ref, w_ref, gc_ref, gs_ref, *, n):
    r = jnp.dot(m_ref[...], w_ref[0], preferred_element_type=F32, precision=HI)
    gc_ref[0] = r[:n].astype(gc_ref.dtype)
    gs_ref[0] = r[n:].astype(gs_ref.dtype)


def _grp(w_grp):
    groups, n, _ = w_grp.shape
    cm, sm = _dft_cos_sin(n)
    m = jnp.asarray(np.concatenate([cm, sm], axis=0), F32)
    return pl.pallas_call(
        functools.partial(_grp_kernel, n=n),
        out_shape=[jax.ShapeDtypeStruct((groups, n, n), BF16)] * 2,
        grid=(groups,),
        in_specs=[_const_spec((2 * n, n)),
                  pl.BlockSpec((1, n, n), lambda g: (g, 0, 0))],
        out_specs=[pl.BlockSpec((1, n, n), lambda g: (g, 0, 0))] * 2,
        compiler_params=_cparams("parallel"), name="group_weights",
    )(m, w_grp)


def _fn_out_kernel(xr_ref, xi_ref, z_ref, res_ref, mod_ref, gc_ref, gs_ref, w_ref, o_ref, *, groups, gd, d, scale):
    acc = None
    for g in range(groups):
        sl = slice(g * gd, (g + 1) * gd)
        f = (jnp.dot(xr_ref[0, :, sl], gc_ref[g], preferred_element_type=F32)
             + jnp.dot(xi_ref[0, :, sl], gs_ref[g], preferred_element_type=F32))
        y = (f * scale * _silu(z_ref[0, :, sl].astype(F32))).astype(BF16)
        p = jnp.dot(y, w_ref[sl, :], preferred_element_type=F32)
        acc = p if acc is None else acc + p
    gate = mod_ref[0, :, 2 * d:3 * d]
    o_ref[0] = res_ref[0] + gate * acc


def _fn_out(xr, xi, z, res, mod, gc, gs, w_out, scale, tm=512):
    b, l, c = xr.shape
    d = res.shape[-1]
    tm = min(tm, l)
    groups, gd, _ = gc.shape
    bm = mod.shape[0]
    mod_map = (lambda i, j: (i, 0, 0)) if bm > 1 else (lambda i, j: (0, 0, 0))
    tok = lambda wd: pl.BlockSpec((1, tm, wd), lambda i, j: (i, j, 0))
    return pl.pallas_call(
        functools.partial(_fn_out_kernel, groups=groups, gd=gd, d=d, scale=scale),
        out_shape=jax.ShapeDtypeStruct((b, l, d), F32),
        grid=(b, l // tm),
        in_specs=[tok(c), tok(c), tok(c), tok(d),
                  pl.BlockSpec((1, 1, 3 * d), mod_map),
                  _const_spec(gc.shape), _const_spec(gs.shape), _const_spec(w_out.shape)],
        out_specs=tok(d),
        compiler_params=_cparams("parallel", "parallel"), name="fourier_out",
    )(xr, xi, z, res, mod, gc, gs, w_out)


def _conv_kernel(prev_ref, main_ref, next_ref, w_ref, ml_ref, mr_ref, o_ref, *, cols, tr, hd, normalize, scale):
    i = pl.program_id(1)
    last = pl.num_programs(1) - 1
    prev = prev_ref[0].astype(F32) * (i > 0).astype(F32)
    nxt = next_ref[0].astype(F32) * (i < last).astype(F32)
    e = jnp.concatenate([prev, main_ref[0].astype(F32), nxt], axis=0)
    t_all = e.shape[0]
    shifted = (pltpu.roll(e, 1, axis=0) * ml_ref[...], e, pltpu.roll(e, t_all - 1, axis=0) * mr_ref[...])
    n = tr * cols
    acc = None
    for dr in range(3):
        for dc in range(3):
            term = shifted[dc][dr * cols:dr * cols + n] * w_ref[dr, dc:dc + 1, :]
            acc = term if acc is None else acc + term
    y = _silu(acc)
    for j in range(y.shape[1] // hd):
        yh = y[:, j * hd:(j + 1) * hd]
        if normalize:
            yh = yh * lax.rsqrt(jnp.sum(yh * yh, axis=-1, keepdims=True) + EPS) * scale
        o_ref[0, j] = yh.astype(o_ref.dtype)


def _conv(qkv, w, rows, cols, ch_start, heads, hd, normalize, scale, out_dtype, cb=512):
    b, l, ctot = qkv.shape
    tr = min(rows, max(1, 512 // cols))
    n = tr * cols
    t_all = (tr + 2) * cols
    cofs = ch_start // cb
    ncb = heads * hd // cb
    col = np.arange(t_all) % cols
    ml = jnp.asarray(np.broadcast_to((col != 0)[:, None], (t_all, cb)), F32)
    mr = jnp.asarray(np.broadcast_to((col != cols - 1)[:, None], (t_all, cb)), F32)
    return pl.pallas_call(
        functools.partial(_conv_kernel, cols=cols, tr=tr, hd=hd, normalize=normalize, scale=scale),
        out_shape=jax.ShapeDtypeStruct((b, heads, l, hd), out_dtype),
        grid=(b, rows // tr, ncb),
        in_specs=[pl.BlockSpec((1, cols, cb), lambda bi, i, c: (bi, jnp.maximum(i * tr - 1, 0), cofs + c)),
                  pl.BlockSpec((1, n, cb), lambda bi, i, c: (bi, i, cofs + c)),
                  pl.BlockSpec((1, cols, cb), lambda bi, i, c: (bi, jnp.minimum((i + 1) * tr, rows - 1), cofs + c)),
                  pl.BlockSpec((3, 3, cb), lambda bi, i, c: (0, 0, cofs + c)),
                  _const_spec((t_all, cb)), _const_spec((t_all, cb))],
        out_specs=pl.BlockSpec((1, cb // hd, n, hd), lambda bi, i, c: (bi, c, i, 0)),
        compiler_params=_cparams("parallel", "parallel", "parallel"), name="conv_silu",
    )(qkv, qkv, qkv, w, ml, mr)


def _prep_kernel(q_ref, k_ref, v_ref, g_ref, wq_ref, a2_ref, u_ref, eg_ref, *, nheads, cs, nck):
    h = pl.program_id(1)
    ri = lax.broadcasted_iota(jnp.int32, (cs, cs), 0)
    ci = lax.broadcasted_iota(jnp.int32, (cs, cs), 1)
    eye = ri == ci
    xor = ri ^ ci
    lane = lax.broadcasted_iota(jnp.int32, (cs, GB_LANES), 1)
    nt = (((1,), (1,)), ((), ()))
    levels = int(math.log2(cs))
    chains = []
    for c in range(nck):
        rows = slice(c * cs, (c + 1) * cs)
        q = q_ref[0, 0, rows, :].astype(F32)
        k = k_ref[0, 0, rows, :].astype(F32)
        v = v_ref[0, 0, rows, :].astype(F32)
        kbf = k.astype(BF16)
        kk = lax.dot_general(kbf, kbf, nt, preferred_element_type=F32)
        qk = lax.dot_general(q.astype(BF16), kbf, nt, preferred_element_type=F32)
        k_t = k.T
        gbv = g_ref[0, rows, :]
        for d in range(2):
            incl = (ri >= ci) if d == 0 else (ri <= ci)
            strict = (ri > ci) if d == 0 else (ri < ci)
            cum = (ri <= ci) if d == 0 else (ri >= ci)
            g_col = jnp.sum(jnp.where(lane == d * nheads + h, gbv, 0.0), axis=1, keepdims=True)
            b_col = jnp.sum(jnp.where(lane == (2 + d) * nheads + h, gbv, 0.0), axis=1, keepdims=True)
            gc_row = jnp.sum(jnp.where(cum, jnp.broadcast_to(g_col, (cs, cs)), 0.0), axis=0, keepdims=True)
            gc_col = jnp.sum(jnp.where(eye, jnp.broadcast_to(gc_row, (cs, cs)), 0.0), axis=1, keepdims=True)
            decay = jnp.where(incl, jnp.exp(jnp.where(incl, gc_col - gc_row, 0.0)), 0.0)
            egc = jnp.exp(gc_col)
            g_tot = jnp.sum(g_col, axis=0, keepdims=True)
            wq_ref[0, 0, d, c, cs:, :] = (q * egc).astype(wq_ref.dtype)
            a2_ref[0, 0, d, c, :cs, :] = (qk * decay).astype(a2_ref.dtype)
            a2_ref[0, 0, d, c, cs:, :] = (k_t * jnp.exp(g_tot - gc_row)).astype(a2_ref.dtype)
            eg_ref[0, 0, d, c] = jnp.broadcast_to(jnp.exp(g_tot), eg_ref.shape[-2:])
            rhs = jnp.concatenate([v * b_col, k * (b_col * egc)], axis=1).astype(BF16)
            chains.append((d, c, jnp.where(strict, kk * b_col * decay, 0.0), rhs))
    dmats = [jnp.where(eye, 1.0, 0.0) - jnp.where(xor == 1, a_off, 0.0) for _, _, a_off, _ in chains]
    for lv in range(1, levels):
        sel = (xor >> lv) == 1
        des = [jnp.dot(dm.astype(BF16), jnp.where(sel, ch[2], 0.0).astype(BF16), preferred_element_type=F32)
               for dm, ch in zip(dmats, chains)]
        dmats = [dm - jnp.dot(de.astype(BF16), dm.astype(BF16), preferred_element_type=F32)
                 for dm, de in zip(dmats, des)]
    sols = [jnp.dot(dm.astype(BF16), ch[3], preferred_element_type=F32) for dm, ch in zip(dmats, chains)]
    dv = u_ref.shape[-1]
    for (d, c, _, _), sol in zip(chains, sols):
        u_ref[0, 0, d, c] = sol[:, :dv].astype(u_ref.dtype)
        wq_ref[0, 0, d, c, :cs, :] = sol[:, dv:].astype(wq_ref.dtype)


def _prep(q, k, v, gb, cs=SCAN_CHUNK, nck=4):
    b, nh, l, dk = q.shape
    dv = v.shape[-1]
    nc = l // cs
    nck = min(nck, nc)
    tl = nck * cs
    tok = lambda wd: pl.BlockSpec((1, 1, tl, wd), lambda bi, hi, n: (bi, hi, n, 0))
    out = lambda r, wd: pl.BlockSpec((1, 1, 2, nck, r, wd), lambda bi, hi, n: (bi, hi, 0, n, 0, 0))
    return pl.pallas_call(
        functools.partial(_prep_kernel, nheads=nh, cs=cs, nck=nck),
        out_shape=[jax.ShapeDtypeStruct((b, nh, 2, nc, 2 * cs, dk), BF16),
                   jax.ShapeDtypeStruct((b, nh, 2, nc, cs + dk, cs), BF16),
                   jax.ShapeDtypeStruct((b, nh, 2, nc, cs, dv), BF16),
                   jax.ShapeDtypeStruct((b, nh, 2, nc, 8, 128), F32)],
        grid=(b, nh, nc // nck),
        in_specs=[tok(dk), tok(dk), tok(dv), pl.BlockSpec((1, tl, GB_LANES), lambda bi, hi, n: (bi, n, 0))],
        out_specs=[out(2 * cs, dk), out(cs + dk, cs), out(cs, dv), out(8, 128)],
        compiler_params=_cparams("parallel", "parallel", "parallel"), name="delta_prep",
    )(q, k, v, gb)


def _scan_kernel(wqf_ref, a2f_ref, uf_ref, egf_ref, wqb_ref, a2b_ref, ub_ref, egb_ref, s0_ref,
                 of_ref, ob_ref, sfin_ref, s_scr, *, hb, cs):
    n = pl.program_id(2)

    @pl.when(n == 0)
    def _():
        s_scr[...] = s0_ref[0]

    dirs = ((wqf_ref, a2f_ref, uf_ref, egf_ref, of_ref), (wqb_ref, a2b_ref, ub_ref, egb_ref, ob_ref))
    for j in range(hb):
        for d, (wq_ref, a2_ref, u_ref, eg_ref, o_ref) in enumerate(dirs):
            s = s_scr[j, d]
            r1 = jnp.dot(wq_ref[0, j, 0, 0], s.astype(BF16), preferred_element_type=F32)
            v_new = u_ref[0, j, 0, 0].astype(F32) - r1[:cs]
            r2 = jnp.dot(a2_ref[0, j, 0, 0], v_new.astype(BF16), preferred_element_type=F32)
            o_ref[0, j] = (r1[cs:] + r2[:cs]).astype(o_ref.dtype)
            s_scr[j, d] = s * eg_ref[0, j, 0, 0][0:1, 0:1] + r2[cs:]

    @pl.when(n == pl.num_programs(2) - 1)
    def _():
        sfin_ref[0] = s_scr[...]


def _scan(wq, a2, u, eg, s0, hb=4):
    b, nh, _, nc, cs, dv = u.shape
    dk = wq.shape[-1]
    spec = lambda r, wd, d, rev: pl.BlockSpec(
        (1, hb, 1, 1, r, wd), lambda bi, hi, n: (bi, hi, d, (nc - 1 - n) if rev else n, 0, 0))
    both = lambda r, wd: [spec(r, wd, 0, False), spec(r, wd, 1, True)]
    ins = [both(2 * cs, dk), both(cs + dk, cs), both(cs, dv), both(8, 128)]
    st = pl.BlockSpec((1, hb, 2, dk, dv), lambda bi, hi, n: (bi, hi, 0, 0, 0))
    return pl.pallas_call(
        functools.partial(_scan_kernel, hb=hb, cs=cs),
        out_shape=[jax.ShapeDtypeStruct((b, nh, nc * cs, dv), F32),
                   jax.ShapeDtypeStruct((b, nh, nc * cs, dv), F32),
                   jax.ShapeDtypeStruct((b, nh, 2, dk, dv), F32)],
        grid=(b, nh // hb, nc),
        in_specs=[s[0] for s in ins] + [s[1] for s in ins] + [st],
        out_specs=[pl.BlockSpec((1, hb, cs, dv), lambda bi, hi, n: (bi, hi, n, 0)),
                   pl.BlockSpec((1, hb, cs, dv), lambda bi, hi, n: (bi, hi, nc - 1 - n, 0)),
                   st],
        scratch_shapes=[pltpu.VMEM((hb, 2, dk, dv), F32)],
        compiler_params=_cparams("parallel", "parallel", "arbitrary"), name="delta_scan",
    )(wq, a2, u, eg, wq, a2, u, eg, s0)


def _dn_out_kernel(of_ref, ob_ref, z_ref, res_ref, mod_ref, ng_ref, w_ref, fg_ref, o_ref, *, nheads, dv, d):
    acc = None
    for hd in range(nheads):
        sl = slice(hd * dv, (hd + 1) * dv)
        o = of_ref[0, hd].astype(F32) + ob_ref[0, hd].astype(F32)
        o = o * lax.rsqrt(jnp.mean(o * o, axis=-1, keepdims=True) + EPS) * ng_ref[...]
        y = (o * _silu(z_ref[0, :, sl].astype(F32))).astype(BF16)
        p = jnp.dot(y, w_ref[sl, :], preferred_element_type=F32)
        acc = p if acc is None else acc + p
    gate = mod_ref[0, :, 2 * d:3 * d]
    hn = res_ref[0] + gate * acc
    o_ref[0] = hn * lax.rsqrt(jnp.mean(hn * hn, axis=-1, keepdims=True) + EPS) * fg_ref[...]


def _dn_out(o_f, o_b, z, res, mod, norm_g, w_out, final_g, tm=512):
    b, nh, l, dv = o_f.shape
    d = res.shape[-1]
    tm = min(tm, l)
    tok = lambda wd: pl.BlockSpec((1, tm, wd), lambda i, j: (i, j, 0))
    hs = pl.BlockSpec((1, nh, tm, dv), lambda i, j: (i, 0, j, 0))
    return pl.pallas_call(
        functools.partial(_dn_out_kernel, nheads=nh, dv=dv, d=d),
        out_shape=jax.ShapeDtypeStruct((b, l, d), F32),
        grid=(b, l // tm),
        in_specs=[hs, hs, tok(nh * dv), tok(d),
                  pl.BlockSpec((1, 1, 3 * d), lambda i, j: (i, 0, 0)),
                  _const_spec((1, dv)), _const_spec(w_out.shape), _const_spec((1, d))],
        out_specs=tok(d),
        compiler_params=_cparams("parallel", "parallel"), name="delta_out",
    )(o_f, o_b, z, res, mod, norm_g.reshape(1, dv), w_out, final_g.reshape(1, d))


def _fourier_layer(h_lat, h_ctx, norm_g, mod_lat, mod_ctx, w_in, w_grp, w_out):
    b, l, _ = h_lat.shape
    lc = h_ctx.shape[1]
    width = w_in.shape[1] // 2
    gd = w_grp.shape[-1]
    w_in = w_in.astype(BF16)
    w_out = w_out.astype(BF16)
    gc, gs = _grp(w_grp)

    u, z = _nmm(h_lat, norm_g, mod_lat, w_in, (width, width), (BF16, BF16))
    n2 = DFT_N2
    n1 = l // n2
    y = _dft1(u, n1, n2)
    xr, xi = _dft2(_dft_complex_matrix(n2), y.reshape(b, 2 * n2, n1 * width))
    xr = xr.reshape(b, l, width)
    xi = xi.reshape(b, l, width)
    out_lat = _fn_out(xr, xi, z, h_lat, mod_lat, gc, gs, w_out, 1.0 / math.sqrt(l * gd))

    uc, zc = _nmm(h_ctx, norm_g, mod_ctx, w_in, (width, width), (BF16, BF16))
    xrc, xic = _dft2(_dft_real_matrix(lc), uc)
    out_ctx = _fn_out(xrc, xic, zc, h_ctx, mod_ctx, gc, gs, w_out, 1.0 / math.sqrt(lc * gd))
    return out_lat, out_ctx


def _delta_project(h, norm_g, mod, w_cat, abp, conv_w, rows, cols):
    qk_w = DN_HEADS * DN_DK
    v_w = DN_HEADS * DN_DV
    qkv, z, gb = _nmm(h, norm_g, mod, w_cat, (2 * qk_w + v_w, v_w, GB_LANES), (BF16, BF16, F32),
                      abp=abp, gb_heads=DN_HEADS)
    q = _conv(qkv, conv_w, rows, cols, 0, DN_HEADS, DN_DK, True, DN_DK ** -0.5, F32)
    k = _conv(qkv, conv_w, rows, cols, qk_w, DN_HEADS, DN_DK, True, 1.0, F32)
    v = _conv(qkv, conv_w, rows, cols, 2 * qk_w, DN_HEADS, DN_DV, False, 1.0, F32)
    return q, k, v, z, gb


def _delta_layer_final(h_lat, h_ctx, norm_g, mod_lat, mod_ctx, w_in, conv_w, a_log, dt_bias, dn_norm_g, w_out,
                       final_g):
    b, l, d = h_lat.shape
    lc = h_ctx.shape[1]
    main_w = 2 * DN_HEADS * DN_DK + 2 * DN_HEADS * DN_DV
    n_ab = 4 * DN_HEADS
    w_cat = jnp.concatenate(
        [w_in[:, :main_w], w_in[:, main_w:], jnp.zeros((d, GB_LANES - n_ab), w_in.dtype)], axis=1).astype(BF16)
    pad = jnp.zeros((GB_LANES - 2 * DN_HEADS,), F32)
    abp = jnp.stack([jnp.concatenate([a_log.reshape(-1).astype(F32), pad]),
                     jnp.concatenate([dt_bias.reshape(-1).astype(F32), pad])])
    qc, kc, vc, _, gbc = _delta_project(h_ctx, norm_g, mod_ctx, w_cat, abp, conv_w, 1, lc)
    ql, kl, vl, zl, gbl = _delta_project(h_lat, norm_g, mod_lat, w_cat, abp, conv_w, l // GRID_W, GRID_W)
    s0 = jnp.zeros((b, DN_HEADS, 2, DN_DK, DN_DV), F32)
    _, _, s_ctx = _scan(*_prep(qc, kc, vc, gbc), s0)
    o_f, o_b, _ = _scan(*_prep(ql, kl, vl, gbl), s_ctx)
    return _dn_out(o_f, o_b, zl, h_lat, mod_lat, dn_norm_g, w_out.astype(BF16), final_g)


def kernel(x, c, ctx, c_ctx, mod_w, mod_b, norm_g, final_g, fn_w_in, fn_w_grp, fn_w_out,
           dn_w_in, dn_conv, dn_a_log, dn_dt_bias, dn_norm_g, dn_w_out):
    b, l, d = x.shape
    cond = jnp.concatenate([c, c_ctx[None, :], jnp.zeros((8 - b - 1, d), c.dtype)], axis=0)
    mods = _mods(cond, mod_w, mod_b)
    mod_lat = [mods[i, :b].reshape(b, 1, 3 * d) for i in range(2)]
    mod_ctx = [mods[i, b:b + 1].reshape(1, 1, 3 * d) for i in range(2)]
    h_lat, h_ctx = _fourier_layer(x, ctx, norm_g[0], mod_lat[0], mod_ctx[0], fn_w_in[0], fn_w_grp[0], fn_w_out[0])
    return _delta_layer_final(h_lat, h_ctx, norm_g[1], mod_lat[1], mod_ctx[1], dn_w_in[0], dn_conv[0],
                              dn_a_log[0], dn_dt_bias[0], dn_norm_g[0], dn_w_out[0], final_g)
```

```python
import functools
import math

import numpy as np
import jax
import jax.numpy as jnp
from jax import lax
from jax.experimental import pallas as pl
from jax.experimental.pallas import tpu as pltpu

F32 = jnp.float32
BF16 = jnp.bfloat16
HI = lax.Precision.HIGHEST
EPS = 1e-6

GRID_W = 64
FN_GROUPS = 8
DN_HEADS = 8
DN_DK = 128
DN_DV = 256
SCAN_CHUNK = 128
DFT_N2 = 128
GB_LANES = 128
VMEM_LIMIT_BYTES = 48 * 1024 * 1024


def _cparams(*sem):
    return pltpu.CompilerParams(dimension_semantics=sem, vmem_limit_bytes=VMEM_LIMIT_BYTES)


def _const_spec(shape):
    zeros = (0,) * len(shape)
    return pl.BlockSpec(shape, lambda *_: zeros)


def _silu(v):
    return v * jax.nn.sigmoid(v)


def _mods_kernel(cond_ref, w_ref, b_ref, o_ref):
    s = _silu(cond_ref[...])
    o_ref[0] = jnp.dot(s, w_ref[0], preferred_element_type=F32, precision=HI) + b_ref[0]


def _mods(cond, mod_w, mod_b):
    depth, d, d3 = mod_w.shape
    rows = cond.shape[0]
    tn = 512
    return pl.pallas_call(
        _mods_kernel,
        out_shape=jax.ShapeDtypeStruct((depth, rows, d3), F32),
        grid=(depth, d3 // tn),
        in_specs=[pl.BlockSpec((rows, d), lambda i, j: (0, 0)),
                  pl.BlockSpec((1, d, tn), lambda i, j: (i, 0, j)),
                  pl.BlockSpec((1, 1, tn), lambda i, j: (i, 0, j))],
        out_specs=pl.BlockSpec((1, rows, tn), lambda i, j: (i, 0, j)),
        compiler_params=_cparams("parallel", "parallel"), name="adaln_mods",
    )(cond, mod_w, mod_b.reshape(depth, 1, d3))


def _nmm_kernel(*refs, splits, d, gb_heads):
    if gb_heads:
        x_ref, g_ref, mod_ref, w_ref, abp_ref = refs[:5]
        outs = refs[5:]
    else:
        x_ref, g_ref, mod_ref, w_ref = refs[:4]
        outs = refs[4:]
    x = x_ref[0]
    y = x * lax.rsqrt(jnp.mean(x * x, axis=-1, keepdims=True) + EPS) * g_ref[...]
    shift = mod_ref[0, :, 0:d]
    scale = mod_ref[0, :, d:2 * d]
    n = (y * (1.0 + scale) + shift).astype(BF16)
    c0 = 0
    for idx, (o_ref, width) in enumerate(zip(outs, splits)):
        step = min(width, 1024)
        for s in range(0, width, step):
            r = jnp.dot(n, w_ref[:, c0 + s:c0 + s + step], preferred_element_type=F32)
            if gb_heads and idx == len(splits) - 1:
                a_log = abp_ref[0:1, :]
                t = r + abp_ref[1:2, :]
                softplus = jnp.maximum(t, 0.0) + jnp.log(1.0 + jnp.exp(-jnp.abs(t)))
                lane = lax.broadcasted_iota(jnp.int32, r.shape, 1)
                r = jnp.where(lane < 2 * gb_heads, -jnp.exp(a_log) * softplus, jax.nn.sigmoid(r))
            o_ref[0, :, s:s + step] = r.astype(o_ref.dtype)
        c0 += width


def _nmm(x, g, mod, w, splits, dtypes, abp=None, gb_heads=0, tm=512):
    b, l, d = x.shape
    tm = min(tm, l)
    ntot = w.shape[1]
    bm = mod.shape[0]
    mod_map = (lambda i, j: (i, 0, 0)) if bm > 1 else (lambda i, j: (0, 0, 0))
    in_specs = [pl.BlockSpec((1, tm, d), lambda i, j: (i, j, 0)),
                _const_spec((1, d)),
                pl.BlockSpec((1, 1, 3 * d), mod_map),
                _const_spec((d, ntot))]
    args = [x, g.reshape(1, d), mod, w]
    if gb_heads:
        in_specs.append(_const_spec(abp.shape))
        args.append(abp)
    return pl.pallas_call(
        functools.partial(_nmm_kernel, splits=splits, d=d, gb_heads=gb_heads),
        out_shape=[jax.ShapeDtypeStruct((b, l, wd), dt) for wd, dt in zip(splits, dtypes)],
        grid=(b, l // tm),
        in_specs=in_specs,
        out_specs=[pl.BlockSpec((1, tm, wd), lambda i, j: (i, j, 0)) for wd in splits],
        compiler_params=_cparams("parallel", "parallel"), name="norm_proj",
    )(*args)


def _dft_cos_sin(n):
    k = np.arange(n)
    ang = 2.0 * np.pi * ((k[:, None] * k[None, :]) % n) / n
    return np.cos(ang), np.sin(ang)


def _dft1_kernel(m_ref, u_ref, cos_ref, sin_ref, o_ref, *, n1, tn2):
    ut = pltpu.einshape("abc->bac", u_ref[0])
    for j in range(tn2):
        r = jnp.dot(m_ref[...], ut[j], preferred_element_type=F32)
        yr = r[:n1]
        yi = r[n1:]
        cs = cos_ref[j]
        sn = sin_ref[j]
        o_ref[0, 0, j] = (yr * cs + yi * sn).astype(o_ref.dtype)
        o_ref[0, 1, j] = (yi * cs - yr * sn).astype(o_ref.dtype)


def _dft1(u, n1, n2, tn2=16, tc=1024):
    b, l, c = u.shape
    cm, sm = _dft_cos_sin(n1)
    m1 = jnp.asarray(np.concatenate([cm, -sm], axis=0), F32).astype(BF16)
    ang = 2.0 * np.pi * (np.arange(n2)[:, None] * np.arange(n1)[None, :]) / l
    tw_cos = jnp.asarray(np.cos(ang)[:, :, None], F32)
    tw_sin = jnp.asarray(np.sin(ang)[:, :, None], F32)
    return pl.pallas_call(
        functools.partial(_dft1_kernel, n1=n1, tn2=tn2),
        out_shape=jax.ShapeDtypeStruct((b, 2, n2, n1, c), BF16),
        grid=(b, n2 // tn2, c // tc),
        in_specs=[_const_spec((2 * n1, n1)),
                  pl.BlockSpec((1, n1, tn2, tc), lambda i, j, k: (i, 0, j, k)),
                  pl.BlockSpec((tn2, n1, 1), lambda i, j, k: (j, 0, 0)),
                  pl.BlockSpec((tn2, n1, 1), lambda i, j, k: (j, 0, 0))],
        out_specs=pl.BlockSpec((1, 2, tn2, n1, tc), lambda i, j, k: (i, 0, j, 0, k)),
        compiler_params=_cparams("parallel", "parallel", "parallel"), name="dft_stage1",
    )(m1, u.reshape(b, n1, n2, c), tw_cos, tw_sin)


def _dft2_lat_kernel(m_ref, y_ref, xr_ref, xi_ref, *, n2, tk1):
    yr = pltpu.einshape("abc->bac", y_ref[0, 0])
    yi = pltpu.einshape("abc->bac", y_ref[0, 1])
    xr, xi = [], []
    for j in range(tk1):
        r = jnp.dot(m_ref[...], jnp.concatenate([yr[j], yi[j]], axis=0), preferred_element_type=F32)
        xr.append(r[:n2].astype(xr_ref.dtype))
        xi.append(r[n2:].astype(xi_ref.dtype))
    xr_ref[0] = pltpu.einshape("abc->bac", jnp.stack(xr))
    xi_ref[0] = pltpu.einshape("abc->bac", jnp.stack(xi))


def _dft2_lat(m, y, tk1=16, tc=256):
    b, _, n2, n1, c = y.shape
    tk1 = min(tk1, n1)
    return pl.pallas_call(
        functools.partial(_dft2_lat_kernel, n2=n2, tk1=tk1),
        out_shape=[jax.ShapeDtypeStruct((b, n2, n1, c), BF16)] * 2,
        grid=(b, n1 // tk1, c // tc),
        in_specs=[_const_spec(m.shape),
                  pl.BlockSpec((1, 2, n2, tk1, tc), lambda i, j, k: (i, 0, 0, j, k))],
        out_specs=[pl.BlockSpec((1, n2, tk1, tc), lambda i, j, k: (i, 0, j, k))] * 2,
        compiler_params=_cparams("parallel", "parallel", "parallel"), name="dft_stage2_lat",
    )(m, y)


def _dft2_kernel(m_ref, y_ref, xr_ref, xi_ref, *, half):
    r = jnp.dot(m_ref[...], y_ref[0], preferred_element_type=F32)
    xr_ref[0] = r[:half].astype(xr_ref.dtype)
    xi_ref[0] = r[half:].astype(xi_ref.dtype)


def _dft2(m, y, tc=2048):
    b, k, cols = y.shape
    half = m.shape[0] // 2
    return pl.pallas_call(
        functools.partial(_dft2_kernel, half=half),
        out_shape=[jax.ShapeDtypeStruct((b, half, cols), BF16)] * 2,
        grid=(b, cols // tc),
        in_specs=[_const_spec(m.shape),
                  pl.BlockSpec((1, k, tc), lambda i, j: (i, 0, j))],
        out_specs=[pl.BlockSpec((1, half, tc), lambda i, j: (i, 0, j))] * 2,
        compiler_params=_cparams("parallel", "parallel"), name="dft_stage2",
    )(m, y)


def _dft_real_matrix(n):
    cm, sm = _dft_cos_sin(n)
    return jnp.asarray(np.concatenate([cm, -sm], axis=0), F32).astype(BF16)


def _dft_complex_matrix(n):
    cm, sm = _dft_cos_sin(n)
    return jnp.asarray(np.block([[cm, sm], [-sm, cm]]), F32).astype(BF16)


def _grp_kernel(m_ref, w_ref, gc_ref, gs_ref, *, n):
    r = jnp.dot(m_ref[...], w_ref[0], preferred_element_type=F32, precision=HI)
    gc_ref[0] = r[:n].astype(gc_ref.dtype)
    gs_ref[0] = r[n:].astype(gs_ref.dtype)


def _grp(w_grp):
    groups, n, _ = w_grp.shape
    cm, sm = _dft_cos_sin(n)
    m = jnp.asarray(np.concatenate([cm, sm], axis=0), F32)
    return pl.pallas_call(
        functools.partial(_grp_kernel, n=n),
        out_shape=[jax.ShapeDtypeStruct((groups, n, n), BF16)] * 2,
        grid=(groups,),
        in_specs=[_const_spec((2 * n, n)),
                  pl.BlockSpec((1, n, n), lambda g: (g, 0, 0))],
        out_specs=[pl.BlockSpec((1, n, n), lambda g: (g, 0, 0))] * 2,
        compiler_params=_cparams("parallel"), name="group_weights",
    )(m, w_grp)


def _fn_out_kernel(xr_ref, xi_ref, z_ref, res_ref, mod_ref, gc_ref, gs_ref, w_ref, o_ref, *, groups, gd, d, scale):
    acc = None
    for g in range(groups):
        sl = slice(g * gd, (g + 1) * gd)
        f = (jnp.dot(xr_ref[0, :, sl], gc_ref[g], preferred_element_type=F32)
             + jnp.dot(xi_ref[0, :, sl], gs_ref[g], preferred_element_type=F32))
        y = (f * scale * _silu(z_ref[0, :, sl].astype(F32))).astype(BF16)
        p = jnp.dot(y, w_ref[sl, :], preferred_element_type=F32)
        acc = p if acc is None else acc + p
    gate = mod_ref[0, :, 2 * d:3 * d]
    o_ref[0] = res_ref[0] + gate * acc


def _fn_out(xr, xi, z, res, mod, gc, gs, w_out, scale, tm=512):
    b, l, c = xr.shape
    d = res.shape[-1]
    tm = min(tm, l)
    groups, gd, _ = gc.shape
    bm = mod.shape[0]
    mod_map = (lambda i, j: (i, 0, 0)) if bm > 1 else (lambda i, j: (0, 0, 0))
    tok = lambda wd: pl.BlockSpec((1, tm, wd), lambda i, j: (i, j, 0))
    return pl.pallas_call(
        functools.partial(_fn_out_kernel, groups=groups, gd=gd, d=d, scale=scale),
        out_shape=jax.ShapeDtypeStruct((b, l, d), F32),
        grid=(b, l // tm),
        in_specs=[tok(c), tok(c), tok(c), tok(d),
                  pl.BlockSpec((1, 1, 3 * d), mod_map),
                  _const_spec(gc.shape), _const_spec(gs.shape), _const_spec(w_out.shape)],
        out_specs=tok(d),
        compiler_params=_cparams("parallel", "parallel"), name="fourier_out",
    )(xr, xi, z, res, mod, gc, gs, w_out)


def _conv_kernel(prev_ref, main_ref, next_ref, w_ref, ml_ref, mr_ref, o_ref, *, cols, tr, hd, normalize, scale):
    i = pl.program_id(1)
    last = pl.num_programs(1) - 1
    prev = prev_ref[0].astype(F32) * (i > 0).astype(F32)
    nxt = next_ref[0].astype(F32) * (i < last).astype(F32)
    e = jnp.concatenate([prev, main_ref[0].astype(F32), nxt], axis=0)
    t_all = e.shape[0]
    shifted = (pltpu.roll(e, 1, axis=0) * ml_ref[...], e, pltpu.roll(e, t_all - 1, axis=0) * mr_ref[...])
    n = tr * cols
    acc = None
    for dr in range(3):
        for dc in range(3):
            term = shifted[dc][dr * cols:dr * cols + n] * w_ref[dr, dc:dc + 1, :]
            acc = term if acc is None else acc + term
    y = _silu(acc)
    for j in range(y.shape[1] // hd):
        yh = y[:, j * hd:(j + 1) * hd]
        if normalize:
            yh = yh * lax.rsqrt(jnp.sum(yh * yh, axis=-1, keepdims=True) + EPS) * scale
        o_ref[0, j] = yh.astype(o_ref.dtype)


def _conv(qkv, w, rows, cols, ch_start, heads, hd, normalize, scale, out_dtype, cb=512):
    b, l, ctot = qkv.shape
    tr = min(rows, max(1, 512 // cols))
    n = tr * cols
    t_all = (tr + 2) * cols
    cofs = ch_start // cb
    ncb = heads * hd // cb
    col = np.arange(t_all) % cols
    ml = jnp.asarray(np.broadcast_to((col != 0)[:, None], (t_all, cb)), F32)
    mr = jnp.asarray(np.broadcast_to((col != cols - 1)[:, None], (t_all, cb)), F32)
    return pl.pallas_call(
        functools.partial(_conv_kernel, cols=cols, tr=tr, hd=hd, normalize=normalize, scale=scale),
        out_shape=jax.ShapeDtypeStruct((b, heads, l, hd), out_dtype),
        grid=(b, rows // tr, ncb),
        in_specs=[pl.BlockSpec((1, cols, cb), lambda bi, i, c: (bi, jnp.maximum(i * tr - 1, 0), cofs + c)),
                  pl.BlockSpec((1, n, cb), lambda bi, i, c: (bi, i, cofs + c)),
                  pl.BlockSpec((1, cols, cb), lambda bi, i, c: (bi, jnp.minimum((i + 1) * tr, rows - 1), cofs + c)),
                  pl.BlockSpec((3, 3, cb), lambda bi, i, c: (0, 0, cofs + c)),
                  _const_spec((t_all, cb)), _const_spec((t_all, cb))],
        out_specs=pl.BlockSpec((1, cb // hd, n, hd), lambda bi, i, c: (bi, c, i, 0)),
        compiler_params=_cparams("parallel", "parallel", "parallel"), name="conv_silu",
    )(qkv, qkv, qkv, w, ml, mr)


def _prep_kernel(q_ref, k_ref, v_ref, g_ref, wq_ref, a2_ref, u_ref, eg_ref, *, nheads, cs, nck):
    h = pl.program_id(1)
    ri = lax.broadcasted_iota(jnp.int32, (cs, cs), 0)
    ci = lax.broadcasted_iota(jnp.int32, (cs, cs), 1)
    eye = ri == ci
    xor = ri ^ ci
    lane = lax.broadcasted_iota(jnp.int32, (cs, GB_LANES), 1)
    nt = (((1,), (1,)), ((), ()))
    levels = int(math.log2(cs))
    chains = []
    for c in range(nck):
        rows = slice(c * cs, (c + 1) * cs)
        q = q_ref[0, 0, rows, :].astype(F32)
        k = k_ref[0, 0, rows, :].astype(F32)
        v = v_ref[0, 0, rows, :].astype(F32)
        kbf = k.astype(BF16)
        kk = lax.dot_general(kbf, kbf, nt, preferred_element_type=F32)
        qk = lax.dot_general(q.astype(BF16), kbf, nt, preferred_element_type=F32)
        k_t = k.T
        gbv = g_ref[0, rows, :]
        for d in range(2):
            incl = (ri >= ci) if d == 0 else (ri <= ci)
            strict = (ri > ci) if d == 0 else (ri < ci)
            cum = (ri <= ci) if d == 0 else (ri >= ci)
            g_col = jnp.sum(jnp.where(lane == d * nheads + h, gbv, 0.0), axis=1, keepdims=True)
            b_col = jnp.sum(jnp.where(lane == (2 + d) * nheads + h, gbv, 0.0), axis=1, keepdims=True)
            gc_row = jnp.sum(jnp.where(cum, jnp.broadcast_to(g_col, (cs, cs)), 0.0), axis=0, keepdims=True)
            gc_col = jnp.sum(jnp.where(eye, jnp.broadcast_to(gc_row, (cs, cs)), 0.0), axis=1, keepdims=True)
            decay = jnp.where(incl, jnp.exp(jnp.where(incl, gc_col - gc_row, 0.0)), 0.0)
            egc = jnp.exp(gc_col)
            g_tot = jnp.sum(g_col, axis=0, keepdims=True)
            wq_ref[0, 0, d, c, cs:, :] = (q * egc).astype(wq_ref.dtype)
            a2_ref[0, 0, d, c, :cs, :] = (qk * decay).astype(a2_ref.dtype)
            a2_ref[0, 0, d, c, cs:, :] = (k_t * jnp.exp(g_tot - gc_row)).astype(a2_ref.dtype)
            eg_ref[0, 0, d, c] = jnp.broadcast_to(jnp.exp(g_tot), eg_ref.shape[-2:])
            rhs = jnp.concatenate([v * b_col, k * (b_col * egc)], axis=1).astype(BF16)
            chains.append((d, c, jnp.where(strict, kk * b_col * decay, 0.0), rhs))
    dmats = [jnp.where(eye, 1.0, 0.0) - jnp.where(xor == 1, a_off, 0.0) for _, _, a_off, _ in chains]
    for lv in range(1, levels):
        sel = (xor >> lv) == 1
        des = [jnp.dot(dm.astype(BF16), jnp.where(sel, ch[2], 0.0).astype(BF16), preferred_element_type=F32)
               for dm, ch in zip(dmats, chains)]
        dmats = [dm - jnp.dot(de.astype(BF16), dm.astype(BF16), preferred_element_type=F32)
                 for dm, de in zip(dmats, des)]
    sols = [jnp.dot(dm.astype(BF16), ch[3], preferred_element_type=F32) for dm, ch in zip(dmats, chains)]
    dv = u_ref.shape[-1]
    for (d, c, _, _), sol in zip(chains, sols):
        u_ref[0, 0, d, c] = sol[:, :dv].astype(u_ref.dtype)
        wq_ref[0, 0, d, c, :cs, :] = sol[:, dv:].astype(wq_ref.dtype)


def _prep(q, k, v, gb, cs=SCAN_CHUNK, nck=4):
    b, nh, l, dk = q.shape
    dv = v.shape[-1]
    nc = l // cs
    nck = min(nck, nc)
    tl = nck * cs
    tok = lambda wd: pl.BlockSpec((1, 1, tl, wd), lambda bi, hi, n: (bi, hi, n, 0))
    out = lambda r, wd: pl.BlockSpec((1, 1, 2, nck, r, wd), lambda bi, hi, n: (bi, hi, 0, n, 0, 0))
    return pl.pallas_call(
        functools.partial(_prep_kernel, nheads=nh, cs=cs, nck=nck),
        out_shape=[jax.ShapeDtypeStruct((b, nh, 2, nc, 2 * cs, dk), BF16),
                   jax.ShapeDtypeStruct((b, nh, 2, nc, cs + dk, cs), BF16),
                   jax.ShapeDtypeStruct((b, nh, 2, nc, cs, dv), BF16),
                   jax.ShapeDtypeStruct((b, nh, 2, nc, 8, 128), F32)],
        grid=(b, nh, nc // nck),
        in_specs=[tok(dk), tok(dk), tok(dv), pl.BlockSpec((1, tl, GB_LANES), lambda bi, hi, n: (bi, n, 0))],
        out_specs=[out(2 * cs, dk), out(cs + dk, cs), out(cs, dv), out(8, 128)],
        compiler_params=_cparams("parallel", "parallel", "parallel"), name="delta_prep",
    )(q, k, v, gb)


def _scan_kernel(wqf_ref, a2f_ref, uf_ref, egf_ref, wqb_ref, a2b_ref, ub_ref, egb_ref, s0_ref,
                 of_ref, ob_ref, sfin_ref, s_scr, *, hb, cs):
    n = pl.program_id(2)

    @pl.when(n == 0)
    def _():
        s_scr[...] = s0_ref[0]

    dirs = ((wqf_ref, a2f_ref, uf_ref, egf_ref, of_ref), (wqb_ref, a2b_ref, ub_ref, egb_ref, ob_ref))
    for j in range(hb):
        for d, (wq_ref, a2_ref, u_ref, eg_ref, o_ref) in enumerate(dirs):
            s = s_scr[j, d]
            r1 = jnp.dot(wq_ref[0, j, 0, 0], s.astype(BF16), preferred_element_type=F32)
            v_new = u_ref[0, j, 0, 0].astype(F32) - r1[:cs]
            r2 = jnp.dot(a2_ref[0, j, 0, 0], v_new.astype(BF16), preferred_element_type=F32)
            o_ref[0, j] = (r1[cs:] + r2[:cs]).astype(o_ref.dtype)
            s_scr[j, d] = s * eg_ref[0, j, 0, 0][0:1, 0:1] + r2[cs:]

    @pl.when(n == pl.num_programs(2) - 1)
    def _():
        sfin_ref[0] = s_scr[...]


def _scan(wq, a2, u, eg, s0, hb=8):
    b, nh, _, nc, cs, dv = u.shape
    dk = wq.shape[-1]
    spec = lambda r, wd, d, rev: pl.BlockSpec(
        (1, hb, 1, 1, r, wd), lambda bi, hi, n: (bi, hi, d, (nc - 1 - n) if rev else n, 0, 0))
    both = lambda r, wd: [spec(r, wd, 0, False), spec(r, wd, 1, True)]
    ins = [both(2 * cs, dk), both(cs + dk, cs), both(cs, dv), both(8, 128)]
    st = pl.BlockSpec((1, hb, 2, dk, dv), lambda bi, hi, n: (bi, hi, 0, 0, 0))
    return pl.pallas_call(
        functools.partial(_scan_kernel, hb=hb, cs=cs),
        out_shape=[jax.ShapeDtypeStruct((b, nh, nc * cs, dv), BF16),
                   jax.ShapeDtypeStruct((b, nh, nc * cs, dv), BF16),
                   jax.ShapeDtypeStruct((b, nh, 2, dk, dv), F32)],
        grid=(b, nh // hb, nc),
        in_specs=[s[0] for s in ins] + [s[1] for s in ins] + [st],
        out_specs=[pl.BlockSpec((1, hb, cs, dv), lambda bi, hi, n: (bi, hi, n, 0)),
                   pl.BlockSpec((1, hb, cs, dv), lambda bi, hi, n: (bi, hi, nc - 1 - n, 0)),
                   st],
        scratch_shapes=[pltpu.VMEM((hb, 2, dk, dv), F32)],
        compiler_params=_cparams("parallel", "parallel", "arbitrary"), name="delta_scan",
    )(wq, a2, u, eg, wq, a2, u, eg, s0)


def _dn_out_kernel(of_ref, ob_ref, z_ref, res_ref, mod_ref, ng_ref, w_ref, fg_ref, o_ref, *, nheads, dv, d):
    acc = None
    for hd in range(nheads):
        sl = slice(hd * dv, (hd + 1) * dv)
        o = of_ref[0, hd].astype(F32) + ob_ref[0, hd].astype(F32)
        o = o * lax.rsqrt(jnp.mean(o * o, axis=-1, keepdims=True) + EPS) * ng_ref[...]
        y = (o * _silu(z_ref[0, :, sl].astype(F32))).astype(BF16)
        p = jnp.dot(y, w_ref[sl, :], preferred_element_type=F32)
        acc = p if acc is None else acc + p
    gate = mod_ref[0, :, 2 * d:3 * d]
    hn = res_ref[0] + gate * acc
    o_ref[0] = hn * lax.rsqrt(jnp.mean(hn * hn, axis=-1, keepdims=True) + EPS) * fg_ref[...]


def _dn_out(o_f, o_b, z, res, mod, norm_g, w_out, final_g, tm=512):
    b, nh, l, dv = o_f.shape
    d = res.shape[-1]
    tm = min(tm, l)
    tok = lambda wd: pl.BlockSpec((1, tm, wd), lambda i, j: (i, j, 0))
    hs = pl.BlockSpec((1, nh, tm, dv), lambda i, j: (i, 0, j, 0))
    return pl.pallas_call(
        functools.partial(_dn_out_kernel, nheads=nh, dv=dv, d=d),
        out_shape=jax.ShapeDtypeStruct((b, l, d), F32),
        grid=(b, l // tm),
        in_specs=[hs, hs, tok(nh * dv), tok(d),
                  pl.BlockSpec((1, 1, 3 * d), lambda i, j: (i, 0, 0)),
                  _const_spec((1, dv)), _const_spec(w_out.shape), _const_spec((1, d))],
        out_specs=tok(d),
        compiler_params=_cparams("parallel", "parallel"), name="delta_out",
    )(o_f, o_b, z, res, mod, norm_g.reshape(1, dv), w_out, final_g.reshape(1, d))


def _fourier_layer(h_lat, h_ctx, norm_g, mod_lat, mod_ctx, w_in, w_grp, w_out):
    b, l, _ = h_lat.shape
    lc = h_ctx.shape[1]
    width = w_in.shape[1] // 2
    gd = w_grp.shape[-1]
    w_in = w_in.astype(BF16)
    w_out = w_out.astype(BF16)
    gc, gs = _grp(w_grp)

    u, z = _nmm(h_lat, norm_g, mod_lat, w_in, (width, width), (BF16, BF16))
    n2 = DFT_N2
    n1 = l // n2
    y = _dft1(u, n1, n2)
    xr, xi = _dft2_lat(_dft_complex_matrix(n2), y)
    xr = xr.reshape(b, l, width)
    xi = xi.reshape(b, l, width)
    out_lat = _fn_out(xr, xi, z, h_lat, mod_lat, gc, gs, w_out, 1.0 / math.sqrt(l * gd))

    uc, zc = _nmm(h_ctx, norm_g, mod_ctx, w_in, (width, width), (BF16, BF16))
    xrc, xic = _dft2(_dft_real_matrix(lc), uc)
    out_ctx = _fn_out(xrc, xic, zc, h_ctx, mod_ctx, gc, gs, w_out, 1.0 / math.sqrt(lc * gd))
    return out_lat, out_ctx


def _delta_project(h, norm_g, mod, w_cat, abp, conv_w, rows, cols):
    qk_w = DN_HEADS * DN_DK
    v_w = DN_HEADS * DN_DV
    qkv, z, gb = _nmm(h, norm_g, mod, w_cat, (2 * qk_w + v_w, v_w, GB_LANES), (BF16, BF16, F32),
                      abp=abp, gb_heads=DN_HEADS)
    q = _conv(qkv, conv_w, rows, cols, 0, DN_HEADS, DN_DK, True, DN_DK ** -0.5, BF16)
    k = _conv(qkv, conv_w, rows, cols, qk_w, DN_HEADS, DN_DK, True, 1.0, BF16)
    v = _conv(qkv, conv_w, rows, cols, 2 * qk_w, DN_HEADS, DN_DV, False, 1.0, BF16)
    return q, k, v, z, gb


def _delta_layer_final(h_lat, h_ctx, norm_g, mod_lat, mod_ctx, w_in, conv_w, a_log, dt_bias, dn_norm_g, w_out,
                       final_g):
    b, l, d = h_lat.shape
    lc = h_ctx.shape[1]
    main_w = 2 * DN_HEADS * DN_DK + 2 * DN_HEADS * DN_DV
    n_ab = 4 * DN_HEADS
    w_cat = jnp.concatenate(
        [w_in[:, :main_w], w_in[:, main_w:], jnp.zeros((d, GB_LANES - n_ab), w_in.dtype)], axis=1).astype(BF16)
    pad = jnp.zeros((GB_LANES - 2 * DN_HEADS,), F32)
    abp = jnp.stack([jnp.concatenate([a_log.reshape(-1).astype(F32), pad]),
                     jnp.concatenate([dt_bias.reshape(-1).astype(F32), pad])])
    qc, kc, vc, _, gbc = _delta_project(h_ctx, norm_g, mod_ctx, w_cat, abp, conv_w, 1, lc)
    ql, kl, vl, zl, gbl = _delta_project(h_lat, norm_g, mod_lat, w_cat, abp, conv_w, l // GRID_W, GRID_W)
    s0 = jnp.zeros((b, DN_HEADS, 2, DN_DK, DN_DV), F32)
    _, _, s_ctx = _scan(*_prep(qc, kc, vc, gbc), s0)
    o_f, o_b, _ = _scan(*_prep(ql, kl, vl, gbl), s_ctx)
    return _dn_out(o_f, o_b, zl, h_lat, mod_lat, dn_norm_g, w_out.astype(BF16), final_g)


def kernel(x, c, ctx, c_ctx, mod_w, mod_b, norm_g, final_g, fn_w_in, fn_w_grp, fn_w_out,
           dn_w_in, dn_conv, dn_a_log, dn_dt_bias, dn_norm_g, dn_w_out):
    b, l, d = x.shape
    cond = jnp.concatenate([c, c_ctx[None, :], jnp.zeros((8 - b - 1, d), c.dtype)], axis=0)
    mods = _mods(cond, mod_w, mod_b)
    mod_lat = [mods[i, :b].reshape(b, 1, 3 * d) for i in range(2)]
    mod_ctx = [mods[i, b:b + 1].reshape(1, 1, 3 * d) for i in range(2)]
    h_lat, h_ctx = _fourier_layer(x, ctx, norm_g[0], mod_lat[0], mod_ctx[0], fn_w_in[0], fn_w_grp[0], fn_w_out[0])
    return _delta_layer_final(h_lat, h_ctx, norm_g[1], mod_lat[1], mod_ctx[1], dn_w_in[0], dn_conv[0],
                              dn_a_log[0], dn_dt_bias[0], dn_norm_g[0], dn_w_out[0], final_g)
```

```python
import functools
import math

import numpy as np
import jax
import jax.numpy as jnp
from jax import lax
from jax.experimental import pallas as pl
from jax.experimental.pallas import tpu as pltpu

F32 = jnp.float32
BF16 = jnp.bfloat16
HI = lax.Precision.HIGHEST
EPS = 1e-6

GRID_W = 64
FN_GROUPS = 8
DN_HEADS = 8
DN_DK = 128
DN_DV = 256
SCAN_CHUNK = 128
DFT_N2 = 128
GB_LANES = 128
VMEM_LIMIT_BYTES = 48 * 1024 * 1024


def _cparams(*sem):
    return pltpu.CompilerParams(dimension_semantics=sem, vmem_limit_bytes=VMEM_LIMIT_BYTES)


def _const_spec(shape):
    zeros = (0,) * len(shape)
    return pl.BlockSpec(shape, lambda *_: zeros)


def _silu(v):
    return v * jax.nn.sigmoid(v)


def _mods_kernel(cond_ref, w_ref, b_ref, o_ref):
    s = _silu(cond_ref[...])
    o_ref[0] = jnp.dot(s, w_ref[0], preferred_element_type=F32, precision=HI) + b_ref[0]


def _mods(cond, mod_w, mod_b):
    depth, d, d3 = mod_w.shape
    rows = cond.shape[0]
    tn = 512
    return pl.pallas_call(
        _mods_kernel,
        out_shape=jax.ShapeDtypeStruct((depth, rows, d3), F32),
        grid=(depth, d3 // tn),
        in_specs=[pl.BlockSpec((rows, d), lambda i, j: (0, 0)),
                  pl.BlockSpec((1, d, tn), lambda i, j: (i, 0, j)),
                  pl.BlockSpec((1, 1, tn), lambda i, j: (i, 0, j))],
        out_specs=pl.BlockSpec((1, rows, tn), lambda i, j: (i, 0, j)),
        compiler_params=_cparams("parallel", "parallel"), name="adaln_mods",
    )(cond, mod_w, mod_b.reshape(depth, 1, d3))


def _nmm_kernel(*refs, splits, d, gb_heads):
    if gb_heads:
        x_ref, g_ref, mod_ref, w_ref, abp_ref = refs[:5]
        outs = refs[5:]
    else:
        x_ref, g_ref, mod_ref, w_ref = refs[:4]
        outs = refs[4:]
    x = x_ref[0]
    y = x * lax.rsqrt(jnp.mean(x * x, axis=-1, keepdims=True) + EPS) * g_ref[...]
    shift = mod_ref[0, :, 0:d]
    scale = mod_ref[0, :, d:2 * d]
    n = (y * (1.0 + scale) + shift).astype(BF16)
    c0 = 0
    for idx, (o_ref, width) in enumerate(zip(outs, splits)):
        step = min(width, 1024)
        for s in range(0, width, step):
            r = jnp.dot(n, w_ref[:, c0 + s:c0 + s + step], preferred_element_type=F32)
            if gb_heads and idx == len(splits) - 1:
                a_log = abp_ref[0:1, :]
                t = r + abp_ref[1:2, :]
                softplus = jnp.maximum(t, 0.0) + jnp.log(1.0 + jnp.exp(-jnp.abs(t)))
                lane = lax.broadcasted_iota(jnp.int32, r.shape, 1)
                r = jnp.where(lane < 2 * gb_heads, -jnp.exp(a_log) * softplus, jax.nn.sigmoid(r))
            o_ref[0, :, s:s + step] = r.astype(o_ref.dtype)
        c0 += width


def _nmm(x, g, mod, w, splits, dtypes, abp=None, gb_heads=0, tm=512):
    b, l, d = x.shape
    tm = min(tm, l)
    ntot = w.shape[1]
    bm = mod.shape[0]
    mod_map = (lambda i, j: (i, 0, 0)) if bm > 1 else (lambda i, j: (0, 0, 0))
    in_specs = [pl.BlockSpec((1, tm, d), lambda i, j: (i, j, 0)),
                _const_spec((1, d)),
                pl.BlockSpec((1, 1, 3 * d), mod_map),
                _const_spec((d, ntot))]
    args = [x, g.reshape(1, d), mod, w]
    if gb_heads:
        in_specs.append(_const_spec(abp.shape))
        args.append(abp)
    return pl.pallas_call(
        functools.partial(_nmm_kernel, splits=splits, d=d, gb_heads=gb_heads),
        out_shape=[jax.ShapeDtypeStruct((b, l, wd), dt) for wd, dt in zip(splits, dtypes)],
        grid=(b, l // tm),
        in_specs=in_specs,
        out_specs=[pl.BlockSpec((1, tm, wd), lambda i, j: (i, j, 0)) for wd in splits],
        compiler_params=_cparams("parallel", "parallel"), name="norm_proj",
    )(*args)


def _dft_cos_sin(n):
    k = np.arange(n)
    ang = 2.0 * np.pi * ((k[:, None] * k[None, :]) % n) / n
    return np.cos(ang), np.sin(ang)


def _dft1_kernel(m_ref, u_ref, cos_ref, sin_ref, o_ref, *, n1, tn2):
    ut = jnp.transpose(u_ref[0], (1, 0, 2))
    for j in range(tn2):
        r = jnp.dot(m_ref[...], ut[j], preferred_element_type=F32)
        yr = r[:n1]
        yi = r[n1:]
        cs = cos_ref[j]
        sn = sin_ref[j]
        o_ref[0, 0, j] = (yr * cs + yi * sn).astype(o_ref.dtype)
        o_ref[0, 1, j] = (yi * cs - yr * sn).astype(o_ref.dtype)


def _dft1(u, n1, n2, tn2=16, tc=1024):
    b, l, c = u.shape
    cm, sm = _dft_cos_sin(n1)
    m1 = jnp.asarray(np.concatenate([cm, -sm], axis=0), F32).astype(BF16)
    ang = 2.0 * np.pi * (np.arange(n2)[:, None] * np.arange(n1)[None, :]) / l
    tw_cos = jnp.asarray(np.cos(ang)[:, :, None], F32)
    tw_sin = jnp.asarray(np.sin(ang)[:, :, None], F32)
    return pl.pallas_call(
        functools.partial(_dft1_kernel, n1=n1, tn2=tn2),
        out_shape=jax.ShapeDtypeStruct((b, 2, n2, n1, c), BF16),
        grid=(b, n2 // tn2, c // tc),
        in_specs=[_const_spec((2 * n1, n1)),
                  pl.BlockSpec((1, n1, tn2, tc), lambda i, j, k: (i, 0, j, k)),
                  pl.BlockSpec((tn2, n1, 1), lambda i, j, k: (j, 0, 0)),
                  pl.BlockSpec((tn2, n1, 1), lambda i, j, k: (j, 0, 0))],
        out_specs=pl.BlockSpec((1, 2, tn2, n1, tc), lambda i, j, k: (i, 0, j, 0, k)),
        compiler_params=_cparams("parallel", "parallel", "parallel"), name="dft_stage1",
    )(m1, u.reshape(b, n1, n2, c), tw_cos, tw_sin)


def _dft2_lat_kernel(m_ref, y_ref, xr_ref, xi_ref, *, n2, tk1):
    yr = jnp.transpose(y_ref[0, 0], (1, 0, 2))
    yi = jnp.transpose(y_ref[0, 1], (1, 0, 2))
    xr, xi = [], []
    for j in range(tk1):
        r = jnp.dot(m_ref[...], jnp.concatenate([yr[j], yi[j]], axis=0), preferred_element_type=F32)
        xr.append(r[:n2].astype(xr_ref.dtype))
        xi.append(r[n2:].astype(xi_ref.dtype))
    xr_ref[0] = jnp.transpose(jnp.stack(xr), (1, 0, 2))
    xi_ref[0] = jnp.transpose(jnp.stack(xi), (1, 0, 2))


def _dft2_lat(m, y, tk1=16, tc=256):
    b, _, n2, n1, c = y.shape
    tk1 = min(tk1, n1)
    return pl.pallas_call(
        functools.partial(_dft2_lat_kernel, n2=n2, tk1=tk1),
        out_shape=[jax.ShapeDtypeStruct((b, n2, n1, c), BF16)] * 2,
        grid=(b, n1 // tk1, c // tc),
        in_specs=[_const_spec(m.shape),
                  pl.BlockSpec((1, 2, n2, tk1, tc), lambda i, j, k: (i, 0, 0, j, k))],
        out_specs=[pl.BlockSpec((1, n2, tk1, tc), lambda i, j, k: (i, 0, j, k))] * 2,
        compiler_params=_cparams("parallel", "parallel", "parallel"), name="dft_stage2_lat",
    )(m, y)


def _dft2_kernel(m_ref, y_ref, xr_ref, xi_ref, *, half):
    r = jnp.dot(m_ref[...], y_ref[0], preferred_element_type=F32)
    xr_ref[0] = r[:half].astype(xr_ref.dtype)
    xi_ref[0] = r[half:].astype(xi_ref.dtype)


def _dft2(m, y, tc=2048):
    b, k, cols = y.shape
    half = m.shape[0] // 2
    return pl.pallas_call(
        functools.partial(_dft2_kernel, half=half),
        out_shape=[jax.ShapeDtypeStruct((b, half, cols), BF16)] * 2,
        grid=(b, cols // tc),
        in_specs=[_const_spec(m.shape),
                  pl.BlockSpec((1, k, tc), lambda i, j: (i, 0, j))],
        out_specs=[pl.BlockSpec((1, half, tc), lambda i, j: (i, 0, j))] * 2,
        compiler_params=_cparams("parallel", "parallel"), name="dft_stage2",
    )(m, y)


def _dft_real_matrix(n):
    cm, sm = _dft_cos_sin(n)
    return jnp.asarray(np.concatenate([cm, -sm], axis=0), F32).astype(BF16)


def _dft_complex_matrix(n):
    cm, sm = _dft_cos_sin(n)
    return jnp.asarray(np.block([[cm, sm], [-sm, cm]]), F32).astype(BF16)


def _grp_kernel(m_ref, w_ref, gc_ref, gs_ref, *, n):
    r = jnp.dot(m_ref[...], w_ref[0], preferred_element_type=F32, precision=HI)
    gc_ref[0] = r[:n].astype(gc_ref.dtype)
    gs_ref[0] = r[n:].astype(gs_ref.dtype)


def _grp(w_grp):
    groups, n, _ = w_grp.shape
    cm, sm = _dft_cos_sin(n)
    m = jnp.asarray(np.concatenate([cm, sm], axis=0), F32)
    return pl.pallas_call(
        functools.partial(_grp_kernel, n=n),
        out_shape=[jax.ShapeDtypeStruct((groups, n, n), BF16)] * 2,
        grid=(groups,),
        in_specs=[_const_spec((2 * n, n)),
                  pl.BlockSpec((1, n, n), lambda g: (g, 0, 0))],
        out_specs=[pl.BlockSpec((1, n, n), lambda g: (g, 0, 0))] * 2,
        compiler_params=_cparams("parallel"), name="group_weights",
    )(m, w_grp)


def _fn_out_kernel(xr_ref, xi_ref, z_ref, res_ref, mod_ref, gc_ref, gs_ref, w_ref, o_ref, *, groups, gd, d, scale):
    acc = None
    for g in range(groups):
        sl = slice(g * gd, (g + 1) * gd)
        f = (jnp.dot(xr_ref[0, :, sl], gc_ref[g], preferred_element_type=F32)
             + jnp.dot(xi_ref[0, :, sl], gs_ref[g], preferred_element_type=F32))
        y = (f * scale * _silu(z_ref[0, :, sl].astype(F32))).astype(BF16)
        p = jnp.dot(y, w_ref[sl, :], preferred_element_type=F32)
        acc = p if acc is None else acc + p
    gate = mod_ref[0, :, 2 * d:3 * d]
    o_ref[0] = res_ref[0] + gate * acc


def _fn_out(xr, xi, z, res, mod, gc, gs, w_out, scale, tm=512):
    b, l, c = xr.shape
    d = res.shape[-1]
    tm = min(tm, l)
    groups, gd, _ = gc.shape
    bm = mod.shape[0]
    mod_map = (lambda i, j: (i, 0, 0)) if bm > 1 else (lambda i, j: (0, 0, 0))
    tok = lambda wd: pl.BlockSpec((1, tm, wd), lambda i, j: (i, j, 0))
    return pl.pallas_call(
        functools.partial(_fn_out_kernel, groups=groups, gd=gd, d=d, scale=scale),
        out_shape=jax.ShapeDtypeStruct((b, l, d), F32),
        grid=(b, l // tm),
        in_specs=[tok(c), tok(c), tok(c), tok(d),
                  pl.BlockSpec((1, 1, 3 * d), mod_map),
                  _const_spec(gc.shape), _const_spec(gs.shape), _const_spec(w_out.shape)],
        out_specs=tok(d),
        compiler_params=_cparams("parallel", "parallel"), name="fourier_out",
    )(xr, xi, z, res, mod, gc, gs, w_out)


def _conv_kernel(prev_ref, main_ref, next_ref, w_ref, ml_ref, mr_ref, o_ref, *, cols, tr, hd, normalize, scale):
    i = pl.program_id(1)
    last = pl.num_programs(1) - 1
    prev = prev_ref[0].astype(F32) * (i > 0).astype(F32)
    nxt = next_ref[0].astype(F32) * (i < last).astype(F32)
    e = jnp.concatenate([prev, main_ref[0].astype(F32), nxt], axis=0)
    t_all = e.shape[0]
    shifted = (pltpu.roll(e, 1, axis=0) * ml_ref[...], e, pltpu.roll(e, t_all - 1, axis=0) * mr_ref[...])
    n = tr * cols
    acc = None
    for dr in range(3):
        for dc in range(3):
            term = shifted[dc][dr * cols:dr * cols + n] * w_ref[dr, dc:dc + 1, :]
            acc = term if acc is None else acc + term
    y = _silu(acc)
    for j in range(y.shape[1] // hd):
        yh = y[:, j * hd:(j + 1) * hd]
        if normalize:
            yh = yh * lax.rsqrt(jnp.sum(yh * yh, axis=-1, keepdims=True) + EPS) * scale
        o_ref[0, j] = yh.astype(o_ref.dtype)


def _conv(qkv, w, rows, cols, ch_start, heads, hd, normalize, scale, out_dtype, cb=512):
    b, l, ctot = qkv.shape
    tr = min(rows, max(1, 512 // cols))
    n = tr * cols
    t_all = (tr + 2) * cols
    cofs = ch_start // cb
    ncb = heads * hd // cb
    col = np.arange(t_all) % cols
    ml = jnp.asarray(np.broadcast_to((col != 0)[:, None], (t_all, cb)), F32)
    mr = jnp.asarray(np.broadcast_to((col != cols - 1)[:, None], (t_all, cb)), F32)
    return pl.pallas_call(
        functools.partial(_conv_kernel, cols=cols, tr=tr, hd=hd, normalize=normalize, scale=scale),
        out_shape=jax.ShapeDtypeStruct((b, heads, l, hd), out_dtype),
        grid=(b, rows // tr, ncb),
        in_specs=[pl.BlockSpec((1, cols, cb), lambda bi, i, c: (bi, jnp.maximum(i * tr - 1, 0), cofs + c)),
                  pl.BlockSpec((1, n, cb), lambda bi, i, c: (bi, i, cofs + c)),
                  pl.BlockSpec((1, cols, cb), lambda bi, i, c: (bi, jnp.minimum((i + 1) * tr, rows - 1), cofs + c)),
                  pl.BlockSpec((3, 3, cb), lambda bi, i, c: (0, 0, cofs + c)),
                  _const_spec((t_all, cb)), _const_spec((t_all, cb))],
        out_specs=pl.BlockSpec((1, cb // hd, n, hd), lambda bi, i, c: (bi, c, i, 0)),
        compiler_params=_cparams("parallel", "parallel", "parallel"), name="conv_silu",
    )(qkv, qkv, qkv, w, ml, mr)


def _prep_kernel(q_ref, k_ref, v_ref, g_ref, wq_ref, a2_ref, u_ref, eg_ref, *, nheads, cs, nck):
    h = pl.program_id(1)
    ri = lax.broadcasted_iota(jnp.int32, (cs, cs), 0)
    ci = lax.broadcasted_iota(jnp.int32, (cs, cs), 1)
    eye = ri == ci
    xor = ri ^ ci
    lane = lax.broadcasted_iota(jnp.int32, (cs, GB_LANES), 1)
    nt = (((1,), (1,)), ((), ()))
    levels = int(math.log2(cs))
    chains = []
    for c in range(nck):
        rows = slice(c * cs, (c + 1) * cs)
        q = q_ref[0, 0, rows, :].astype(F32)
        k = k_ref[0, 0, rows, :].astype(F32)
        v = v_ref[0, 0, rows, :].astype(F32)
        kbf = k.astype(BF16)
        kk = lax.dot_general(kbf, kbf, nt, preferred_element_type=F32)
        qk = lax.dot_general(q.astype(BF16), kbf, nt, preferred_element_type=F32)
        k_t = k.T
        gbv = g_ref[0, rows, :]
        for d in range(2):
            incl = (ri >= ci) if d == 0 else (ri <= ci)
            strict = (ri > ci) if d == 0 else (ri < ci)
            cum = (ri <= ci) if d == 0 else (ri >= ci)
            g_col = jnp.sum(jnp.where(lane == d * nheads + h, gbv, 0.0), axis=1, keepdims=True)
            b_col = jnp.sum(jnp.where(lane == (2 + d) * nheads + h, gbv, 0.0), axis=1, keepdims=True)
            gc_row = jnp.sum(jnp.where(cum, jnp.broadcast_to(g_col, (cs, cs)), 0.0), axis=0, keepdims=True)
            gc_col = jnp.sum(jnp.where(eye, jnp.broadcast_to(gc_row, (cs, cs)), 0.0), axis=1, keepdims=True)
            decay = jnp.where(incl, jnp.exp(jnp.where(incl, gc_col - gc_row, 0.0)), 0.0)
            egc = jnp.exp(gc_col)
            g_tot = jnp.sum(g_col, axis=0, keepdims=True)
            wq_ref[0, 0, d, c, cs:, :] = (q * egc).astype(wq_ref.dtype)
            a2_ref[0, 0, d, c, :cs, :] = (qk * decay).astype(a2_ref.dtype)
            a2_ref[0, 0, d, c, cs:, :] = (k_t * jnp.exp(g_tot - gc_row)).astype(a2_ref.dtype)
            eg_ref[0, 0, d, c] = jnp.broadcast_to(jnp.exp(g_tot), eg_ref.shape[-2:])
            rhs = jnp.concatenate([v * b_col, k * (b_col * egc)], axis=1).astype(BF16)
            chains.append((d, c, jnp.where(strict, kk * b_col * decay, 0.0), rhs))
    dmats = [jnp.where(eye, 1.0, 0.0) - jnp.where(xor == 1, a_off, 0.0) for _, _, a_off, _ in chains]
    neg_offs = [(-a_off).astype(BF16) for _, _, a_off, _ in chains]
    for lv in range(1, levels):
        sel = jnp.where((xor >> lv) == 1, 1.0, 0.0).astype(BF16)
        dmbs = [dm.astype(BF16) for dm in dmats]
        des = [jnp.dot(db, no * sel, preferred_element_type=F32) for db, no in zip(dmbs, neg_offs)]
        dmats = [dm + jnp.dot(de.astype(BF16), db, preferred_element_type=F32)
                 for dm, de, db in zip(dmats, des, dmbs)]
    sols = [jnp.dot(dm.astype(BF16), ch[3], preferred_element_type=F32) for dm, ch in zip(dmats, chains)]
    dv = u_ref.shape[-1]
    for (d, c, _, _), sol in zip(chains, sols):
        u_ref[0, 0, d, c] = sol[:, :dv].astype(u_ref.dtype)
        wq_ref[0, 0, d, c, :cs, :] = sol[:, dv:].astype(wq_ref.dtype)


def _prep(q, k, v, gb, cs=SCAN_CHUNK, nck=4):
    b, nh, l, dk = q.shape
    dv = v.shape[-1]
    nc = l // cs
    nck = min(nck, nc)
    tl = nck * cs
    tok = lambda wd: pl.BlockSpec((1, 1, tl, wd), lambda bi, hi, n: (bi, hi, n, 0))
    out = lambda r, wd: pl.BlockSpec((1, 1, 2, nck, r, wd), lambda bi, hi, n: (bi, hi, 0, n, 0, 0))
    return pl.pallas_call(
        functools.partial(_prep_kernel, nheads=nh, cs=cs, nck=nck),
        out_shape=[jax.ShapeDtypeStruct((b, nh, 2, nc, 2 * cs, dk), BF16),
                   jax.ShapeDtypeStruct((b, nh, 2, nc, cs + dk, cs), BF16),
                   jax.ShapeDtypeStruct((b, nh, 2, nc, cs, dv), BF16),
                   jax.ShapeDtypeStruct((b, nh, 2, nc, 8, 128), F32)],
        grid=(b, nh, nc // nck),
        in_specs=[tok(dk), tok(dk), tok(dv), pl.BlockSpec((1, tl, GB_LANES), lambda bi, hi, n: (bi, n, 0))],
        out_specs=[out(2 * cs, dk), out(cs + dk, cs), out(cs, dv), out(8, 128)],
        compiler_params=_cparams("parallel", "parallel", "parallel"), name="delta_prep",
    )(q, k, v, gb)


def _scan_kernel(wqf_ref, a2f_ref, uf_ref, egf_ref, wqb_ref, a2b_ref, ub_ref, egb_ref, s0_ref,
                 of_ref, ob_ref, sfin_ref, s_scr, *, hb, cs):
    n = pl.program_id(2)

    @pl.when(n == 0)
    def _():
        s_scr[...] = s0_ref[0]

    dirs = ((wqf_ref, a2f_ref, uf_ref, egf_ref, of_ref), (wqb_ref, a2b_ref, ub_ref, egb_ref, ob_ref))
    chains = [(j, d) + refs for j in range(hb) for d, refs in enumerate(dirs)]
    r1s = [jnp.dot(wq_ref[0, j, 0, 0], s_scr[j, d].astype(BF16), preferred_element_type=F32)
           for j, d, wq_ref, _, _, _, _ in chains]
    vns = [(u_ref[0, j, 0, 0].astype(F32) - r1[:cs]).astype(BF16)
           for (j, _, _, _, u_ref, _, _), r1 in zip(chains, r1s)]
    r2s = [jnp.dot(a2_ref[0, j, 0, 0], vn, preferred_element_type=F32)
           for (j, _, _, a2_ref, _, _, _), vn in zip(chains, vns)]
    for (j, d, _, _, _, eg_ref, o_ref), r1, r2 in zip(chains, r1s, r2s):
        o_ref[0, j] = (r1[cs:] + r2[:cs]).astype(o_ref.dtype)
        s_scr[j, d] = s_scr[j, d] * eg_ref[0, j, 0, 0][0:1, 0:1] + r2[cs:]

    @pl.when(n == pl.num_programs(2) - 1)
    def _():
        sfin_ref[0] = s_scr[...]


def _scan(wq, a2, u, eg, s0, hb=8):
    b, nh, _, nc, cs, dv = u.shape
    dk = wq.shape[-1]
    spec = lambda r, wd, d, rev: pl.BlockSpec(
        (1, hb, 1, 1, r, wd), lambda bi, hi, n: (bi, hi, d, (nc - 1 - n) if rev else n, 0, 0))
    both = lambda r, wd: [spec(r, wd, 0, False), spec(r, wd, 1, True)]
    ins = [both(2 * cs, dk), both(cs + dk, cs), both(cs, dv), both(8, 128)]
    st = pl.BlockSpec((1, hb, 2, dk, dv), lambda bi, hi, n: (bi, hi, 0, 0, 0))
    return pl.pallas_call(
        functools.partial(_scan_kernel, hb=hb, cs=cs),
        out_shape=[jax.ShapeDtypeStruct((b, nh, nc * cs, dv), BF16),
                   jax.ShapeDtypeStruct((b, nh, nc * cs, dv), BF16),
                   jax.ShapeDtypeStruct((b, nh, 2, dk, dv), F32)],
        grid=(b, nh // hb, nc),
        in_specs=[s[0] for s in ins] + [s[1] for s in ins] + [st],
        out_specs=[pl.BlockSpec((1, hb, cs, dv), lambda bi, hi, n: (bi, hi, n, 0)),
                   pl.BlockSpec((1, hb, cs, dv), lambda bi, hi, n: (bi, hi, nc - 1 - n, 0)),
                   st],
        scratch_shapes=[pltpu.VMEM((hb, 2, dk, dv), F32)],
        compiler_params=_cparams("parallel", "parallel", "arbitrary"), name="delta_scan",
    )(wq, a2, u, eg, wq, a2, u, eg, s0)


def _dn_out_kernel(of_ref, ob_ref, z_ref, res_ref, mod_ref, ng_ref, w_ref, fg_ref, o_ref, *, nheads, dv, d):
    acc = None
    for hd in range(nheads):
        sl = slice(hd * dv, (hd + 1) * dv)
        o = of_ref[0, hd].astype(F32) + ob_ref[0, hd].astype(F32)
        o = o * lax.rsqrt(jnp.mean(o * o, axis=-1, keepdims=True) + EPS) * ng_ref[...]
        y = (o * _silu(z_ref[0, :, sl].astype(F32))).astype(BF16)
        p = jnp.dot(y, w_ref[sl, :], preferred_element_type=F32)
        acc = p if acc is None else acc + p
    gate = mod_ref[0, :, 2 * d:3 * d]
    hn = res_ref[0] + gate * acc
    o_ref[0] = hn * lax.rsqrt(jnp.mean(hn * hn, axis=-1, keepdims=True) + EPS) * fg_ref[...]


def _dn_out(o_f, o_b, z, res, mod, norm_g, w_out, final_g, tm=512):
    b, nh, l, dv = o_f.shape
    d = res.shape[-1]
    tm = min(tm, l)
    tok = lambda wd: pl.BlockSpec((1, tm, wd), lambda i, j: (i, j, 0))
    hs = pl.BlockSpec((1, nh, tm, dv), lambda i, j: (i, 0, j, 0))
    return pl.pallas_call(
        functools.partial(_dn_out_kernel, nheads=nh, dv=dv, d=d),
        out_shape=jax.ShapeDtypeStruct((b, l, d), F32),
        grid=(b, l // tm),
        in_specs=[hs, hs, tok(nh * dv), tok(d),
                  pl.BlockSpec((1, 1, 3 * d), lambda i, j: (i, 0, 0)),
                  _const_spec((1, dv)), _const_spec(w_out.shape), _const_spec((1, d))],
        out_specs=tok(d),
        compiler_params=_cparams("parallel", "parallel"), name="delta_out",
    )(o_f, o_b, z, res, mod, norm_g.reshape(1, dv), w_out, final_g.reshape(1, d))


def _fourier_layer(h_lat, h_ctx, norm_g, mod_lat, mod_ctx, w_in, w_grp, w_out):
    b, l, _ = h_lat.shape
    lc = h_ctx.shape[1]
    width = w_in.shape[1] // 2
    gd = w_grp.shape[-1]
    w_in = w_in.astype(BF16)
    w_out = w_out.astype(BF16)
    gc, gs = _grp(w_grp)

    u, z = _nmm(h_lat, norm_g, mod_lat, w_in, (width, width), (BF16, BF16))
    n2 = DFT_N2
    n1 = l // n2
    y = _dft1(u, n1, n2)
    xr, xi = _dft2_lat(_dft_complex_matrix(n2), y)
    xr = xr.reshape(b, l, width)
    xi = xi.reshape(b, l, width)
    out_lat = _fn_out(xr, xi, z, h_lat, mod_lat, gc, gs, w_out, 1.0 / math.sqrt(l * gd))

    uc, zc = _nmm(h_ctx, norm_g, mod_ctx, w_in, (width, width), (BF16, BF16))
    xrc, xic = _dft2(_dft_real_matrix(lc), uc)
    out_ctx = _fn_out(xrc, xic, zc, h_ctx, mod_ctx, gc, gs, w_out, 1.0 / math.sqrt(lc * gd))
    return out_lat, out_ctx


def _delta_project(h, norm_g, mod, w_cat, abp, conv_w, rows, cols):
    qk_w = DN_HEADS * DN_DK
    v_w = DN_HEADS * DN_DV
    qkv, z, gb = _nmm(h, norm_g, mod, w_cat, (2 * qk_w + v_w, v_w, GB_LANES), (BF16, BF16, F32),
                      abp=abp, gb_heads=DN_HEADS)
    q = _conv(qkv, conv_w, rows, cols, 0, DN_HEADS, DN_DK, True, DN_DK ** -0.5, BF16)
    k = _conv(qkv, conv_w, rows, cols, qk_w, DN_HEADS, DN_DK, True, 1.0, BF16)
    v = _conv(qkv, conv_w, rows, cols, 2 * qk_w, DN_HEADS, DN_DV, False, 1.0, BF16)
    return q, k, v, z, gb


def _delta_layer_final(h_lat, h_ctx, norm_g, mod_lat, mod_ctx, w_in, conv_w, a_log, dt_bias, dn_norm_g, w_out,
                       final_g):
    b, l, d = h_lat.shape
    lc = h_ctx.shape[1]
    main_w = 2 * DN_HEADS * DN_DK + 2 * DN_HEADS * DN_DV
    n_ab = 4 * DN_HEADS
    w_cat = jnp.concatenate(
        [w_in[:, :main_w], w_in[:, main_w:], jnp.zeros((d, GB_LANES - n_ab), w_in.dtype)], axis=1).astype(BF16)
    pad = jnp.zeros((GB_LANES - 2 * DN_HEADS,), F32)
    abp = jnp.stack([jnp.concatenate([a_log.reshape(-1).astype(F32), pad]),
                     jnp.concatenate([dt_bias.reshape(-1).astype(F32), pad])])
    qc, kc, vc, _, gbc = _delta_project(h_ctx, norm_g, mod_ctx, w_cat, abp, conv_w, 1, lc)
    ql, kl, vl, zl, gbl = _delta_project(h_lat, norm_g, mod_lat, w_cat, abp, conv_w, l // GRID_W, GRID_W)
    s0 = jnp.zeros((b, DN_HEADS, 2, DN_DK, DN_DV), F32)
    _, _, s_ctx = _scan(*_prep(qc, kc, vc, gbc), s0)
    o_f, o_b, _ = _scan(*_prep(ql, kl, vl, gbl), s_ctx)
    return _dn_out(o_f, o_b, zl, h_lat, mod_lat, dn_norm_g, w_out.astype(BF16), final_g)


def kernel(x, c, ctx, c_ctx, mod_w, mod_b, norm_g, final_g, fn_w_in, fn_w_grp, fn_w_out,
           dn_w_in, dn_conv, dn_a_log, dn_dt_bias, dn_norm_g, dn_w_out):
    b, l, d = x.shape
    cond = jnp.concatenate([c, c_ctx[None, :], jnp.zeros((8 - b - 1, d), c.dtype)], axis=0)
    mods = _mods(cond, mod_w, mod_b)
    mod_lat = [mods[i, :b].reshape(b, 1, 3 * d) for i in range(2)]
    mod_ctx = [mods[i, b:b + 1].reshape(1, 1, 3 * d) for i in range(2)]
    h_lat, h_ctx = _fourier_layer(x, ctx, norm_g[0], mod_lat[0], mod_ctx[0], fn_w_in[0], fn_w_grp[0], fn_w_out[0])
    return _delta_layer_final(h_lat, h_ctx, norm_g[1], mod_lat[1], mod_ctx[1], dn_w_in[0], dn_conv[0],
                              dn_a_log[0], dn_dt_bias[0], dn_norm_g[0], dn_w_out[0], final_g)
```

```python
import functools
import math

import numpy as np
import jax
import jax.numpy as jnp
from jax import lax
from jax.experimental import pallas as pl
from jax.experimental.pallas import tpu as pltpu

F32 = jnp.float32
BF16 = jnp.bfloat16
HI = lax.Precision.HIGHEST
EPS = 1e-6
LOG2E = 1.4426950408889634

GRID_W = 64
FN_GROUPS = 8
DN_HEADS = 8
DN_DK = 128
DN_DV = 256
SCAN_CHUNK = 128
DFT_N2 = 128
GB_LANES = 128
VMEM_LIMIT_BYTES = 48 * 1024 * 1024


def _cparams(*sem):
    return pltpu.CompilerParams(dimension_semantics=sem, vmem_limit_bytes=VMEM_LIMIT_BYTES)


def _const_spec(shape):
    zeros = (0,) * len(shape)
    return pl.BlockSpec(shape, lambda *_: zeros)


def _silu(v):
    return v * jax.nn.sigmoid(v)


def _mods_kernel(cond_ref, w_ref, b_ref, o_ref):
    s = _silu(cond_ref[...])
    o_ref[0] = jnp.dot(s, w_ref[0], preferred_element_type=F32, precision=HI) + b_ref[0]


def _mods(cond, mod_w, mod_b):
    depth, d, d3 = mod_w.shape
    rows = cond.shape[0]
    tn = 512
    return pl.pallas_call(
        _mods_kernel,
        out_shape=jax.ShapeDtypeStruct((depth, rows, d3), F32),
        grid=(depth, d3 // tn),
        in_specs=[pl.BlockSpec((rows, d), lambda i, j: (0, 0)),
                  pl.BlockSpec((1, d, tn), lambda i, j: (i, 0, j)),
                  pl.BlockSpec((1, 1, tn), lambda i, j: (i, 0, j))],
        out_specs=pl.BlockSpec((1, rows, tn), lambda i, j: (i, 0, j)),
        compiler_params=_cparams("parallel", "parallel"), name="adaln_mods",
    )(cond, mod_w, mod_b.reshape(depth, 1, d3))


def _nmm_kernel(*refs, splits, d, gb_heads):
    if gb_heads:
        x_ref, g_ref, mod_ref, w_ref, abp_ref = refs[:5]
        outs = refs[5:]
    else:
        x_ref, g_ref, mod_ref, w_ref = refs[:4]
        outs = refs[4:]
    x = x_ref[0]
    y = x * lax.rsqrt(jnp.mean(x * x, axis=-1, keepdims=True) + EPS) * g_ref[...]
    shift = mod_ref[0, :, 0:d]
    scale = mod_ref[0, :, d:2 * d]
    n = (y * (1.0 + scale) + shift).astype(BF16)
    c0 = 0
    for idx, (o_ref, width) in enumerate(zip(outs, splits)):
        step = min(width, 1024)
        for s in range(0, width, step):
            r = jnp.dot(n, w_ref[:, c0 + s:c0 + s + step], preferred_element_type=F32)
            if gb_heads and idx == len(splits) - 1:
                a_log = abp_ref[0:1, :]
                t = r + abp_ref[1:2, :]
                softplus = jnp.maximum(t, 0.0) + jnp.log(1.0 + jnp.exp(-jnp.abs(t)))
                lane = lax.broadcasted_iota(jnp.int32, r.shape, 1)
                r = jnp.where(lane < 2 * gb_heads, (-LOG2E) * jnp.exp(a_log) * softplus, jax.nn.sigmoid(r))
            o_ref[0, :, s:s + step] = r.astype(o_ref.dtype)
        c0 += width


def _nmm(x, g, mod, w, splits, dtypes, abp=None, gb_heads=0, tm=512):
    b, l, d = x.shape
    tm = min(tm, l)
    ntot = w.shape[1]
    bm = mod.shape[0]
    mod_map = (lambda i, j: (i, 0, 0)) if bm > 1 else (lambda i, j: (0, 0, 0))
    in_specs = [pl.BlockSpec((1, tm, d), lambda i, j: (i, j, 0)),
                _const_spec((1, d)),
                pl.BlockSpec((1, 1, 3 * d), mod_map),
                _const_spec((d, ntot))]
    args = [x, g.reshape(1, d), mod, w]
    if gb_heads:
        in_specs.append(_const_spec(abp.shape))
        args.append(abp)
    return pl.pallas_call(
        functools.partial(_nmm_kernel, splits=splits, d=d, gb_heads=gb_heads),
        out_shape=[jax.ShapeDtypeStruct((b, l, wd), dt) for wd, dt in zip(splits, dtypes)],
        grid=(b, l // tm),
        in_specs=in_specs,
        out_specs=[pl.BlockSpec((1, tm, wd), lambda i, j: (i, j, 0)) for wd in splits],
        compiler_params=_cparams("parallel", "parallel"), name="norm_proj",
    )(*args)


def _dft_cos_sin(n):
    k = np.arange(n)
    ang = 2.0 * np.pi * ((k[:, None] * k[None, :]) % n) / n
    return np.cos(ang), np.sin(ang)


def _dft1_kernel(m_ref, u_ref, cos_ref, sin_ref, o_ref, *, n1, tn2):
    ut = jnp.transpose(u_ref[0], (1, 0, 2))
    for j in range(tn2):
        r = jnp.dot(m_ref[...], ut[j], preferred_element_type=F32)
        yr = r[:n1]
        yi = r[n1:]
        cs = cos_ref[j]
        sn = sin_ref[j]
        o_ref[0, 0, j] = (yr * cs + yi * sn).astype(o_ref.dtype)
        o_ref[0, 1, j] = (yi * cs - yr * sn).astype(o_ref.dtype)


def _dft1(u, n1, n2, tn2=16, tc=1024):
    b, l, c = u.shape
    cm, sm = _dft_cos_sin(n1)
    m1 = jnp.asarray(np.concatenate([cm, -sm], axis=0), F32).astype(BF16)
    ang = 2.0 * np.pi * (np.arange(n2)[:, None] * np.arange(n1)[None, :]) / l
    tw_cos = jnp.asarray(np.cos(ang)[:, :, None], F32)
    tw_sin = jnp.asarray(np.sin(ang)[:, :, None], F32)
    return pl.pallas_call(
        functools.partial(_dft1_kernel, n1=n1, tn2=tn2),
        out_shape=jax.ShapeDtypeStruct((b, 2, n2, n1, c), BF16),
        grid=(b, n2 // tn2, c // tc),
        in_specs=[_const_spec((2 * n1, n1)),
                  pl.BlockSpec((1, n1, tn2, tc), lambda i, j, k: (i, 0, j, k)),
                  pl.BlockSpec((tn2, n1, 1), lambda i, j, k: (j, 0, 0)),
                  pl.BlockSpec((tn2, n1, 1), lambda i, j, k: (j, 0, 0))],
        out_specs=pl.BlockSpec((1, 2, tn2, n1, tc), lambda i, j, k: (i, 0, j, 0, k)),
        compiler_params=_cparams("parallel", "parallel", "parallel"), name="dft_stage1",
    )(m1, u.reshape(b, n1, n2, c), tw_cos, tw_sin)


def _dft2_lat_kernel(m_ref, y_ref, xr_ref, xi_ref, *, n2, tk1):
    yr = jnp.transpose(y_ref[0, 0], (1, 0, 2))
    yi = jnp.transpose(y_ref[0, 1], (1, 0, 2))
    xr, xi = [], []
    for j in range(tk1):
        r = jnp.dot(m_ref[...], jnp.concatenate([yr[j], yi[j]], axis=0), preferred_element_type=F32)
        xr.append(r[:n2].astype(xr_ref.dtype))
        xi.append(r[n2:].astype(xi_ref.dtype))
    xr_ref[0] = jnp.transpose(jnp.stack(xr), (1, 0, 2))
    xi_ref[0] = jnp.transpose(jnp.stack(xi), (1, 0, 2))


def _dft2_lat(m, y, tk1=16, tc=256):
    b, _, n2, n1, c = y.shape
    tk1 = min(tk1, n1)
    return pl.pallas_call(
        functools.partial(_dft2_lat_kernel, n2=n2, tk1=tk1),
        out_shape=[jax.ShapeDtypeStruct((b, n2, n1, c), BF16)] * 2,
        grid=(b, n1 // tk1, c // tc),
        in_specs=[_const_spec(m.shape),
                  pl.BlockSpec((1, 2, n2, tk1, tc), lambda i, j, k: (i, 0, 0, j, k))],
        out_specs=[pl.BlockSpec((1, n2, tk1, tc), lambda i, j, k: (i, 0, j, k))] * 2,
        compiler_params=_cparams("parallel", "parallel", "parallel"), name="dft_stage2_lat",
    )(m, y)


def _dft2_kernel(m_ref, y_ref, xr_ref, xi_ref, *, half):
    r = jnp.dot(m_ref[...], y_ref[0], preferred_element_type=F32)
    xr_ref[0] = r[:half].astype(xr_ref.dtype)
    xi_ref[0] = r[half:].astype(xi_ref.dtype)


def _dft2(m, y, tc=2048):
    b, k, cols = y.shape
    half = m.shape[0] // 2
    return pl.pallas_call(
        functools.partial(_dft2_kernel, half=half),
        out_shape=[jax.ShapeDtypeStruct((b, half, cols), BF16)] * 2,
        grid=(b, cols // tc),
        in_specs=[_const_spec(m.shape),
                  pl.BlockSpec((1, k, tc), lambda i, j: (i, 0, j))],
        out_specs=[pl.BlockSpec((1, half, tc), lambda i, j: (i, 0, j))] * 2,
        compiler_params=_cparams("parallel", "parallel"), name="dft_stage2",
    )(m, y)


def _dft_real_matrix(n):
    cm, sm = _dft_cos_sin(n)
    return jnp.asarray(np.concatenate([cm, -sm], axis=0), F32).astype(BF16)


def _dft_complex_matrix(n):
    cm, sm = _dft_cos_sin(n)
    return jnp.asarray(np.block([[cm, sm], [-sm, cm]]), F32).astype(BF16)


def _grp_kernel(m_ref, w_ref, gc_ref, gs_ref, *, n):
    r = jnp.dot(m_ref[...], w_ref[0], preferred_element_type=F32, precision=HI)
    gc_ref[0] = r[:n].astype(gc_ref.dtype)
    gs_ref[0] = r[n:].astype(gs_ref.dtype)


def _grp(w_grp):
    groups, n, _ = w_grp.shape
    cm, sm = _dft_cos_sin(n)
    m = jnp.asarray(np.concatenate([cm, sm], axis=0), F32)
    return pl.pallas_call(
        functools.partial(_grp_kernel, n=n),
        out_shape=[jax.ShapeDtypeStruct((groups, n, n), BF16)] * 2,
        grid=(groups,),
        in_specs=[_const_spec((2 * n, n)),
                  pl.BlockSpec((1, n, n), lambda g: (g, 0, 0))],
        out_specs=[pl.BlockSpec((1, n, n), lambda g: (g, 0, 0))] * 2,
        compiler_params=_cparams("parallel"), name="group_weights",
    )(m, w_grp)


def _fn_out_kernel(xr_ref, xi_ref, z_ref, res_ref, mod_ref, gc_ref, gs_ref, w_ref, o_ref, *, groups, gd, d, scale):
    acc = None
    for g in range(groups):
        sl = slice(g * gd, (g + 1) * gd)
        f = (jnp.dot(xr_ref[0, :, sl], gc_ref[g], preferred_element_type=F32)
             + jnp.dot(xi_ref[0, :, sl], gs_ref[g], preferred_element_type=F32))
        y = (f * scale * _silu(z_ref[0, :, sl].astype(F32))).astype(BF16)
        p = jnp.dot(y, w_ref[sl, :], preferred_element_type=F32)
        acc = p if acc is None else acc + p
    gate = mod_ref[0, :, 2 * d:3 * d]
    o_ref[0] = res_ref[0] + gate * acc


def _fn_out(xr, xi, z, res, mod, gc, gs, w_out, scale, tm=512):
    b, l, c = xr.shape
    d = res.shape[-1]
    tm = min(tm, l)
    groups, gd, _ = gc.shape
    bm = mod.shape[0]
    mod_map = (lambda i, j: (i, 0, 0)) if bm > 1 else (lambda i, j: (0, 0, 0))
    tok = lambda wd: pl.BlockSpec((1, tm, wd), lambda i, j: (i, j, 0))
    return pl.pallas_call(
        functools.partial(_fn_out_kernel, groups=groups, gd=gd, d=d, scale=scale),
        out_shape=jax.ShapeDtypeStruct((b, l, d), F32),
        grid=(b, l // tm),
        in_specs=[tok(c), tok(c), tok(c), tok(d),
                  pl.BlockSpec((1, 1, 3 * d), mod_map),
                  _const_spec(gc.shape), _const_spec(gs.shape), _const_spec(w_out.shape)],
        out_specs=tok(d),
        compiler_params=_cparams("parallel", "parallel"), name="fourier_out",
    )(xr, xi, z, res, mod, gc, gs, w_out)


def _conv_kernel(prev_ref, main_ref, next_ref, w_ref, ml_ref, mr_ref, o_ref, *, cols, tr, hd, normalize, scale):
    i = pl.program_id(1)
    last = pl.num_programs(1) - 1
    prev = prev_ref[0].astype(F32) * (i > 0).astype(F32)
    nxt = next_ref[0].astype(F32) * (i < last).astype(F32)
    e = jnp.concatenate([prev, main_ref[0].astype(F32), nxt], axis=0)
    t_all = e.shape[0]
    shifted = (pltpu.roll(e, 1, axis=0) * ml_ref[...], e, pltpu.roll(e, t_all - 1, axis=0) * mr_ref[...])
    n = tr * cols
    acc = None
    for dr in range(3):
        for dc in range(3):
            term = shifted[dc][dr * cols:dr * cols + n] * w_ref[dr, dc:dc + 1, :]
            acc = term if acc is None else acc + term
    y = _silu(acc)
    for j in range(y.shape[1] // hd):
        yh = y[:, j * hd:(j + 1) * hd]
        if normalize:
            yh = yh * lax.rsqrt(jnp.sum(yh * yh, axis=-1, keepdims=True) + EPS) * scale
        o_ref[0, j] = yh.astype(o_ref.dtype)


def _conv(qkv, w, rows, cols, ch_start, heads, hd, normalize, scale, out_dtype, cb=512):
    b, l, ctot = qkv.shape
    tr = min(rows, max(1, 512 // cols))
    n = tr * cols
    t_all = (tr + 2) * cols
    cofs = ch_start // cb
    ncb = heads * hd // cb
    col = np.arange(t_all) % cols
    ml = jnp.asarray(np.broadcast_to((col != 0)[:, None], (t_all, cb)), F32)
    mr = jnp.asarray(np.broadcast_to((col != cols - 1)[:, None], (t_all, cb)), F32)
    return pl.pallas_call(
        functools.partial(_conv_kernel, cols=cols, tr=tr, hd=hd, normalize=normalize, scale=scale),
        out_shape=jax.ShapeDtypeStruct((b, heads, l, hd), out_dtype),
        grid=(b, rows // tr, ncb),
        in_specs=[pl.BlockSpec((1, cols, cb), lambda bi, i, c: (bi, jnp.maximum(i * tr - 1, 0), cofs + c)),
                  pl.BlockSpec((1, n, cb), lambda bi, i, c: (bi, i, cofs + c)),
                  pl.BlockSpec((1, cols, cb), lambda bi, i, c: (bi, jnp.minimum((i + 1) * tr, rows - 1), cofs + c)),
                  pl.BlockSpec((3, 3, cb), lambda bi, i, c: (0, 0, cofs + c)),
                  _const_spec((t_all, cb)), _const_spec((t_all, cb))],
        out_specs=pl.BlockSpec((1, cb // hd, n, hd), lambda bi, i, c: (bi, c, i, 0)),
        compiler_params=_cparams("parallel", "parallel", "parallel"), name="conv_silu",
    )(qkv, qkv, qkv, w, ml, mr)


def _prep_kernel(q_ref, k_ref, v_ref, g_ref, wq_ref, a2_ref, u_ref, eg_ref, *, nheads, cs):
    ri = lax.broadcasted_iota(jnp.int32, (cs, cs), 0)
    ci = lax.broadcasted_iota(jnp.int32, (cs, cs), 1)
    eye = ri == ci
    xor = ri ^ ci
    nt = (((1,), (1,)), ((), ()))
    levels = int(math.log2(cs))
    gbv = g_ref[0]
    hi = gbv.astype(BF16)
    rem = gbv - hi.astype(F32)
    mid = rem.astype(BF16)
    parts = (hi, mid, (rem - mid.astype(F32)).astype(BF16))
    cums, cums_t = [], []
    for tri in (ri >= ci, ri <= ci):
        t01 = jnp.where(tri, 1.0, 0.0).astype(BF16)
        cum = sum(jnp.dot(t01, p, preferred_element_type=F32) for p in parts)
        cums.append(cum)
        cums_t.append(cum.T)
    chains = []
    for hd in range(nheads):
        qb = q_ref[0, hd]
        kb = k_ref[0, hd]
        k = kb.astype(F32)
        v = v_ref[0, hd].astype(F32)
        kk = lax.dot_general(kb, kb, nt, preferred_element_type=F32)
        qk = lax.dot_general(qb, kb, nt, preferred_element_type=F32)
        k_t = k.T
        for d in range(2):
            incl = (ri >= ci) if d == 0 else (ri <= ci)
            strict = (ri > ci) if d == 0 else (ri < ci)
            lg = d * nheads + hd
            lb = (2 + d) * nheads + hd
            gc_col = cums[d][:, lg:lg + 1]
            gc_row = cums_t[d][lg:lg + 1, :]
            g_tot = cums[0][cs - 1:cs, lg:lg + 1]
            b_col = gbv[:, lb:lb + 1]
            ex = jnp.exp2(gc_col - gc_row)
            egc = jnp.exp2(gc_col)
            wq_ref[0, hd, d, 0, cs:, :] = (qb.astype(F32) * egc).astype(wq_ref.dtype)
            a2_ref[0, hd, d, 0, :cs, :] = jnp.where(incl, qk * ex, 0.0).astype(a2_ref.dtype)
            a2_ref[0, hd, d, 0, cs:, :] = (k_t * jnp.exp2(g_tot - gc_row)).astype(a2_ref.dtype)
            eg_ref[0, hd, d, 0] = jnp.broadcast_to(jnp.exp2(g_tot), eg_ref.shape[-2:])
            rhs = jnp.concatenate([v * b_col, k * (b_col * egc)], axis=1).astype(BF16)
            neg_off = jnp.where(strict, kk * (-b_col) * ex, 0.0)
            chains.append((hd, d, neg_off, rhs))
    dmats = [jnp.where(eye, 1.0, jnp.where(xor == 1, no, 0.0)) for _, _, no, _ in chains]
    neg_offs = [no.astype(BF16) for _, _, no, _ in chains]
    for lv in range(1, levels):
        sel = jnp.where((xor >> lv) == 1, 1.0, 0.0).astype(BF16)
        dmbs = [dm.astype(BF16) for dm in dmats]
        des = [jnp.dot(db, no * sel, preferred_element_type=F32) for db, no in zip(dmbs, neg_offs)]
        dmats = [dm + jnp.dot(de.astype(BF16), db, preferred_element_type=F32)
                 for dm, de, db in zip(dmats, des, dmbs)]
    sols = [jnp.dot(dm.astype(BF16), ch[3], preferred_element_type=F32) for dm, ch in zip(dmats, chains)]
    dv = u_ref.shape[-1]
    for (hd, d, _, _), sol in zip(chains, sols):
        u_ref[0, hd, d, 0] = sol[:, :dv].astype(u_ref.dtype)
        wq_ref[0, hd, d, 0, :cs, :] = sol[:, dv:].astype(wq_ref.dtype)


def _prep(q, k, v, gb, cs=SCAN_CHUNK):
    b, nh, l, dk = q.shape
    dv = v.shape[-1]
    nc = l // cs
    tok = lambda wd: pl.BlockSpec((1, nh, cs, wd), lambda bi, n: (bi, 0, n, 0))
    out = lambda r, wd: pl.BlockSpec((1, nh, 2, 1, r, wd), lambda bi, n: (bi, 0, 0, n, 0, 0))
    return pl.pallas_call(
        functools.partial(_prep_kernel, nheads=nh, cs=cs),
        out_shape=[jax.ShapeDtypeStruct((b, nh, 2, nc, 2 * cs, dk), BF16),
                   jax.ShapeDtypeStruct((b, nh, 2, nc, cs + dk, cs), BF16),
                   jax.ShapeDtypeStruct((b, nh, 2, nc, cs, dv), BF16),
                   jax.ShapeDtypeStruct((b, nh, 2, nc, 8, 128), F32)],
        grid=(b, nc),
        in_specs=[tok(dk), tok(dk), tok(dv), pl.BlockSpec((1, cs, GB_LANES), lambda bi, n: (bi, n, 0))],
        out_specs=[out(2 * cs, dk), out(cs + dk, cs), out(cs, dv), out(8, 128)],
        compiler_params=_cparams("parallel", "parallel"), name="delta_prep",
    )(q, k, v, gb)


def _scan_kernel(wqf_ref, a2f_ref, uf_ref, egf_ref, wqb_ref, a2b_ref, ub_ref, egb_ref, s0_ref,
                 of_ref, ob_ref, sfin_ref, s_scr, *, hb, cs):
    n = pl.program_id(2)

    @pl.when(n == 0)
    def _():
        s_scr[...] = s0_ref[0]

    dirs = ((wqf_ref, a2f_ref, uf_ref, egf_ref, of_ref), (wqb_ref, a2b_ref, ub_ref, egb_ref, ob_ref))
    chains = [(j, d) + refs for j in range(hb) for d, refs in enumerate(dirs)]
    r1s = [jnp.dot(wq_ref[0, j, 0, 0], s_scr[j, d].astype(BF16), preferred_element_type=F32)
           for j, d, wq_ref, _, _, _, _ in chains]
    vns = [(u_ref[0, j, 0, 0].astype(F32) - r1[:cs]).astype(BF16)
           for (j, _, _, _, u_ref, _, _), r1 in zip(chains, r1s)]
    r2s = [jnp.dot(a2_ref[0, j, 0, 0], vn, preferred_element_type=F32)
           for (j, _, _, a2_ref, _, _, _), vn in zip(chains, vns)]
    for (j, d, _, _, _, eg_ref, o_ref), r1, r2 in zip(chains, r1s, r2s):
        o_ref[0, j] = (r1[cs:] + r2[:cs]).astype(o_ref.dtype)
        s_scr[j, d] = s_scr[j, d] * eg_ref[0, j, 0, 0][0:1, 0:1] + r2[cs:]

    @pl.when(n == pl.num_programs(2) - 1)
    def _():
        sfin_ref[0] = s_scr[...]


def _scan(wq, a2, u, eg, s0, hb=8):
    b, nh, _, nc, cs, dv = u.shape
    dk = wq.shape[-1]
    spec = lambda r, wd, d, rev: pl.BlockSpec(
        (1, hb, 1, 1, r, wd), lambda bi, hi, n: (bi, hi, d, (nc - 1 - n) if rev else n, 0, 0))
    both = lambda r, wd: [spec(r, wd, 0, False), spec(r, wd, 1, True)]
    ins = [both(2 * cs, dk), both(cs + dk, cs), both(cs, dv), both(8, 128)]
    st = pl.BlockSpec((1, hb, 2, dk, dv), lambda bi, hi, n: (bi, hi, 0, 0, 0))
    return pl.pallas_call(
        functools.partial(_scan_kernel, hb=hb, cs=cs),
        out_shape=[jax.ShapeDtypeStruct((b, nh, nc * cs, dv), BF16),
                   jax.ShapeDtypeStruct((b, nh, nc * cs, dv), BF16),
                   jax.ShapeDtypeStruct((b, nh, 2, dk, dv), F32)],
        grid=(b, nh // hb, nc),
        in_specs=[s[0] for s in ins] + [s[1] for s in ins] + [st],
        out_specs=[pl.BlockSpec((1, hb, cs, dv), lambda bi, hi, n: (bi, hi, n, 0)),
                   pl.BlockSpec((1, hb, cs, dv), lambda bi, hi, n: (bi, hi, nc - 1 - n, 0)),
                   st],
        scratch_shapes=[pltpu.VMEM((hb, 2, dk, dv), F32)],
        compiler_params=_cparams("parallel", "parallel", "arbitrary"), name="delta_scan",
    )(wq, a2, u, eg, wq, a2, u, eg, s0)


def _dn_out_kernel(of_ref, ob_ref, z_ref, res_ref, mod_ref, ng_ref, w_ref, fg_ref, o_ref, *, nheads, dv, d):
    acc = None
    for hd in range(nheads):
        sl = slice(hd * dv, (hd + 1) * dv)
        o = of_ref[0, hd].astype(F32) + ob_ref[0, hd].astype(F32)
        o = o * lax.rsqrt(jnp.mean(o * o, axis=-1, keepdims=True) + EPS) * ng_ref[...]
        y = (o * _silu(z_ref[0, :, sl].astype(F32))).astype(BF16)
        p = jnp.dot(y, w_ref[sl, :], preferred_element_type=F32)
        acc = p if acc is None else acc + p
    gate = mod_ref[0, :, 2 * d:3 * d]
    hn = res_ref[0] + gate * acc
    o_ref[0] = hn * lax.rsqrt(jnp.mean(hn * hn, axis=-1, keepdims=True) + EPS) * fg_ref[...]


def _dn_out(o_f, o_b, z, res, mod, norm_g, w_out, final_g, tm=512):
    b, nh, l, dv = o_f.shape
    d = res.shape[-1]
    tm = min(tm, l)
    tok = lambda wd: pl.BlockSpec((1, tm, wd), lambda i, j: (i, j, 0))
    hs = pl.BlockSpec((1, nh, tm, dv), lambda i, j: (i, 0, j, 0))
    return pl.pallas_call(
        functools.partial(_dn_out_kernel, nheads=nh, dv=dv, d=d),
        out_shape=jax.ShapeDtypeStruct((b, l, d), F32),
        grid=(b, l // tm),
        in_specs=[hs, hs, tok(nh * dv), tok(d),
                  pl.BlockSpec((1, 1, 3 * d), lambda i, j: (i, 0, 0)),
                  _const_spec((1, dv)), _const_spec(w_out.shape), _const_spec((1, d))],
        out_specs=tok(d),
        compiler_params=_cparams("parallel", "parallel"), name="delta_out",
    )(o_f, o_b, z, res, mod, norm_g.reshape(1, dv), w_out, final_g.reshape(1, d))


def _fourier_layer(h_lat, h_ctx, norm_g, mod_lat, mod_ctx, w_in, w_grp, w_out):
    b, l, _ = h_lat.shape
    lc = h_ctx.shape[1]
    width = w_in.shape[1] // 2
    gd = w_grp.shape[-1]
    w_in = w_in.astype(BF16)
    w_out = w_out.astype(BF16)
    gc, gs = _grp(w_grp)

    u, z = _nmm(h_lat, norm_g, mod_lat, w_in, (width, width), (BF16, BF16))
    n2 = DFT_N2
    n1 = l // n2
    y = _dft1(u, n1, n2)
    xr, xi = _dft2_lat(_dft_complex_matrix(n2), y)
    xr = xr.reshape(b, l, width)
    xi = xi.reshape(b, l, width)
    out_lat = _fn_out(xr, xi, z, h_lat, mod_lat, gc, gs, w_out, 1.0 / math.sqrt(l * gd))

    uc, zc = _nmm(h_ctx, norm_g, mod_ctx, w_in, (width, width), (BF16, BF16))
    xrc, xic = _dft2(_dft_real_matrix(lc), uc)
    out_ctx = _fn_out(xrc, xic, zc, h_ctx, mod_ctx, gc, gs, w_out, 1.0 / math.sqrt(lc * gd))
    return out_lat, out_ctx


def _delta_project(h, norm_g, mod, w_cat, abp, conv_w, rows, cols):
    qk_w = DN_HEADS * DN_DK
    v_w = DN_HEADS * DN_DV
    qkv, z, gb = _nmm(h, norm_g, mod, w_cat, (2 * qk_w + v_w, v_w, GB_LANES), (BF16, BF16, F32),
                      abp=abp, gb_heads=DN_HEADS)
    q = _conv(qkv, conv_w, rows, cols, 0, DN_HEADS, DN_DK, True, DN_DK ** -0.5, BF16)
    k = _conv(qkv, conv_w, rows, cols, qk_w, DN_HEADS, DN_DK, True, 1.0, BF16)
    v = _conv(qkv, conv_w, rows, cols, 2 * qk_w, DN_HEADS, DN_DV, False, 1.0, BF16)
    return q, k, v, z, gb


def _delta_layer_final(h_lat, h_ctx, norm_g, mod_lat, mod_ctx, w_in, conv_w, a_log, dt_bias, dn_norm_g, w_out,
                       final_g):
    b, l, d = h_lat.shape
    lc = h_ctx.shape[1]
    main_w = 2 * DN_HEADS * DN_DK + 2 * DN_HEADS * DN_DV
    n_ab = 4 * DN_HEADS
    w_cat = jnp.concatenate(
        [w_in[:, :main_w], w_in[:, main_w:], jnp.zeros((d, GB_LANES - n_ab), w_in.dtype)], axis=1).astype(BF16)
    pad = jnp.zeros((GB_LANES - 2 * DN_HEADS,), F32)
    abp = jnp.stack([jnp.concatenate([a_log.reshape(-1).astype(F32), pad]),
                     jnp.concatenate([dt_bias.reshape(-1).astype(F32), pad])])
    qc, kc, vc, _, gbc = _delta_project(h_ctx, norm_g, mod_ctx, w_cat, abp, conv_w, 1, lc)
    ql, kl, vl, zl, gbl = _delta_project(h_lat, norm_g, mod_lat, w_cat, abp, conv_w, l // GRID_W, GRID_W)
    s0 = jnp.zeros((b, DN_HEADS, 2, DN_DK, DN_DV), F32)
    _, _, s_ctx = _scan(*_prep(qc, kc, vc, gbc), s0)
    o_f, o_b, _ = _scan(*_prep(ql, kl, vl, gbl), s_ctx)
    return _dn_out(o_f, o_b, zl, h_lat, mod_lat, dn_norm_g, w_out.astype(BF16), final_g)


def kernel(x, c, ctx, c_ctx, mod_w, mod_b, norm_g, final_g, fn_w_in, fn_w_grp, fn_w_out,
           dn_w_in, dn_conv, dn_a_log, dn_dt_bias, dn_norm_g, dn_w_out):
    b, l, d = x.shape
    cond = jnp.concatenate([c, c_ctx[None, :], jnp.zeros((8 - b - 1, d), c.dtype)], axis=0)
    mods = _mods(cond, mod_w, mod_b)
    mod_lat = [mods[i, :b].reshape(b, 1, 3 * d) for i in range(2)]
    mod_ctx = [mods[i, b:b + 1].reshape(1, 1, 3 * d) for i in range(2)]
    h_lat, h_ctx = _fourier_layer(x, ctx, norm_g[0], mod_lat[0], mod_ctx[0], fn_w_in[0], fn_w_grp[0], fn_w_out[0])
    return _delta_layer_final(h_lat, h_ctx, norm_g[1], mod_lat[1], mod_ctx[1], dn_w_in[0], dn_conv[0],
                              dn_a_log[0], dn_dt_bias[0], dn_norm_g[0], dn_w_out[0], final_g)
```

```python
import functools
import math

import numpy as np
import jax
import jax.numpy as jnp
from jax import lax
from jax.experimental import pallas as pl
from jax.experimental.pallas import tpu as pltpu

F32 = jnp.float32
BF16 = jnp.bfloat16
HI = lax.Precision.HIGHEST
EPS = 1e-6
LOG2E = 1.4426950408889634

GRID_W = 64
FN_GROUPS = 8
DN_HEADS = 8
DN_DK = 128
DN_DV = 256
SCAN_CHUNK = 128
DFT_N2 = 128
GB_LANES = 128
VMEM_LIMIT_BYTES = 48 * 1024 * 1024


def _cparams(*sem):
    return pltpu.CompilerParams(dimension_semantics=sem, vmem_limit_bytes=VMEM_LIMIT_BYTES)


def _const_spec(shape):
    zeros = (0,) * len(shape)
    return pl.BlockSpec(shape, lambda *_: zeros, pipeline_mode=pl.Buffered(1))


def _silu(v):
    return v * jax.nn.sigmoid(v)


def _mods_kernel(cond_ref, w_ref, b_ref, o_ref):
    s = _silu(cond_ref[...])
    o_ref[0] = jnp.dot(s, w_ref[0], preferred_element_type=F32, precision=HI) + b_ref[0]


def _mods(cond, mod_w, mod_b):
    depth, d, d3 = mod_w.shape
    rows = cond.shape[0]
    tn = 512
    return pl.pallas_call(
        _mods_kernel,
        out_shape=jax.ShapeDtypeStruct((depth, rows, d3), F32),
        grid=(depth, d3 // tn),
        in_specs=[pl.BlockSpec((rows, d), lambda i, j: (0, 0)),
                  pl.BlockSpec((1, d, tn), lambda i, j: (i, 0, j)),
                  pl.BlockSpec((1, 1, tn), lambda i, j: (i, 0, j))],
        out_specs=pl.BlockSpec((1, rows, tn), lambda i, j: (i, 0, j)),
        compiler_params=_cparams("parallel", "parallel"), name="adaln_mods",
    )(cond, mod_w, mod_b.reshape(depth, 1, d3))


def _norm_mod(x, g_ref, mod_ref, d):
    y = x * lax.rsqrt(jnp.mean(x * x, axis=-1, keepdims=True) + EPS) * g_ref[...]
    return y * (1.0 + mod_ref[0, :, d:2 * d]) + mod_ref[0, :, 0:d]


def _nmm_kernel(x_ref, g_ref, mod_ref, w_ref, *outs, splits, d):
    n = _norm_mod(x_ref[0], g_ref, mod_ref, d).astype(BF16)
    c0 = 0
    for o_ref, width in zip(outs, splits):
        step = min(width, 1024)
        for s in range(0, width, step):
            r = jnp.dot(n, w_ref[:, c0 + s:c0 + s + step], preferred_element_type=F32)
            o_ref[0, :, s:s + step] = r.astype(o_ref.dtype)
        c0 += width


def _mod_map(mod):
    return (lambda i, j: (i, 0, 0)) if mod.shape[0] > 1 else (lambda i, j: (0, 0, 0))


def _nmm(x, g, mod, w, splits, dtypes, tm=512):
    b, l, d = x.shape
    tm = min(tm, l)
    return pl.pallas_call(
        functools.partial(_nmm_kernel, splits=splits, d=d),
        out_shape=[jax.ShapeDtypeStruct((b, l, wd), dt) for wd, dt in zip(splits, dtypes)],
        grid=(b, l // tm),
        in_specs=[pl.BlockSpec((1, tm, d), lambda i, j: (i, j, 0)),
                  _const_spec((1, d)),
                  pl.BlockSpec((1, 1, 3 * d), _mod_map(mod)),
                  _const_spec(w.shape)],
        out_specs=[pl.BlockSpec((1, tm, wd), lambda i, j: (i, j, 0)) for wd in splits],
        compiler_params=_cparams("parallel", "parallel"), name="norm_proj",
    )(x, g.reshape(1, d), mod, w)


def _dft_cos_sin(n):
    k = np.arange(n)
    ang = 2.0 * np.pi * ((k[:, None] * k[None, :]) % n) / n
    return np.cos(ang), np.sin(ang)


def _dft1_kernel(m_ref, u_ref, cos_ref, sin_ref, o_ref, *, n1, tn2):
    ut = jnp.transpose(u_ref[0], (1, 0, 2))
    for j in range(tn2):
        r = jnp.dot(m_ref[...], ut[j], preferred_element_type=F32)
        yr = r[:n1]
        yi = r[n1:]
        cs = cos_ref[j]
        sn = sin_ref[j]
        o_ref[0, 0, j] = (yr * cs + yi * sn).astype(o_ref.dtype)
        o_ref[0, 1, j] = (yi * cs - yr * sn).astype(o_ref.dtype)


def _dft1(u, n1, n2, tn2=16, tc=1024):
    b, l, c = u.shape
    cm, sm = _dft_cos_sin(n1)
    m1 = jnp.asarray(np.concatenate([cm, -sm], axis=0), F32).astype(BF16)
    ang = 2.0 * np.pi * (np.arange(n2)[:, None] * np.arange(n1)[None, :]) / l
    tw_cos = jnp.asarray(np.cos(ang)[:, :, None], F32)
    tw_sin = jnp.asarray(np.sin(ang)[:, :, None], F32)
    return pl.pallas_call(
        functools.partial(_dft1_kernel, n1=n1, tn2=tn2),
        out_shape=jax.ShapeDtypeStruct((b, 2, n2, n1, c), BF16),
        grid=(b, n2 // tn2, c // tc),
        in_specs=[_const_spec((2 * n1, n1)),
                  pl.BlockSpec((1, n1, tn2, tc), lambda i, j, k: (i, 0, j, k)),
                  pl.BlockSpec((tn2, n1, 1), lambda i, j, k: (j, 0, 0)),
                  pl.BlockSpec((tn2, n1, 1), lambda i, j, k: (j, 0, 0))],
        out_specs=pl.BlockSpec((1, 2, tn2, n1, tc), lambda i, j, k: (i, 0, j, 0, k)),
        compiler_params=_cparams("parallel", "parallel", "parallel"), name="dft_stage1",
    )(m1, u.reshape(b, n1, n2, c), tw_cos, tw_sin)


def _dft2_lat_kernel(m_ref, y_ref, xr_ref, xi_ref, *, n2, tk1):
    yr = jnp.transpose(y_ref[0, 0], (1, 0, 2))
    yi = jnp.transpose(y_ref[0, 1], (1, 0, 2))
    xr, xi = [], []
    for j in range(tk1):
        r = jnp.dot(m_ref[...], jnp.concatenate([yr[j], yi[j]], axis=0), preferred_element_type=F32)
        xr.append(r[:n2].astype(xr_ref.dtype))
        xi.append(r[n2:].astype(xi_ref.dtype))
    xr_ref[0] = jnp.transpose(jnp.stack(xr), (1, 0, 2))
    xi_ref[0] = jnp.transpose(jnp.stack(xi), (1, 0, 2))


def _dft2_lat(m, y, tk1=16, tc=256):
    b, _, n2, n1, c = y.shape
    tk1 = min(tk1, n1)
    return pl.pallas_call(
        functools.partial(_dft2_lat_kernel, n2=n2, tk1=tk1),
        out_shape=[jax.ShapeDtypeStruct((b, n2, n1, c), BF16)] * 2,
        grid=(b, n1 // tk1, c // tc),
        in_specs=[_const_spec(m.shape),
                  pl.BlockSpec((1, 2, n2, tk1, tc), lambda i, j, k: (i, 0, 0, j, k))],
        out_specs=[pl.BlockSpec((1, n2, tk1, tc), lambda i, j, k: (i, 0, j, k))] * 2,
        compiler_params=_cparams("parallel", "parallel", "parallel"), name="dft_stage2_lat",
    )(m, y)


def _dft2_kernel(m_ref, y_ref, xr_ref, xi_ref, *, half):
    r = jnp.dot(m_ref[...], y_ref[0], preferred_element_type=F32)
    xr_ref[0] = r[:half].astype(xr_ref.dtype)
    xi_ref[0] = r[half:].astype(xi_ref.dtype)


def _dft2(m, y, tc=2048):
    b, k, cols = y.shape
    half = m.shape[0] // 2
    return pl.pallas_call(
        functools.partial(_dft2_kernel, half=half),
        out_shape=[jax.ShapeDtypeStruct((b, half, cols), BF16)] * 2,
        grid=(b, cols // tc),
        in_specs=[_const_spec(m.shape),
                  pl.BlockSpec((1, k, tc), lambda i, j: (i, 0, j))],
        out_specs=[pl.BlockSpec((1, half, tc), lambda i, j: (i, 0, j))] * 2,
        compiler_params=_cparams("parallel", "parallel"), name="dft_stage2",
    )(m, y)


def _dft_real_matrix(n):
    cm, sm = _dft_cos_sin(n)
    return jnp.asarray(np.concatenate([cm, -sm], axis=0), F32).astype(BF16)


def _dft_complex_matrix(n):
    cm, sm = _dft_cos_sin(n)
    return jnp.asarray(np.block([[cm, sm], [-sm, cm]]), F32).astype(BF16)


def _grp_kernel(m_ref, w_ref, gc_ref, gs_ref, *, n):
    r = jnp.dot(m_ref[...], w_ref[0], preferred_element_type=F32, precision=HI)
    gc_ref[0] = r[:n].astype(gc_ref.dtype)
    gs_ref[0] = r[n:].astype(gs_ref.dtype)


def _grp(w_grp):
    groups, n, _ = w_grp.shape
    cm, sm = _dft_cos_sin(n)
    m = jnp.asarray(np.concatenate([cm, sm], axis=0), F32)
    return pl.pallas_call(
        functools.partial(_grp_kernel, n=n),
        out_shape=[jax.ShapeDtypeStruct((groups, n, n), BF16)] * 2,
        grid=(groups,),
        in_specs=[_const_spec((2 * n, n)),
                  pl.BlockSpec((1, n, n), lambda g: (g, 0, 0))],
        out_specs=[pl.BlockSpec((1, n, n), lambda g: (g, 0, 0))] * 2,
        compiler_params=_cparams("parallel"), name="group_weights",
    )(m, w_grp)


def _fn_out_kernel(xr_ref, xi_ref, z_ref, res_ref, mod_ref, gc_ref, gs_ref, w_ref, o_ref, *, groups, gd, d, scale):
    acc = None
    for g in range(groups):
        sl = slice(g * gd, (g + 1) * gd)
        f = (jnp.dot(xr_ref[0, :, sl], gc_ref[g], preferred_element_type=F32)
             + jnp.dot(xi_ref[0, :, sl], gs_ref[g], preferred_element_type=F32))
        y = (f * scale * _silu(z_ref[0, :, sl].astype(F32))).astype(BF16)
        p = jnp.dot(y, w_ref[sl, :], preferred_element_type=F32)
        acc = p if acc is None else acc + p
    gate = mod_ref[0, :, 2 * d:3 * d]
    o_ref[0] = res_ref[0] + gate * acc


def _fn_out(xr, xi, z, res, mod, gc, gs, w_out, scale, tm=512):
    b, l, c = xr.shape
    d = res.shape[-1]
    tm = min(tm, l)
    groups, gd, _ = gc.shape
    tok = lambda wd: pl.BlockSpec((1, tm, wd), lambda i, j: (i, j, 0))
    return pl.pallas_call(
        functools.partial(_fn_out_kernel, groups=groups, gd=gd, d=d, scale=scale),
        out_shape=jax.ShapeDtypeStruct((b, l, d), F32),
        grid=(b, l // tm),
        in_specs=[tok(c), tok(c), tok(c), tok(d),
                  pl.BlockSpec((1, 1, 3 * d), _mod_map(mod)),
                  _const_spec(gc.shape), _const_spec(gs.shape), _const_spec(w_out.shape)],
        out_specs=tok(d),
        compiler_params=_cparams("parallel", "parallel"), name="fourier_out",
    )(xr, xi, z, res, mod, gc, gs, w_out)


def _proj_conv_kernel(xp_ref, xm_ref, xn_ref, g_ref, mod_ref, w_ref, abp_ref, cw_ref, ml_ref, mr_ref,
                      q_ref, k_ref, v_ref, z_ref, gb_ref, *, d, cols, tr, nheads, dk, dv, cb):
    i = pl.program_id(1)
    last = pl.num_programs(1) - 1
    n_prev = (_norm_mod(xp_ref[0], g_ref, mod_ref, d) * (i > 0).astype(F32)).astype(BF16)
    n_main = _norm_mod(xm_ref[0], g_ref, mod_ref, d).astype(BF16)
    n_next = (_norm_mod(xn_ref[0], g_ref, mod_ref, d) * (i < last).astype(F32)).astype(BF16)
    ne = jnp.concatenate([n_prev, n_main, n_next], axis=0)
    n = tr * cols
    qk_w = nheads * dk
    v_w = nheads * dv
    for c0 in range(0, 2 * qk_w + v_w, cb):
        e = jnp.dot(ne, w_ref[:, c0:c0 + cb], preferred_element_type=F32)
        part = []
        for dc in range(3):
            acc = None
            for dr in range(3):
                term = e[dr * cols:dr * cols + n] * cw_ref[dr, dc:dc + 1, c0:c0 + cb]
                acc = term if acc is None else acc + term
            part.append(acc)
        y = _silu(pltpu.roll(part[0], 1, axis=0) * ml_ref[...] + part[1]
                  + pltpu.roll(part[2], n - 1, axis=0) * mr_ref[...])
        if c0 < 2 * qk_w:
            o_ref, scale, base = (q_ref, dk ** -0.5, c0) if c0 < qk_w else (k_ref, 1.0, c0 - qk_w)
            for j in range(cb // dk):
                yh = y[:, j * dk:(j + 1) * dk]
                yh = yh * lax.rsqrt(jnp.sum(yh * yh, axis=-1, keepdims=True) + EPS) * scale
                o_ref[0, base // dk + j] = yh.astype(o_ref.dtype)
        else:
            base = c0 - 2 * qk_w
            for j in range(cb // dv):
                v_ref[0, base // dv + j] = y[:, j * dv:(j + 1) * dv].astype(v_ref.dtype)
    z0 = 2 * qk_w + v_w
    zstep = min(v_w, 1024)
    for s in range(0, v_w, zstep):
        r = jnp.dot(n_main, w_ref[:, z0 + s:z0 + s + zstep], preferred_element_type=F32)
        z_ref[0, :, s:s + zstep] = r.astype(z_ref.dtype)
    r = jnp.dot(n_main, w_ref[:, z0 + v_w:z0 + v_w + GB_LANES], preferred_element_type=F32)
    t = r + abp_ref[1:2, :]
    softplus = jnp.maximum(t, 0.0) + jnp.log(1.0 + jnp.exp(-jnp.abs(t)))
    lane = lax.broadcasted_iota(jnp.int32, r.shape, 1)
    gb_ref[0] = jnp.where(lane < 2 * nheads, (-LOG2E) * jnp.exp(abp_ref[0:1, :]) * softplus, jax.nn.sigmoid(r))


def _proj_conv(x, g, mod, w, abp, conv_w, rows, cols, cb=512):
    b, l, d = x.shape
    tr = min(rows, max(1, 512 // cols))
    n = tr * cols
    col = np.arange(n) % cols
    ml = jnp.asarray(np.broadcast_to((col != 0)[:, None], (n, cb)), F32)
    mr = jnp.asarray(np.broadcast_to((col != cols - 1)[:, None], (n, cb)), F32)
    heads = lambda wd: pl.BlockSpec((1, DN_HEADS, n, wd), lambda bi, i: (bi, 0, i, 0))
    tok = lambda wd: pl.BlockSpec((1, n, wd), lambda bi, i: (bi, i, 0))
    return pl.pallas_call(
        functools.partial(_proj_conv_kernel, d=d, cols=cols, tr=tr, nheads=DN_HEADS, dk=DN_DK, dv=DN_DV, cb=cb),
        out_shape=[jax.ShapeDtypeStruct((b, DN_HEADS, l, DN_DK), BF16),
                   jax.ShapeDtypeStruct((b, DN_HEADS, l, DN_DK), BF16),
                   jax.ShapeDtypeStruct((b, DN_HEADS, l, DN_DV), BF16),
                   jax.ShapeDtypeStruct((b, l, DN_HEADS * DN_DV), BF16),
                   jax.ShapeDtypeStruct((b, l, GB_LANES), F32)],
        grid=(b, rows // tr),
        in_specs=[pl.BlockSpec((1, cols, d), lambda bi, i: (bi, jnp.maximum(i * tr - 1, 0), 0)),
                  pl.BlockSpec((1, n, d), lambda bi, i: (bi, i, 0)),
                  pl.BlockSpec((1, cols, d), lambda bi, i: (bi, jnp.minimum((i + 1) * tr, rows - 1), 0)),
                  _const_spec((1, d)),
                  pl.BlockSpec((1, 1, 3 * d), _mod_map(mod)),
                  _const_spec(w.shape), _const_spec(abp.shape), _const_spec(conv_w.shape),
                  _const_spec((n, cb)), _const_spec((n, cb))],
        out_specs=[heads(DN_DK), heads(DN_DK), heads(DN_DV), tok(DN_HEADS * DN_DV), tok(GB_LANES)],
        compiler_params=_cparams("parallel", "parallel"), name="proj_conv",
    )(x, x, x, g.reshape(1, d), mod, w, abp, conv_w, ml, mr)


def _prep_kernel(q_ref, k_ref, v_ref, g_ref, wq_ref, a2_ref, u_ref, eg_ref, *, nheads, cs):
    ri = lax.broadcasted_iota(jnp.int32, (cs, cs), 0)
    ci = lax.broadcasted_iota(jnp.int32, (cs, cs), 1)
    eye = ri == ci
    xor = ri ^ ci
    nt = (((1,), (1,)), ((), ()))
    levels = int(math.log2(cs))
    gbv = g_ref[0]
    hi = gbv.astype(BF16)
    rem = gbv - hi.astype(F32)
    mid = rem.astype(BF16)
    parts = (hi, mid, (rem - mid.astype(F32)).astype(BF16))
    cums, cums_t = [], []
    for tri in (ri >= ci, ri <= ci):
        t01 = jnp.where(tri, 1.0, 0.0).astype(BF16)
        cum = sum(jnp.dot(t01, p, preferred_element_type=F32) for p in parts)
        cums.append(cum)
        cums_t.append(cum.T)
    chains = []
    for hd in range(nheads):
        qb = q_ref[0, hd]
        kb = k_ref[0, hd]
        k = kb.astype(F32)
        v = v_ref[0, hd].astype(F32)
        kk = lax.dot_general(kb, kb, nt, preferred_element_type=F32)
        qk = lax.dot_general(qb, kb, nt, preferred_element_type=F32)
        k_t = k.T
        for d in range(2):
            incl = (ri >= ci) if d == 0 else (ri <= ci)
            strict = (ri > ci) if d == 0 else (ri < ci)
            lg = d * nheads + hd
            lb = (2 + d) * nheads + hd
            gc_col = cums[d][:, lg:lg + 1]
            gc_row = cums_t[d][lg:lg + 1, :]
            g_tot = cums[0][cs - 1:cs, lg:lg + 1]
            b_col = gbv[:, lb:lb + 1]
            ex = jnp.exp2(gc_col - gc_row)
            egc = jnp.exp2(gc_col)
            wq_ref[0, hd, d, 0, cs:, :] = (qb.astype(F32) * egc).astype(wq_ref.dtype)
            a2_ref[0, hd, d, 0, :cs, :] = jnp.where(incl, qk * ex, 0.0).astype(a2_ref.dtype)
            a2_ref[0, hd, d, 0, cs:, :] = (k_t * jnp.exp2(g_tot - gc_row)).astype(a2_ref.dtype)
            eg_ref[0, hd, d, 0] = jnp.broadcast_to(jnp.exp2(g_tot), eg_ref.shape[-2:])
            rhs = jnp.concatenate([v * b_col, k * (b_col * egc)], axis=1).astype(BF16)
            neg_off = jnp.where(strict, kk * (-b_col) * ex, 0.0)
            chains.append((hd, d, neg_off, rhs))
    dmats = [jnp.where(eye, 1.0, jnp.where(xor == 1, no, 0.0)) for _, _, no, _ in chains]
    neg_offs = [no.astype(BF16) for _, _, no, _ in chains]
    for lv in range(1, levels):
        sel = jnp.where((xor >> lv) == 1, 1.0, 0.0).astype(BF16)
        dmbs = [dm.astype(BF16) for dm in dmats]
        des = [jnp.dot(db, no * sel, preferred_element_type=F32) for db, no in zip(dmbs, neg_offs)]
        dmats = [dm + jnp.dot(de.astype(BF16), db, preferred_element_type=F32)
                 for dm, de, db in zip(dmats, des, dmbs)]
    sols = [jnp.dot(dm.astype(BF16), ch[3], preferred_element_type=F32) for dm, ch in zip(dmats, chains)]
    dv = u_ref.shape[-1]
    for (hd, d, _, _), sol in zip(chains, sols):
        u_ref[0, hd, d, 0] = sol[:, :dv].astype(u_ref.dtype)
        wq_ref[0, hd, d, 0, :cs, :] = sol[:, dv:].astype(wq_ref.dtype)


def _prep(q, k, v, gb, cs=SCAN_CHUNK):
    b, nh, l, dk = q.shape
    dv = v.shape[-1]
    nc = l // cs
    tok = lambda wd: pl.BlockSpec((1, nh, cs, wd), lambda bi, n: (bi, 0, n, 0))
    out = lambda r, wd: pl.BlockSpec((1, nh, 2, 1, r, wd), lambda bi, n: (bi, 0, 0, n, 0, 0))
    return pl.pallas_call(
        functools.partial(_prep_kernel, nheads=nh, cs=cs),
        out_shape=[jax.ShapeDtypeStruct((b, nh, 2, nc, 2 * cs, dk), BF16),
                   jax.ShapeDtypeStruct((b, nh, 2, nc, cs + dk, cs), BF16),
                   jax.ShapeDtypeStruct((b, nh, 2, nc, cs, dv), BF16),
                   jax.ShapeDtypeStruct((b, nh, 2, nc, 8, 128), F32)],
        grid=(b, nc),
        in_specs=[tok(dk), tok(dk), tok(dv), pl.BlockSpec((1, cs, GB_LANES), lambda bi, n: (bi, n, 0))],
        out_specs=[out(2 * cs, dk), out(cs + dk, cs), out(cs, dv), out(8, 128)],
        compiler_params=_cparams("parallel", "parallel"), name="delta_prep",
    )(q, k, v, gb)


def _scan_kernel(wqf_ref, a2f_ref, uf_ref, egf_ref, wqb_ref, a2b_ref, ub_ref, egb_ref, s0_ref,
                 of_ref, ob_ref, sfin_ref, s_scr, sb_scr, *, hb, cs):
    n = pl.program_id(2)

    @pl.when(n == 0)
    def _():
        s_scr[...] = s0_ref[0]
        sb_scr[...] = s0_ref[0].astype(sb_scr.dtype)

    dirs = ((wqf_ref, a2f_ref, uf_ref, egf_ref, of_ref), (wqb_ref, a2b_ref, ub_ref, egb_ref, ob_ref))
    chains = [(j, d) + refs for j in range(hb) for d, refs in enumerate(dirs)]
    r1s = [jnp.dot(wq_ref[0, j, 0, 0], sb_scr[j, d], preferred_element_type=F32)
           for j, d, wq_ref, _, _, _, _ in chains]
    vns = [(u_ref[0, j, 0, 0].astype(F32) - r1[:cs]).astype(BF16)
           for (j, _, _, _, u_ref, _, _), r1 in zip(chains, r1s)]
    r2s = [jnp.dot(a2_ref[0, j, 0, 0], vn, preferred_element_type=F32)
           for (j, _, _, a2_ref, _, _, _), vn in zip(chains, vns)]
    for (j, d, _, _, _, eg_ref, o_ref), r1, r2 in zip(chains, r1s, r2s):
        o_ref[0, j] = (r1[cs:] + r2[:cs]).astype(o_ref.dtype)
        s_new = s_scr[j, d] * eg_ref[0, j, 0, 0][0:1, 0:1] + r2[cs:]
        s_scr[j, d] = s_new
        sb_scr[j, d] = s_new.astype(sb_scr.dtype)

    @pl.when(n == pl.num_programs(2) - 1)
    def _():
        sfin_ref[0] = s_scr[...]


def _scan(wq, a2, u, eg, s0, hb=8):
    b, nh, _, nc, cs, dv = u.shape
    dk = wq.shape[-1]
    spec = lambda r, wd, d, rev: pl.BlockSpec(
        (1, hb, 1, 1, r, wd), lambda bi, hi, n: (bi, hi, d, (nc - 1 - n) if rev else n, 0, 0))
    both = lambda r, wd: [spec(r, wd, 0, False), spec(r, wd, 1, True)]
    ins = [both(2 * cs, dk), both(cs + dk, cs), both(cs, dv), both(8, 128)]
    st = pl.BlockSpec((1, hb, 2, dk, dv), lambda bi, hi, n: (bi, hi, 0, 0, 0))
    return pl.pallas_call(
        functools.partial(_scan_kernel, hb=hb, cs=cs),
        out_shape=[jax.ShapeDtypeStruct((b, nh, nc * cs, dv), BF16),
                   jax.ShapeDtypeStruct((b, nh, nc * cs, dv), BF16),
                   jax.ShapeDtypeStruct((b, nh, 2, dk, dv), F32)],
        grid=(b, nh // hb, nc),
        in_specs=[s[0] for s in ins] + [s[1] for s in ins] + [st],
        out_specs=[pl.BlockSpec((1, hb, cs, dv), lambda bi, hi, n: (bi, hi, n, 0)),
                   pl.BlockSpec((1, hb, cs, dv), lambda bi, hi, n: (bi, hi, nc - 1 - n, 0)),
                   st],
        scratch_shapes=[pltpu.VMEM((hb, 2, dk, dv), F32), pltpu.VMEM((hb, 2, dk, dv), BF16)],
        compiler_params=_cparams("parallel", "parallel", "arbitrary"), name="delta_scan",
    )(wq, a2, u, eg, wq, a2, u, eg, s0)


def _dn_out_kernel(of_ref, ob_ref, z_ref, res_ref, mod_ref, ng_ref, w_ref, fg_ref, o_ref, *, nheads, dv, d):
    acc = None
    for hd in range(nheads):
        sl = slice(hd * dv, (hd + 1) * dv)
        o = of_ref[0, hd].astype(F32) + ob_ref[0, hd].astype(F32)
        o = o * lax.rsqrt(jnp.mean(o * o, axis=-1, keepdims=True) + EPS) * ng_ref[...]
        y = (o * _silu(z_ref[0, :, sl].astype(F32))).astype(BF16)
        p = jnp.dot(y, w_ref[sl, :], preferred_element_type=F32)
        acc = p if acc is None else acc + p
    gate = mod_ref[0, :, 2 * d:3 * d]
    hn = res_ref[0] + gate * acc
    o_ref[0] = hn * lax.rsqrt(jnp.mean(hn * hn, axis=-1, keepdims=True) + EPS) * fg_ref[...]


def _dn_out(o_f, o_b, z, res, mod, norm_g, w_out, final_g, tm=512):
    b, nh, l, dv = o_f.shape
    d = res.shape[-1]
    tm = min(tm, l)
    tok = lambda wd: pl.BlockSpec((1, tm, wd), lambda i, j: (i, j, 0))
    hs = pl.BlockSpec((1, nh, tm, dv), lambda i, j: (i, 0, j, 0))
    return pl.pallas_call(
        functools.partial(_dn_out_kernel, nheads=nh, dv=dv, d=d),
        out_shape=jax.ShapeDtypeStruct((b, l, d), F32),
        grid=(b, l // tm),
        in_specs=[hs, hs, tok(nh * dv), tok(d),
                  pl.BlockSpec((1, 1, 3 * d), lambda i, j: (i, 0, 0)),
                  _const_spec((1, dv)), _const_spec(w_out.shape), _const_spec((1, d))],
        out_specs=tok(d),
        compiler_params=_cparams("parallel", "parallel"), name="delta_out",
    )(o_f, o_b, z, res, mod, norm_g.reshape(1, dv), w_out, final_g.reshape(1, d))


def _fourier_layer(h_lat, h_ctx, norm_g, mod_lat, mod_ctx, w_in, w_grp, w_out):
    b, l, _ = h_lat.shape
    lc = h_ctx.shape[1]
    width = w_in.shape[1] // 2
    gd = w_grp.shape[-1]
    w_in = w_in.astype(BF16)
    w_out = w_out.astype(BF16)
    gc, gs = _grp(w_grp)

    u, z = _nmm(h_lat, norm_g, mod_lat, w_in, (width, width), (BF16, BF16))
    n2 = DFT_N2
    n1 = l // n2
    y = _dft1(u, n1, n2)
    xr, xi = _dft2_lat(_dft_complex_matrix(n2), y)
    xr = xr.reshape(b, l, width)
    xi = xi.reshape(b, l, width)
    out_lat = _fn_out(xr, xi, z, h_lat, mod_lat, gc, gs, w_out, 1.0 / math.sqrt(l * gd))

    uc, zc = _nmm(h_ctx, norm_g, mod_ctx, w_in, (width, width), (BF16, BF16))
    xrc, xic = _dft2(_dft_real_matrix(lc), uc)
    out_ctx = _fn_out(xrc, xic, zc, h_ctx, mod_ctx, gc, gs, w_out, 1.0 / math.sqrt(lc * gd))
    return out_lat, out_ctx


def _delta_layer_final(h_lat, h_ctx, norm_g, mod_lat, mod_ctx, w_in, conv_w, a_log, dt_bias, dn_norm_g, w_out,
                       final_g):
    b, l, d = h_lat.shape
    lc = h_ctx.shape[1]
    main_w = 2 * DN_HEADS * DN_DK + 2 * DN_HEADS * DN_DV
    n_ab = 4 * DN_HEADS
    w_cat = jnp.concatenate(
        [w_in[:, :main_w], w_in[:, main_w:], jnp.zeros((d, GB_LANES - n_ab), w_in.dtype)], axis=1).astype(BF16)
    pad = jnp.zeros((GB_LANES - 2 * DN_HEADS,), F32)
    abp = jnp.stack([jnp.concatenate([a_log.reshape(-1).astype(F32), pad]),
                     jnp.concatenate([dt_bias.reshape(-1).astype(F32), pad])])
    qc, kc, vc, _, gbc = _proj_conv(h_ctx, norm_g, mod_ctx, w_cat, abp, conv_w, 1, lc)
    ql, kl, vl, zl, gbl = _proj_conv(h_lat, norm_g, mod_lat, w_cat, abp, conv_w, l // GRID_W, GRID_W)
    s0 = jnp.zeros((b, DN_HEADS, 2, DN_DK, DN_DV), F32)
    _, _, s_ctx = _scan(*_prep(qc, kc, vc, gbc), s0)
    o_f, o_b, _ = _scan(*_prep(ql, kl, vl, gbl), s_ctx)
    return _dn_out(o_f, o_b, zl, h_lat, mod_lat, dn_norm_g, w_out.astype(BF16), final_g)


def kernel(x, c, ctx, c_ctx, mod_w, mod_b, norm_g, final_g, fn_w_in, fn_w_grp, fn_w_out,
           dn_w_in, dn_conv, dn_a_log, dn_dt_bias, dn_norm_g, dn_w_out):
    b, l, d = x.shape
    cond = jnp.concatenate([c, c_ctx[None, :], jnp.zeros((8 - b - 1, d), c.dtype)], axis=0)
    mods = _mods(cond, mod_w, mod_b)
    mod_lat = [mods[i, :b].reshape(b, 1, 3 * d) for i in range(2)]
    mod_ctx = [mods[i, b:b + 1].reshape(1, 1, 3 * d) for i in range(2)]
    h_lat, h_ctx = _fourier_layer(x, ctx, norm_g[0], mod_lat[0], mod_ctx[0], fn_w_in[0], fn_w_grp[0], fn_w_out[0])
    return _delta_layer_final(h_lat, h_ctx, norm_g[1], mod_lat[1], mod_ctx[1], dn_w_in[0], dn_conv[0],
                              dn_a_log[0], dn_dt_bias[0], dn_norm_g[0], dn_w_out[0], final_g)
```

```python
import functools
import math

import numpy as np
import jax
import jax.numpy as jnp
from jax import lax
from jax.experimental import pallas as pl
from jax.experimental.pallas import tpu as pltpu

F32 = jnp.float32
BF16 = jnp.bfloat16
HI = lax.Precision.HIGHEST
EPS = 1e-6
LOG2E = 1.4426950408889634

GRID_W = 64
FN_GROUPS = 8
DN_HEADS = 8
DN_DK = 128
DN_DV = 256
SCAN_CHUNK = 128
DFT_N2 = 128
GB_LANES = 128
VMEM_LIMIT_BYTES = 48 * 1024 * 1024


def _cparams(*sem):
    return pltpu.CompilerParams(dimension_semantics=sem, vmem_limit_bytes=VMEM_LIMIT_BYTES)


def _const_spec(shape):
    zeros = (0,) * len(shape)
    return pl.BlockSpec(shape, lambda *_: zeros, pipeline_mode=pl.Buffered(1))


def _silu(v):
    return v * jax.nn.sigmoid(v)


def _mods_kernel(cond_ref, w_ref, b_ref, o_ref):
    s = _silu(cond_ref[...])
    o_ref[0] = jnp.dot(s, w_ref[0], preferred_element_type=F32, precision=HI) + b_ref[0]


def _mods(cond, mod_w, mod_b):
    depth, d, d3 = mod_w.shape
    rows = cond.shape[0]
    tn = 512
    return pl.pallas_call(
        _mods_kernel,
        out_shape=jax.ShapeDtypeStruct((depth, rows, d3), F32),
        grid=(depth, d3 // tn),
        in_specs=[pl.BlockSpec((rows, d), lambda i, j: (0, 0)),
                  pl.BlockSpec((1, d, tn), lambda i, j: (i, 0, j)),
                  pl.BlockSpec((1, 1, tn), lambda i, j: (i, 0, j))],
        out_specs=pl.BlockSpec((1, rows, tn), lambda i, j: (i, 0, j)),
        compiler_params=_cparams("parallel", "parallel"), name="adaln_mods",
    )(cond, mod_w, mod_b.reshape(depth, 1, d3))


def _norm_mod(x, g_ref, mod_ref, d):
    y = x * lax.rsqrt(jnp.mean(x * x, axis=-1, keepdims=True) + EPS) * g_ref[...]
    return y * (1.0 + mod_ref[0, :, d:2 * d]) + mod_ref[0, :, 0:d]


def _nmm_kernel(x_ref, g_ref, mod_ref, w_ref, *outs, splits, d):
    n = _norm_mod(x_ref[0], g_ref, mod_ref, d).astype(BF16)
    c0 = 0
    for o_ref, width in zip(outs, splits):
        step = min(width, 1024)
        for s in range(0, width, step):
            r = jnp.dot(n, w_ref[:, c0 + s:c0 + s + step], preferred_element_type=F32)
            o_ref[0, :, s:s + step] = r.astype(o_ref.dtype)
        c0 += width


def _mod_map(mod):
    return (lambda i, j: (i, 0, 0)) if mod.shape[0] > 1 else (lambda i, j: (0, 0, 0))


def _nmm(x, g, mod, w, splits, dtypes, tm=512):
    b, l, d = x.shape
    tm = min(tm, l)
    return pl.pallas_call(
        functools.partial(_nmm_kernel, splits=splits, d=d),
        out_shape=[jax.ShapeDtypeStruct((b, l, wd), dt) for wd, dt in zip(splits, dtypes)],
        grid=(b, l // tm),
        in_specs=[pl.BlockSpec((1, tm, d), lambda i, j: (i, j, 0)),
                  _const_spec((1, d)),
                  pl.BlockSpec((1, 1, 3 * d), _mod_map(mod)),
                  _const_spec(w.shape)],
        out_specs=[pl.BlockSpec((1, tm, wd), lambda i, j: (i, j, 0)) for wd in splits],
        compiler_params=_cparams("parallel", "parallel"), name="norm_proj",
    )(x, g.reshape(1, d), mod, w)


def _dft_cos_sin(n):
    k = np.arange(n)
    ang = 2.0 * np.pi * ((k[:, None] * k[None, :]) % n) / n
    return np.cos(ang), np.sin(ang)


def _dft1_kernel(m_ref, u_ref, cos_ref, sin_ref, o_ref, *, n1, tn2):
    ut = jnp.transpose(u_ref[0], (1, 0, 2))
    for j in range(tn2):
        r = jnp.dot(m_ref[...], ut[j], preferred_element_type=F32)
        yr = r[:n1]
        yi = r[n1:]
        cs = cos_ref[j]
        sn = sin_ref[j]
        o_ref[0, 0, j] = (yr * cs + yi * sn).astype(o_ref.dtype)
        o_ref[0, 1, j] = (yi * cs - yr * sn).astype(o_ref.dtype)


def _dft1(u, n1, n2, tn2=16, tc=1024):
    b, l, c = u.shape
    cm, sm = _dft_cos_sin(n1)
    m1 = jnp.asarray(np.concatenate([cm, -sm], axis=0), F32).astype(BF16)
    ang = 2.0 * np.pi * (np.arange(n2)[:, None] * np.arange(n1)[None, :]) / l
    tw_cos = jnp.asarray(np.cos(ang)[:, :, None], F32)
    tw_sin = jnp.asarray(np.sin(ang)[:, :, None], F32)
    return pl.pallas_call(
        functools.partial(_dft1_kernel, n1=n1, tn2=tn2),
        out_shape=jax.ShapeDtypeStruct((b, 2, n2, n1, c), BF16),
        grid=(b, n2 // tn2, c // tc),
        in_specs=[_const_spec((2 * n1, n1)),
                  pl.BlockSpec((1, n1, tn2, tc), lambda i, j, k: (i, 0, j, k)),
                  pl.BlockSpec((tn2, n1, 1), lambda i, j, k: (j, 0, 0)),
                  pl.BlockSpec((tn2, n1, 1), lambda i, j, k: (j, 0, 0))],
        out_specs=pl.BlockSpec((1, 2, tn2, n1, tc), lambda i, j, k: (i, 0, j, 0, k)),
        compiler_params=_cparams("parallel", "parallel", "parallel"), name="dft_stage1",
    )(m1, u.reshape(b, n1, n2, c), tw_cos, tw_sin)


def _dft2_lat_kernel(m_ref, y_ref, xr_ref, xi_ref, *, n2, tk1):
    yr = jnp.transpose(y_ref[0, 0], (1, 0, 2))
    yi = jnp.transpose(y_ref[0, 1], (1, 0, 2))
    xr, xi = [], []
    for j in range(tk1):
        r = jnp.dot(m_ref[...], jnp.concatenate([yr[j], yi[j]], axis=0), preferred_element_type=F32)
        xr.append(r[:n2].astype(xr_ref.dtype))
        xi.append(r[n2:].astype(xi_ref.dtype))
    xr_ref[0] = jnp.transpose(jnp.stack(xr), (1, 0, 2))
    xi_ref[0] = jnp.transpose(jnp.stack(xi), (1, 0, 2))


def _dft2_lat(m, y, tk1=16, tc=256):
    b, _, n2, n1, c = y.shape
    tk1 = min(tk1, n1)
    return pl.pallas_call(
        functools.partial(_dft2_lat_kernel, n2=n2, tk1=tk1),
        out_shape=[jax.ShapeDtypeStruct((b, n2, n1, c), BF16)] * 2,
        grid=(b, n1 // tk1, c // tc),
        in_specs=[_const_spec(m.shape),
                  pl.BlockSpec((1, 2, n2, tk1, tc), lambda i, j, k: (i, 0, 0, j, k))],
        out_specs=[pl.BlockSpec((1, n2, tk1, tc), lambda i, j, k: (i, 0, j, k))] * 2,
        compiler_params=_cparams("parallel", "parallel", "parallel"), name="dft_stage2_lat",
    )(m, y)


def _dft2_kernel(m_ref, y_ref, xr_ref, xi_ref, *, half):
    r = jnp.dot(m_ref[...], y_ref[0], preferred_element_type=F32)
    xr_ref[0] = r[:half].astype(xr_ref.dtype)
    xi_ref[0] = r[half:].astype(xi_ref.dtype)


def _dft2(m, y, tc=2048):
    b, k, cols = y.shape
    half = m.shape[0] // 2
    return pl.pallas_call(
        functools.partial(_dft2_kernel, half=half),
        out_shape=[jax.ShapeDtypeStruct((b, half, cols), BF16)] * 2,
        grid=(b, cols // tc),
        in_specs=[_const_spec(m.shape),
                  pl.BlockSpec((1, k, tc), lambda i, j: (i, 0, j))],
        out_specs=[pl.BlockSpec((1, half, tc), lambda i, j: (i, 0, j))] * 2,
        compiler_params=_cparams("parallel", "parallel"), name="dft_stage2",
    )(m, y)


def _dft_real_matrix(n):
    cm, sm = _dft_cos_sin(n)
    return jnp.asarray(np.concatenate([cm, -sm], axis=0), F32).astype(BF16)


def _dft_complex_matrix(n):
    cm, sm = _dft_cos_sin(n)
    return jnp.asarray(np.block([[cm, sm], [-sm, cm]]), F32).astype(BF16)


def _grp_kernel(m_ref, w_ref, gc_ref, gs_ref, *, n):
    r = jnp.dot(m_ref[...], w_ref[0], preferred_element_type=F32, precision=HI)
    gc_ref[0] = r[:n].astype(gc_ref.dtype)
    gs_ref[0] = r[n:].astype(gs_ref.dtype)


def _grp(w_grp):
    groups, n, _ = w_grp.shape
    cm, sm = _dft_cos_sin(n)
    m = jnp.asarray(np.concatenate([cm, sm], axis=0), F32)
    return pl.pallas_call(
        functools.partial(_grp_kernel, n=n),
        out_shape=[jax.ShapeDtypeStruct((groups, n, n), BF16)] * 2,
        grid=(groups,),
        in_specs=[_const_spec((2 * n, n)),
                  pl.BlockSpec((1, n, n), lambda g: (g, 0, 0))],
        out_specs=[pl.BlockSpec((1, n, n), lambda g: (g, 0, 0))] * 2,
        compiler_params=_cparams("parallel"), name="group_weights",
    )(m, w_grp)


def _fn_out_kernel(xr_ref, xi_ref, z_ref, res_ref, mod_ref, gc_ref, gs_ref, w_ref, o_ref, *, groups, gd, d, scale):
    acc = None
    for g in range(groups):
        sl = slice(g * gd, (g + 1) * gd)
        f = (jnp.dot(xr_ref[0, :, sl], gc_ref[g], preferred_element_type=F32)
             + jnp.dot(xi_ref[0, :, sl], gs_ref[g], preferred_element_type=F32))
        y = (f * scale * _silu(z_ref[0, :, sl].astype(F32))).astype(BF16)
        p = jnp.dot(y, w_ref[sl, :], preferred_element_type=F32)
        acc = p if acc is None else acc + p
    gate = mod_ref[0, :, 2 * d:3 * d]
    o_ref[0] = res_ref[0] + gate * acc


def _fn_out(xr, xi, z, res, mod, gc, gs, w_out, scale, tm=512):
    b, l, c = xr.shape
    d = res.shape[-1]
    tm = min(tm, l)
    groups, gd, _ = gc.shape
    tok = lambda wd: pl.BlockSpec((1, tm, wd), lambda i, j: (i, j, 0))
    return pl.pallas_call(
        functools.partial(_fn_out_kernel, groups=groups, gd=gd, d=d, scale=scale),
        out_shape=jax.ShapeDtypeStruct((b, l, d), F32),
        grid=(b, l // tm),
        in_specs=[tok(c), tok(c), tok(c), tok(d),
                  pl.BlockSpec((1, 1, 3 * d), _mod_map(mod)),
                  _const_spec(gc.shape), _const_spec(gs.shape), _const_spec(w_out.shape)],
        out_specs=tok(d),
        compiler_params=_cparams("parallel", "parallel"), name="fourier_out",
    )(xr, xi, z, res, mod, gc, gs, w_out)


def _proj_conv_kernel(xp_ref, xm_ref, xn_ref, g_ref, mod_ref, w_ref, abp_ref, cw_ref, ml_ref, mr_ref,
                      q_ref, k_ref, v_ref, z_ref, gb_ref, *, d, cols, tr, nheads, dk, dv, cb):
    i = pl.program_id(1)
    last = pl.num_programs(1) - 1
    n_prev = (_norm_mod(xp_ref[0], g_ref, mod_ref, d) * (i > 0).astype(F32)).astype(BF16)
    n_main = _norm_mod(xm_ref[0], g_ref, mod_ref, d).astype(BF16)
    n_next = (_norm_mod(xn_ref[0], g_ref, mod_ref, d) * (i < last).astype(F32)).astype(BF16)
    ne = jnp.concatenate([n_prev, n_main, n_next], axis=0)
    n = tr * cols
    qk_w = nheads * dk
    v_w = nheads * dv
    for c0 in range(0, 2 * qk_w + v_w, cb):
        e = jnp.dot(ne, w_ref[:, c0:c0 + cb], preferred_element_type=F32)
        part = []
        for dc in range(3):
            acc = None
            for dr in range(3):
                term = e[dr * cols:dr * cols + n] * cw_ref[dr, dc:dc + 1, c0:c0 + cb]
                acc = term if acc is None else acc + term
            part.append(acc)
        y = _silu(pltpu.roll(part[0], 1, axis=0) * ml_ref[...] + part[1]
                  + pltpu.roll(part[2], n - 1, axis=0) * mr_ref[...])
        if c0 < 2 * qk_w:
            o_ref, scale, base = (q_ref, dk ** -0.5, c0) if c0 < qk_w else (k_ref, 1.0, c0 - qk_w)
            for j in range(cb // dk):
                yh = y[:, j * dk:(j + 1) * dk]
                yh = yh * lax.rsqrt(jnp.sum(yh * yh, axis=-1, keepdims=True) + EPS) * scale
                o_ref[0, base // dk + j] = yh.astype(o_ref.dtype)
        else:
            base = c0 - 2 * qk_w
            for j in range(cb // dv):
                v_ref[0, base // dv + j] = y[:, j * dv:(j + 1) * dv].astype(v_ref.dtype)
    z0 = 2 * qk_w + v_w
    zstep = min(v_w, 1024)
    for s in range(0, v_w, zstep):
        r = jnp.dot(n_main, w_ref[:, z0 + s:z0 + s + zstep], preferred_element_type=F32)
        z_ref[0, :, s:s + zstep] = r.astype(z_ref.dtype)
    r = jnp.dot(n_main, w_ref[:, z0 + v_w:z0 + v_w + GB_LANES], preferred_element_type=F32)
    t = r + abp_ref[1:2, :]
    softplus = jnp.maximum(t, 0.0) + jnp.log(1.0 + jnp.exp(-jnp.abs(t)))
    lane = lax.broadcasted_iota(jnp.int32, r.shape, 1)
    gb_ref[0] = jnp.where(lane < 2 * nheads, (-LOG2E) * jnp.exp(abp_ref[0:1, :]) * softplus, jax.nn.sigmoid(r))


def _proj_conv(x, g, mod, w, abp, conv_w, rows, cols, cb=512):
    b, l, d = x.shape
    tr = min(rows, max(1, 512 // cols))
    n = tr * cols
    col = np.arange(n) % cols
    ml = jnp.asarray(np.broadcast_to((col != 0)[:, None], (n, cb)), F32)
    mr = jnp.asarray(np.broadcast_to((col != cols - 1)[:, None], (n, cb)), F32)
    heads = lambda wd: pl.BlockSpec((1, DN_HEADS, n, wd), lambda bi, i: (bi, 0, i, 0))
    tok = lambda wd: pl.BlockSpec((1, n, wd), lambda bi, i: (bi, i, 0))
    return pl.pallas_call(
        functools.partial(_proj_conv_kernel, d=d, cols=cols, tr=tr, nheads=DN_HEADS, dk=DN_DK, dv=DN_DV, cb=cb),
        out_shape=[jax.ShapeDtypeStruct((b, DN_HEADS, l, DN_DK), BF16),
                   jax.ShapeDtypeStruct((b, DN_HEADS, l, DN_DK), BF16),
                   jax.ShapeDtypeStruct((b, DN_HEADS, l, DN_DV), BF16),
                   jax.ShapeDtypeStruct((b, l, DN_HEADS * DN_DV), BF16),
                   jax.ShapeDtypeStruct((b, l, GB_LANES), F32)],
        grid=(b, rows // tr),
        in_specs=[pl.BlockSpec((1, cols, d), lambda bi, i: (bi, jnp.maximum(i * tr - 1, 0), 0)),
                  pl.BlockSpec((1, n, d), lambda bi, i: (bi, i, 0)),
                  pl.BlockSpec((1, cols, d), lambda bi, i: (bi, jnp.minimum((i + 1) * tr, rows - 1), 0)),
                  _const_spec((1, d)),
                  pl.BlockSpec((1, 1, 3 * d), _mod_map(mod)),
                  _const_spec(w.shape), _const_spec(abp.shape), _const_spec(conv_w.shape),
                  _const_spec((n, cb)), _const_spec((n, cb))],
        out_specs=[heads(DN_DK), heads(DN_DK), heads(DN_DV), tok(DN_HEADS * DN_DV), tok(GB_LANES)],
        compiler_params=_cparams("parallel", "parallel"), name="proj_conv",
    )(x, x, x, g.reshape(1, d), mod, w, abp, conv_w, ml, mr)


def _prep_kernel(q_ref, k_ref, v_ref, g_ref, wq_ref, a2_ref, u_ref, eg_ref, *, nheads, cs):
    ri = lax.broadcasted_iota(jnp.int32, (cs, cs), 0)
    ci = lax.broadcasted_iota(jnp.int32, (cs, cs), 1)
    eye = ri == ci
    xor = ri ^ ci
    nt = (((1,), (1,)), ((), ()))
    levels = int(math.log2(cs))
    gbv = g_ref[0]
    hi = gbv.astype(BF16)
    rem = gbv - hi.astype(F32)
    mid = rem.astype(BF16)
    parts = (hi, mid, (rem - mid.astype(F32)).astype(BF16))
    cums, cums_t = [], []
    for tri in (ri >= ci, ri <= ci):
        t01 = jnp.where(tri, 1.0, 0.0).astype(BF16)
        cum = sum(jnp.dot(t01, p, preferred_element_type=F32) for p in parts)
        cums.append(cum)
        cums_t.append(cum.T)
    chains = []
    for hd in range(nheads):
        qb = q_ref[0, hd]
        kb = k_ref[0, hd]
        k = kb.astype(F32)
        v = v_ref[0, hd].astype(F32)
        kk = lax.dot_general(kb, kb, nt, preferred_element_type=F32)
        qk = lax.dot_general(qb, kb, nt, preferred_element_type=F32)
        k_t = k.T
        for d in range(2):
            incl = (ri >= ci) if d == 0 else (ri <= ci)
            strict = (ri > ci) if d == 0 else (ri < ci)
            lg = d * nheads + hd
            lb = (2 + d) * nheads + hd
            gc_col = cums[d][:, lg:lg + 1]
            gc_row = cums_t[d][lg:lg + 1, :]
            g_tot = cums[0][cs - 1:cs, lg:lg + 1]
            b_col = gbv[:, lb:lb + 1]
            ex = jnp.exp2(gc_col - gc_row)
            egc = jnp.exp2(gc_col)
            wq_ref[0, d, 0, hd, cs:, :] = (qb.astype(F32) * egc).astype(wq_ref.dtype)
            a2_ref[0, d, 0, hd, :cs, :] = jnp.where(incl, qk * ex, 0.0).astype(a2_ref.dtype)
            a2_ref[0, d, 0, hd, cs:, :] = (k_t * jnp.exp2(g_tot - gc_row)).astype(a2_ref.dtype)
            eg_ref[0, d, 0, hd] = jnp.broadcast_to(jnp.exp2(g_tot), eg_ref.shape[-2:])
            rhs = jnp.concatenate([v * b_col, k * (b_col * egc)], axis=1).astype(BF16)
            neg_off = jnp.where(strict, kk * (-b_col) * ex, 0.0)
            chains.append((hd, d, neg_off, rhs))
    dmats = [jnp.where(eye, 1.0, jnp.where(xor == 1, no, 0.0)) for _, _, no, _ in chains]
    neg_offs = [no.astype(BF16) for _, _, no, _ in chains]
    for lv in range(1, levels):
        sel = jnp.where((xor >> lv) == 1, 1.0, 0.0).astype(BF16)
        dmbs = [dm.astype(BF16) for dm in dmats]
        des = [jnp.dot(db, no * sel, preferred_element_type=F32) for db, no in zip(dmbs, neg_offs)]
        dmats = [dm + jnp.dot(de.astype(BF16), db, preferred_element_type=F32)
                 for dm, de, db in zip(dmats, des, dmbs)]
    sols = [jnp.dot(dm.astype(BF16), ch[3], preferred_element_type=F32) for dm, ch in zip(dmats, chains)]
    dv = u_ref.shape[-1]
    for (hd, d, _, _), sol in zip(chains, sols):
        u_ref[0, d, 0, hd] = sol[:, :dv].astype(u_ref.dtype)
        wq_ref[0, d, 0, hd, :cs, :] = sol[:, dv:].astype(wq_ref.dtype)


def _prep(q, k, v, gb, cs=SCAN_CHUNK):
    b, nh, l, dk = q.shape
    dv = v.shape[-1]
    nc = l // cs
    tok = lambda wd: pl.BlockSpec((1, nh, cs, wd), lambda bi, n: (bi, 0, n, 0))
    out = lambda r, wd: pl.BlockSpec((1, 2, 1, nh, r, wd), lambda bi, n: (bi, 0, n, 0, 0, 0))
    return pl.pallas_call(
        functools.partial(_prep_kernel, nheads=nh, cs=cs),
        out_shape=[jax.ShapeDtypeStruct((b, 2, nc, nh, 2 * cs, dk), BF16),
                   jax.ShapeDtypeStruct((b, 2, nc, nh, cs + dk, cs), BF16),
                   jax.ShapeDtypeStruct((b, 2, nc, nh, cs, dv), BF16),
                   jax.ShapeDtypeStruct((b, 2, nc, nh, 8, 128), F32)],
        grid=(b, nc),
        in_specs=[tok(dk), tok(dk), tok(dv), pl.BlockSpec((1, cs, GB_LANES), lambda bi, n: (bi, n, 0))],
        out_specs=[out(2 * cs, dk), out(cs + dk, cs), out(cs, dv), out(8, 128)],
        compiler_params=_cparams("parallel", "parallel"), name="delta_prep",
    )(q, k, v, gb)


def _scan_kernel(wqf_ref, a2f_ref, uf_ref, egf_ref, wqb_ref, a2b_ref, ub_ref, egb_ref, s0_ref,
                 of_ref, ob_ref, sfin_ref, s_scr, sb_scr, *, hb, cs):
    n = pl.program_id(2)

    @pl.when(n == 0)
    def _():
        s_scr[...] = s0_ref[0]
        sb_scr[...] = s0_ref[0].astype(sb_scr.dtype)

    dirs = ((wqf_ref, a2f_ref, uf_ref, egf_ref, of_ref), (wqb_ref, a2b_ref, ub_ref, egb_ref, ob_ref))
    chains = [(j, d) + refs for j in range(hb) for d, refs in enumerate(dirs)]
    r1s = [jnp.dot(wq_ref[0, 0, 0, j], sb_scr[j, d], preferred_element_type=F32)
           for j, d, wq_ref, _, _, _, _ in chains]
    vns = [(u_ref[0, 0, 0, j].astype(F32) - r1[:cs]).astype(BF16)
           for (j, _, _, _, u_ref, _, _), r1 in zip(chains, r1s)]
    r2s = [jnp.dot(a2_ref[0, 0, 0, j], vn, preferred_element_type=F32)
           for (j, _, _, a2_ref, _, _, _), vn in zip(chains, vns)]
    for (j, d, _, _, _, eg_ref, o_ref), r1, r2 in zip(chains, r1s, r2s):
        o_ref[0, j] = (r1[cs:] + r2[:cs]).astype(o_ref.dtype)
        s_new = s_scr[j, d] * eg_ref[0, 0, 0, j][0:1, 0:1] + r2[cs:]
        s_scr[j, d] = s_new
        sb_scr[j, d] = s_new.astype(sb_scr.dtype)

    @pl.when(n == pl.num_programs(2) - 1)
    def _():
        sfin_ref[0] = s_scr[...]


def _scan(wq, a2, u, eg, s0, hb=8):
    b, _, nc, nh, cs, dv = u.shape
    dk = wq.shape[-1]
    spec = lambda r, wd, d, rev: pl.BlockSpec(
        (1, 1, 1, hb, r, wd), lambda bi, hi, n: (bi, d, (nc - 1 - n) if rev else n, hi, 0, 0))
    both = lambda r, wd: [spec(r, wd, 0, False), spec(r, wd, 1, True)]
    ins = [both(2 * cs, dk), both(cs + dk, cs), both(cs, dv), both(8, 128)]
    st = pl.BlockSpec((1, hb, 2, dk, dv), lambda bi, hi, n: (bi, hi, 0, 0, 0))
    return pl.pallas_call(
        functools.partial(_scan_kernel, hb=hb, cs=cs),
        out_shape=[jax.ShapeDtypeStruct((b, nh, nc * cs, dv), BF16),
                   jax.ShapeDtypeStruct((b, nh, nc * cs, dv), BF16),
                   jax.ShapeDtypeStruct((b, nh, 2, dk, dv), F32)],
        grid=(b, nh // hb, nc),
        in_specs=[s[0] for s in ins] + [s[1] for s in ins] + [st],
        out_specs=[pl.BlockSpec((1, hb, cs, dv), lambda bi, hi, n: (bi, hi, n, 0)),
                   pl.BlockSpec((1, hb, cs, dv), lambda bi, hi, n: (bi, hi, nc - 1 - n, 0)),
                   st],
        scratch_shapes=[pltpu.VMEM((hb, 2, dk, dv), F32), pltpu.VMEM((hb, 2, dk, dv), BF16)],
        compiler_params=_cparams("parallel", "parallel", "arbitrary"), name="delta_scan",
    )(wq, a2, u, eg, wq, a2, u, eg, s0)


def _dn_out_kernel(of_ref, ob_ref, z_ref, res_ref, mod_ref, ng_ref, w_ref, fg_ref, o_ref, *, nheads, dv, d):
    acc = None
    for hd in range(nheads):
        sl = slice(hd * dv, (hd + 1) * dv)
        o = of_ref[0, hd].astype(F32) + ob_ref[0, hd].astype(F32)
        o = o * lax.rsqrt(jnp.mean(o * o, axis=-1, keepdims=True) + EPS) * ng_ref[...]
        y = (o * _silu(z_ref[0, :, sl].astype(F32))).astype(BF16)
        p = jnp.dot(y, w_ref[sl, :], preferred_element_type=F32)
        acc = p if acc is None else acc + p
    gate = mod_ref[0, :, 2 * d:3 * d]
    hn = res_ref[0] + gate * acc
    o_ref[0] = hn * lax.rsqrt(jnp.mean(hn * hn, axis=-1, keepdims=True) + EPS) * fg_ref[...]


def _dn_out(o_f, o_b, z, res, mod, norm_g, w_out, final_g, tm=512):
    b, nh, l, dv = o_f.shape
    d = res.shape[-1]
    tm = min(tm, l)
    tok = lambda wd: pl.BlockSpec((1, tm, wd), lambda i, j: (i, j, 0))
    hs = pl.BlockSpec((1, nh, tm, dv), lambda i, j: (i, 0, j, 0))
    return pl.pallas_call(
        functools.partial(_dn_out_kernel, nheads=nh, dv=dv, d=d),
        out_shape=jax.ShapeDtypeStruct((b, l, d), F32),
        grid=(b, l // tm),
        in_specs=[hs, hs, tok(nh * dv), tok(d),
                  pl.BlockSpec((1, 1, 3 * d), lambda i, j: (i, 0, 0)),
                  _const_spec((1, dv)), _const_spec(w_out.shape), _const_spec((1, d))],
        out_specs=tok(d),
        compiler_params=_cparams("parallel", "parallel"), name="delta_out",
    )(o_f, o_b, z, res, mod, norm_g.reshape(1, dv), w_out, final_g.reshape(1, d))


def _fourier_layer(h_lat, h_ctx, norm_g, mod_lat, mod_ctx, w_in, w_grp, w_out):
    b, l, _ = h_lat.shape
    lc = h_ctx.shape[1]
    width = w_in.shape[1] // 2
    gd = w_grp.shape[-1]
    w_in = w_in.astype(BF16)
    w_out = w_out.astype(BF16)
    gc, gs = _grp(w_grp)

    u, z = _nmm(h_lat, norm_g, mod_lat, w_in, (width, width), (BF16, BF16))
    n2 = DFT_N2
    n1 = l // n2
    y = _dft1(u, n1, n2)
    xr, xi = _dft2_lat(_dft_complex_matrix(n2), y)
    xr = xr.reshape(b, l, width)
    xi = xi.reshape(b, l, width)
    out_lat = _fn_out(xr, xi, z, h_lat, mod_lat, gc, gs, w_out, 1.0 / math.sqrt(l * gd))

    uc, zc = _nmm(h_ctx, norm_g, mod_ctx, w_in, (width, width), (BF16, BF16))
    xrc, xic = _dft2(_dft_real_matrix(lc), uc)
    out_ctx = _fn_out(xrc, xic, zc, h_ctx, mod_ctx, gc, gs, w_out, 1.0 / math.sqrt(lc * gd))
    return out_lat, out_ctx


def _delta_layer_final(h_lat, h_ctx, norm_g, mod_lat, mod_ctx, w_in, conv_w, a_log, dt_bias, dn_norm_g, w_out,
                       final_g):
    b, l, d = h_lat.shape
    lc = h_ctx.shape[1]
    main_w = 2 * DN_HEADS * DN_DK + 2 * DN_HEADS * DN_DV
    n_ab = 4 * DN_HEADS
    w_cat = jnp.concatenate(
        [w_in[:, :main_w], w_in[:, main_w:], jnp.zeros((d, GB_LANES - n_ab), w_in.dtype)], axis=1).astype(BF16)
    pad = jnp.zeros((GB_LANES - 2 * DN_HEADS,), F32)
    abp = jnp.stack([jnp.concatenate([a_log.reshape(-1).astype(F32), pad]),
                     jnp.concatenate([dt_bias.reshape(-1).astype(F32), pad])])
    qc, kc, vc, _, gbc = _proj_conv(h_ctx, norm_g, mod_ctx, w_cat, abp, conv_w, 1, lc)
    ql, kl, vl, zl, gbl = _proj_conv(h_lat, norm_g, mod_lat, w_cat, abp, conv_w, l // GRID_W, GRID_W)
    s0 = jnp.zeros((b, DN_HEADS, 2, DN_DK, DN_DV), F32)
    _, _, s_ctx = _scan(*_prep(qc, kc, vc, gbc), s0)
    o_f, o_b, _ = _scan(*_prep(ql, kl, vl, gbl), s_ctx)
    return _dn_out(o_f, o_b, zl, h_lat, mod_lat, dn_norm_g, w_out.astype(BF16), final_g)


def kernel(x, c, ctx, c_ctx, mod_w, mod_b, norm_g, final_g, fn_w_in, fn_w_grp, fn_w_out,
           dn_w_in, dn_conv, dn_a_log, dn_dt_bias, dn_norm_g, dn_w_out):
    b, l, d = x.shape
    cond = jnp.concatenate([c, c_ctx[None, :], jnp.zeros((8 - b - 1, d), c.dtype)], axis=0)
    mods = _mods(cond, mod_w, mod_b)
    mod_lat = [mods[i, :b].reshape(b, 1, 3 * d) for i in range(2)]
    mod_ctx = [mods[i, b:b + 1].reshape(1, 1, 3 * d) for i in range(2)]
    h_lat, h_ctx = _fourier_layer(x, ctx, norm_g[0], mod_lat[0], mod_ctx[0], fn_w_in[0], fn_w_grp[0], fn_w_out[0])
    return _delta_layer_final(h_lat, h_ctx, norm_g[1], mod_lat[1], mod_ctx[1], dn_w_in[0], dn_conv[0],
                              dn_a_log[0], dn_dt_bias[0], dn_norm_g[0], dn_w_out[0], final_g)
```

```python
import functools
import math

import numpy as np
import jax
import jax.numpy as jnp
from jax import lax
from jax.experimental import pallas as pl
from jax.experimental.pallas import tpu as pltpu

F32 = jnp.float32
BF16 = jnp.bfloat16
HI = lax.Precision.HIGHEST
EPS = 1e-6
LOG2E = 1.4426950408889634

GRID_W = 64
FN_GROUPS = 8
DN_HEADS = 8
DN_DK = 128
DN_DV = 256
SCAN_CHUNK = 128
DFT_N2 = 128
GB_LANES = 128
VMEM_LIMIT_BYTES = 48 * 1024 * 1024


def _cparams(*sem):
    return pltpu.CompilerParams(dimension_semantics=sem, vmem_limit_bytes=VMEM_LIMIT_BYTES)


def _const_spec(shape):
    zeros = (0,) * len(shape)
    return pl.BlockSpec(shape, lambda *_: zeros, pipeline_mode=pl.Buffered(1))


def _silu(v):
    return v * jax.nn.sigmoid(v)


def _mods_kernel(cond_ref, w_ref, b_ref, o_ref):
    s = _silu(cond_ref[...])
    o_ref[0] = jnp.dot(s, w_ref[0], preferred_element_type=F32, precision=HI) + b_ref[0]


def _mods(cond, mod_w, mod_b):
    depth, d, d3 = mod_w.shape
    rows = cond.shape[0]
    tn = 512
    return pl.pallas_call(
        _mods_kernel,
        out_shape=jax.ShapeDtypeStruct((depth, rows, d3), F32),
        grid=(depth, d3 // tn),
        in_specs=[pl.BlockSpec((rows, d), lambda i, j: (0, 0)),
                  pl.BlockSpec((1, d, tn), lambda i, j: (i, 0, j)),
                  pl.BlockSpec((1, 1, tn), lambda i, j: (i, 0, j))],
        out_specs=pl.BlockSpec((1, rows, tn), lambda i, j: (i, 0, j)),
        compiler_params=_cparams("parallel", "parallel"), name="adaln_mods",
    )(cond, mod_w, mod_b.reshape(depth, 1, d3))


def _norm_mod(x, g_ref, mod_ref, d):
    y = x * lax.rsqrt(jnp.mean(x * x, axis=-1, keepdims=True) + EPS) * g_ref[...]
    return y * (1.0 + mod_ref[0, :, d:2 * d]) + mod_ref[0, :, 0:d]


def _nmm_kernel(x_ref, g_ref, mod_ref, w_ref, *outs, splits, d):
    n = _norm_mod(x_ref[0], g_ref, mod_ref, d).astype(BF16)
    c0 = 0
    for o_ref, width in zip(outs, splits):
        step = min(width, 1024)
        for s in range(0, width, step):
            r = jnp.dot(n, w_ref[:, c0 + s:c0 + s + step], preferred_element_type=F32)
            o_ref[0, :, s:s + step] = r.astype(o_ref.dtype)
        c0 += width


def _mod_map(mod):
    return (lambda i, j: (i, 0, 0)) if mod.shape[0] > 1 else (lambda i, j: (0, 0, 0))


def _nmm(x, g, mod, w, splits, dtypes, tm=512):
    b, l, d = x.shape
    tm = min(tm, l)
    return pl.pallas_call(
        functools.partial(_nmm_kernel, splits=splits, d=d),
        out_shape=[jax.ShapeDtypeStruct((b, l, wd), dt) for wd, dt in zip(splits, dtypes)],
        grid=(b, l // tm),
        in_specs=[pl.BlockSpec((1, tm, d), lambda i, j: (i, j, 0)),
                  _const_spec((1, d)),
                  pl.BlockSpec((1, 1, 3 * d), _mod_map(mod)),
                  _const_spec(w.shape)],
        out_specs=[pl.BlockSpec((1, tm, wd), lambda i, j: (i, j, 0)) for wd in splits],
        compiler_params=_cparams("parallel", "parallel"), name="norm_proj",
    )(x, g.reshape(1, d), mod, w)


def _dft_cos_sin(n):
    k = np.arange(n)
    ang = 2.0 * np.pi * ((k[:, None] * k[None, :]) % n) / n
    return np.cos(ang), np.sin(ang)


def _dft1_kernel(m_ref, u_ref, cos_ref, sin_ref, o_ref, *, n1, tn2):
    ut = jnp.transpose(u_ref[0], (1, 0, 2))
    for j in range(tn2):
        r = jnp.dot(m_ref[...], ut[j], preferred_element_type=F32)
        yr = r[:n1]
        yi = r[n1:]
        cs = cos_ref[j]
        sn = sin_ref[j]
        o_ref[0, 0, j] = (yr * cs + yi * sn).astype(o_ref.dtype)
        o_ref[0, 1, j] = (yi * cs - yr * sn).astype(o_ref.dtype)


def _dft1(u, n1, n2, tn2=16, tc=1024):
    b, l, c = u.shape
    cm, sm = _dft_cos_sin(n1)
    m1 = jnp.asarray(np.concatenate([cm, -sm], axis=0), F32).astype(BF16)
    ang = 2.0 * np.pi * (np.arange(n2)[:, None] * np.arange(n1)[None, :]) / l
    tw_cos = jnp.asarray(np.cos(ang)[:, :, None], F32)
    tw_sin = jnp.asarray(np.sin(ang)[:, :, None], F32)
    return pl.pallas_call(
        functools.partial(_dft1_kernel, n1=n1, tn2=tn2),
        out_shape=jax.ShapeDtypeStruct((b, 2, n2, n1, c), BF16),
        grid=(b, n2 // tn2, c // tc),
        in_specs=[_const_spec((2 * n1, n1)),
                  pl.BlockSpec((1, n1, tn2, tc), lambda i, j, k: (i, 0, j, k)),
                  pl.BlockSpec((tn2, n1, 1), lambda i, j, k: (j, 0, 0)),
                  pl.BlockSpec((tn2, n1, 1), lambda i, j, k: (j, 0, 0))],
        out_specs=pl.BlockSpec((1, 2, tn2, n1, tc), lambda i, j, k: (i, 0, j, 0, k)),
        compiler_params=_cparams("parallel", "parallel", "parallel"), name="dft_stage1",
    )(m1, u.reshape(b, n1, n2, c), tw_cos, tw_sin)


def _dft2_lat_kernel(m_ref, y_ref, xr_ref, xi_ref, *, n2, tk1):
    yr = jnp.transpose(y_ref[0, 0], (1, 0, 2))
    yi = jnp.transpose(y_ref[0, 1], (1, 0, 2))
    xr, xi = [], []
    for j in range(tk1):
        r = jnp.dot(m_ref[...], jnp.concatenate([yr[j], yi[j]], axis=0), preferred_element_type=F32)
        xr.append(r[:n2].astype(xr_ref.dtype))
        xi.append(r[n2:].astype(xi_ref.dtype))
    xr_ref[0] = jnp.transpose(jnp.stack(xr), (1, 0, 2))
    xi_ref[0] = jnp.transpose(jnp.stack(xi), (1, 0, 2))


def _dft2_lat(m, y, tk1=16, tc=256):
    b, _, n2, n1, c = y.shape
    tk1 = min(tk1, n1)
    return pl.pallas_call(
        functools.partial(_dft2_lat_kernel, n2=n2, tk1=tk1),
        out_shape=[jax.ShapeDtypeStruct((b, n2, n1, c), BF16)] * 2,
        grid=(b, n1 // tk1, c // tc),
        in_specs=[_const_spec(m.shape),
                  pl.BlockSpec((1, 2, n2, tk1, tc), lambda i, j, k: (i, 0, 0, j, k))],
        out_specs=[pl.BlockSpec((1, n2, tk1, tc), lambda i, j, k: (i, 0, j, k))] * 2,
        compiler_params=_cparams("parallel", "parallel", "parallel"), name="dft_stage2_lat",
    )(m, y)


def _dft2_kernel(m_ref, y_ref, xr_ref, xi_ref, *, half):
    r = jnp.dot(m_ref[...], y_ref[0], preferred_element_type=F32)
    xr_ref[0] = r[:half].astype(xr_ref.dtype)
    xi_ref[0] = r[half:].astype(xi_ref.dtype)


def _dft2(m, y, tc=2048):
    b, k, cols = y.shape
    half = m.shape[0] // 2
    return pl.pallas_call(
        functools.partial(_dft2_kernel, half=half),
        out_shape=[jax.ShapeDtypeStruct((b, half, cols), BF16)] * 2,
        grid=(b, cols // tc),
        in_specs=[_const_spec(m.shape),
                  pl.BlockSpec((1, k, tc), lambda i, j: (i, 0, j))],
        out_specs=[pl.BlockSpec((1, half, tc), lambda i, j: (i, 0, j))] * 2,
        compiler_params=_cparams("parallel", "parallel"), name="dft_stage2",
    )(m, y)


def _dft_real_matrix(n):
    cm, sm = _dft_cos_sin(n)
    return jnp.asarray(np.concatenate([cm, -sm], axis=0), F32).astype(BF16)


def _dft_complex_matrix(n):
    cm, sm = _dft_cos_sin(n)
    return jnp.asarray(np.block([[cm, sm], [-sm, cm]]), F32).astype(BF16)


def _grp_kernel(m_ref, w_ref, gc_ref, gs_ref, *, n):
    r = jnp.dot(m_ref[...], w_ref[0], preferred_element_type=F32, precision=HI)
    gc_ref[0] = r[:n].astype(gc_ref.dtype)
    gs_ref[0] = r[n:].astype(gs_ref.dtype)


def _grp(w_grp):
    groups, n, _ = w_grp.shape
    cm, sm = _dft_cos_sin(n)
    m = jnp.asarray(np.concatenate([cm, sm], axis=0), F32)
    return pl.pallas_call(
        functools.partial(_grp_kernel, n=n),
        out_shape=[jax.ShapeDtypeStruct((groups, n, n), BF16)] * 2,
        grid=(groups,),
        in_specs=[_const_spec((2 * n, n)),
                  pl.BlockSpec((1, n, n), lambda g: (g, 0, 0))],
        out_specs=[pl.BlockSpec((1, n, n), lambda g: (g, 0, 0))] * 2,
        compiler_params=_cparams("parallel"), name="group_weights",
    )(m, w_grp)


def _fn_out_kernel(xr_ref, xi_ref, z_ref, res_ref, mod_ref, gc_ref, gs_ref, w_ref, o_ref, *, groups, gd, d, scale):
    acc = None
    for g in range(groups):
        sl = slice(g * gd, (g + 1) * gd)
        f = (jnp.dot(xr_ref[0, :, sl], gc_ref[g], preferred_element_type=F32)
             + jnp.dot(xi_ref[0, :, sl], gs_ref[g], preferred_element_type=F32))
        y = (f * scale * _silu(z_ref[0, :, sl].astype(F32))).astype(BF16)
        p = jnp.dot(y, w_ref[sl, :], preferred_element_type=F32)
        acc = p if acc is None else acc + p
    gate = mod_ref[0, :, 2 * d:3 * d]
    o_ref[0] = res_ref[0] + gate * acc


def _fn_out(xr, xi, z, res, mod, gc, gs, w_out, scale, tm=512):
    b, l, c = xr.shape
    d = res.shape[-1]
    tm = min(tm, l)
    groups, gd, _ = gc.shape
    tok = lambda wd: pl.BlockSpec((1, tm, wd), lambda i, j: (i, j, 0))
    return pl.pallas_call(
        functools.partial(_fn_out_kernel, groups=groups, gd=gd, d=d, scale=scale),
        out_shape=jax.ShapeDtypeStruct((b, l, d), F32),
        grid=(b, l // tm),
        in_specs=[tok(c), tok(c), tok(c), tok(d),
                  pl.BlockSpec((1, 1, 3 * d), _mod_map(mod)),
                  _const_spec(gc.shape), _const_spec(gs.shape), _const_spec(w_out.shape)],
        out_specs=tok(d),
        compiler_params=_cparams("parallel", "parallel"), name="fourier_out",
    )(xr, xi, z, res, mod, gc, gs, w_out)


def _proj_conv_kernel(xp_ref, xm_ref, xn_ref, g_ref, mod_ref, w_ref, abp_ref, cw_ref, ml_ref, mr_ref,
                      q_ref, k_ref, v_ref, z_ref, gb_ref, *, d, cols, tr, nheads, dk, dv, cb):
    i = pl.program_id(1)
    last = pl.num_programs(1) - 1
    n_prev = (_norm_mod(xp_ref[0], g_ref, mod_ref, d) * (i > 0).astype(F32)).astype(BF16)
    n_main = _norm_mod(xm_ref[0], g_ref, mod_ref, d).astype(BF16)
    n_next = (_norm_mod(xn_ref[0], g_ref, mod_ref, d) * (i < last).astype(F32)).astype(BF16)
    ne = jnp.concatenate([n_prev, n_main, n_next], axis=0)
    n = tr * cols
    qk_w = nheads * dk
    v_w = nheads * dv
    for c0 in range(0, 2 * qk_w + v_w, cb):
        e = jnp.dot(ne, w_ref[:, c0:c0 + cb], preferred_element_type=F32)
        part = []
        for dc in range(3):
            acc = None
            for dr in range(3):
                term = e[dr * cols:dr * cols + n] * cw_ref[dr, dc:dc + 1, c0:c0 + cb]
                acc = term if acc is None else acc + term
            part.append(acc)
        y = _silu(pltpu.roll(part[0], 1, axis=0) * ml_ref[...] + part[1]
                  + pltpu.roll(part[2], n - 1, axis=0) * mr_ref[...])
        if c0 < 2 * qk_w:
            o_ref, scale, base = (q_ref, dk ** -0.5, c0) if c0 < qk_w else (k_ref, 1.0, c0 - qk_w)
            for j in range(cb // dk):
                yh = y[:, j * dk:(j + 1) * dk]
                yh = yh * lax.rsqrt(jnp.sum(yh * yh, axis=-1, keepdims=True) + EPS) * scale
                o_ref[0, base // dk + j] = yh.astype(o_ref.dtype)
        else:
            base = c0 - 2 * qk_w
            for j in range(cb // dv):
                v_ref[0, base // dv + j] = y[:, j * dv:(j + 1) * dv].astype(v_ref.dtype)
    z0 = 2 * qk_w + v_w
    zstep = min(v_w, 1024)
    for s in range(0, v_w, zstep):
        r = jnp.dot(n_main, w_ref[:, z0 + s:z0 + s + zstep], preferred_element_type=F32)
        z_ref[0, :, s:s + zstep] = r.astype(z_ref.dtype)
    r = jnp.dot(n_main, w_ref[:, z0 + v_w:z0 + v_w + GB_LANES], preferred_element_type=F32)
    t = r + abp_ref[1:2, :]
    softplus = jnp.maximum(t, 0.0) + jnp.log(1.0 + jnp.exp(-jnp.abs(t)))
    lane = lax.broadcasted_iota(jnp.int32, r.shape, 1)
    gb_ref[0] = jnp.where(lane < 2 * nheads, (-LOG2E) * jnp.exp(abp_ref[0:1, :]) * softplus, jax.nn.sigmoid(r))


def _proj_conv(x, g, mod, w, abp, conv_w, rows, cols, cb=512):
    b, l, d = x.shape
    tr = min(rows, max(1, 512 // cols))
    n = tr * cols
    col = np.arange(n) % cols
    ml = jnp.asarray(np.broadcast_to((col != 0)[:, None], (n, cb)), F32)
    mr = jnp.asarray(np.broadcast_to((col != cols - 1)[:, None], (n, cb)), F32)
    heads = lambda wd: pl.BlockSpec((1, DN_HEADS, n, wd), lambda bi, i: (bi, 0, i, 0))
    tok = lambda wd: pl.BlockSpec((1, n, wd), lambda bi, i: (bi, i, 0))
    return pl.pallas_call(
        functools.partial(_proj_conv_kernel, d=d, cols=cols, tr=tr, nheads=DN_HEADS, dk=DN_DK, dv=DN_DV, cb=cb),
        out_shape=[jax.ShapeDtypeStruct((b, DN_HEADS, l, DN_DK), BF16),
                   jax.ShapeDtypeStruct((b, DN_HEADS, l, DN_DK), BF16),
                   jax.ShapeDtypeStruct((b, DN_HEADS, l, DN_DV), BF16),
                   jax.ShapeDtypeStruct((b, l, DN_HEADS * DN_DV), BF16),
                   jax.ShapeDtypeStruct((b, l, GB_LANES), F32)],
        grid=(b, rows // tr),
        in_specs=[pl.BlockSpec((1, cols, d), lambda bi, i: (bi, jnp.maximum(i * tr - 1, 0), 0)),
                  pl.BlockSpec((1, n, d), lambda bi, i: (bi, i, 0)),
                  pl.BlockSpec((1, cols, d), lambda bi, i: (bi, jnp.minimum((i + 1) * tr, rows - 1), 0)),
                  _const_spec((1, d)),
                  pl.BlockSpec((1, 1, 3 * d), _mod_map(mod)),
                  _const_spec(w.shape), _const_spec(abp.shape), _const_spec(conv_w.shape),
                  _const_spec((n, cb)), _const_spec((n, cb))],
        out_specs=[heads(DN_DK), heads(DN_DK), heads(DN_DV), tok(DN_HEADS * DN_DV), tok(GB_LANES)],
        compiler_params=_cparams("parallel", "parallel"), name="proj_conv",
    )(x, x, x, g.reshape(1, d), mod, w, abp, conv_w, ml, mr)


def _delta_kernel(qa_ref, ka_ref, va_ref, ga_ref, qb_ref, kb_ref, vb_ref, gb_ref, s0_ref,
                  of_ref, ob_ref, sfin_ref, s_scr, sb_scr, *, nheads, cs):
    n = pl.program_id(1)

    @pl.when(n == 0)
    def _():
        s_scr[...] = s0_ref[0]
        sb_scr[...] = s0_ref[0].astype(sb_scr.dtype)

    ri = lax.broadcasted_iota(jnp.int32, (cs, cs), 0)
    ci = lax.broadcasted_iota(jnp.int32, (cs, cs), 1)
    eye = ri == ci
    xor = ri ^ ci
    nt = (((1,), (1,)), ((), ()))
    levels = int(math.log2(cs))
    chains = []
    for d, (q_ref, k_ref, v_ref, g_ref, o_ref) in enumerate(((qa_ref, ka_ref, va_ref, ga_ref, of_ref),
                                                              (qb_ref, kb_ref, vb_ref, gb_ref, ob_ref))):
        incl = (ri >= ci) if d == 0 else (ri <= ci)
        strict = (ri > ci) if d == 0 else (ri < ci)
        gbv = g_ref[0]
        hi = gbv.astype(BF16)
        rem = gbv - hi.astype(F32)
        mid = rem.astype(BF16)
        t01 = jnp.where(incl, 1.0, 0.0).astype(BF16)
        cum = sum(jnp.dot(t01, p, preferred_element_type=F32)
                  for p in (hi, mid, (rem - mid.astype(F32)).astype(BF16)))
        cum_t = cum.T
        last = cs - 1 if d == 0 else 0
        for hd in range(nheads):
            qb = q_ref[0, hd]
            kb = k_ref[0, hd]
            k = kb.astype(F32)
            v = v_ref[0, hd].astype(F32)
            kk = lax.dot_general(kb, kb, nt, preferred_element_type=F32)
            qk = lax.dot_general(qb, kb, nt, preferred_element_type=F32)
            lg = d * nheads + hd
            lb = (2 + d) * nheads + hd
            gc_col = cum[:, lg:lg + 1]
            gc_row = cum_t[lg:lg + 1, :]
            g_tot = cum[last:last + 1, lg:lg + 1]
            b_col = gbv[:, lb:lb + 1]
            ex = jnp.exp2(gc_col - gc_row)
            egc = jnp.exp2(gc_col)
            chains.append(dict(
                hd=hd, d=d, o_ref=o_ref, eg=jnp.exp2(g_tot),
                qh=(qb.astype(F32) * egc).astype(BF16),
                qkd=jnp.where(incl, qk * ex, 0.0).astype(BF16),
                ktt=(k.T * jnp.exp2(g_tot - gc_row)).astype(BF16),
                rhs=jnp.concatenate([v * b_col, k * (b_col * egc)], axis=1).astype(BF16),
                neg_off=jnp.where(strict, kk * (-b_col) * ex, 0.0)))
    dmats = [jnp.where(eye, 1.0, jnp.where(xor == 1, ch["neg_off"], 0.0)) for ch in chains]
    neg_offs = [ch["neg_off"].astype(BF16) for ch in chains]
    for lv in range(1, levels):
        sel = jnp.where((xor >> lv) == 1, 1.0, 0.0).astype(BF16)
        dmbs = [dm.astype(BF16) for dm in dmats]
        des = [jnp.dot(db, no * sel, preferred_element_type=F32) for db, no in zip(dmbs, neg_offs)]
        dmats = [dm + jnp.dot(de.astype(BF16), db, preferred_element_type=F32)
                 for dm, de, db in zip(dmats, des, dmbs)]
    sols = [jnp.dot(dm.astype(BF16), ch["rhs"], preferred_element_type=F32) for dm, ch in zip(dmats, chains)]
    dv = of_ref.shape[-1]
    r1s = [jnp.dot(jnp.concatenate([sol[:, dv:].astype(BF16), ch["qh"]], axis=0), sb_scr[ch["hd"], ch["d"]],
                   preferred_element_type=F32) for ch, sol in zip(chains, sols)]
    vns = [(sol[:, :dv] - r1[:cs]).astype(BF16) for sol, r1 in zip(sols, r1s)]
    r2s = [jnp.dot(jnp.concatenate([ch["qkd"], ch["ktt"]], axis=0), vn, preferred_element_type=F32)
           for ch, vn in zip(chains, vns)]
    for ch, r1, r2 in zip(chains, r1s, r2s):
        hd, d = ch["hd"], ch["d"]
        ch["o_ref"][0, hd] = (r1[cs:] + r2[:cs]).astype(of_ref.dtype)
        s_new = s_scr[hd, d] * ch["eg"] + r2[cs:]
        s_scr[hd, d] = s_new
        sb_scr[hd, d] = s_new.astype(sb_scr.dtype)

    @pl.when(n == pl.num_programs(1) - 1)
    def _():
        sfin_ref[0] = s_scr[...]


def _delta(q, k, v, gb, s0, cs=SCAN_CHUNK):
    b, nh, l, dk = q.shape
    dv = v.shape[-1]
    nc = l // cs
    fwd = lambda wd: pl.BlockSpec((1, nh, cs, wd), lambda bi, n: (bi, 0, n, 0))
    bwd = lambda wd: pl.BlockSpec((1, nh, cs, wd), lambda bi, n: (bi, 0, nc - 1 - n, 0))
    st = pl.BlockSpec((1, nh, 2, dk, dv), lambda bi, n: (bi, 0, 0, 0, 0))
    return pl.pallas_call(
        functools.partial(_delta_kernel, nheads=nh, cs=cs),
        out_shape=[jax.ShapeDtypeStruct((b, nh, l, dv), BF16),
                   jax.ShapeDtypeStruct((b, nh, l, dv), BF16),
                   jax.ShapeDtypeStruct((b, nh, 2, dk, dv), F32)],
        grid=(b, nc),
        in_specs=[fwd(dk), fwd(dk), fwd(dv), pl.BlockSpec((1, cs, GB_LANES), lambda bi, n: (bi, n, 0)),
                  bwd(dk), bwd(dk), bwd(dv), pl.BlockSpec((1, cs, GB_LANES), lambda bi, n: (bi, nc - 1 - n, 0)),
                  st],
        out_specs=[fwd(dv), bwd(dv), st],
        scratch_shapes=[pltpu.VMEM((nh, 2, dk, dv), F32), pltpu.VMEM((nh, 2, dk, dv), BF16)],
        compiler_params=_cparams("parallel", "arbitrary"), name="delta_rule",
    )(q, k, v, gb, q, k, v, gb, s0)


def _dn_out_kernel(of_ref, ob_ref, z_ref, res_ref, mod_ref, ng_ref, w_ref, fg_ref, o_ref, *, nheads, dv, d):
    acc = None
    for hd in range(nheads):
        sl = slice(hd * dv, (hd + 1) * dv)
        o = of_ref[0, hd].astype(F32) + ob_ref[0, hd].astype(F32)
        o = o * lax.rsqrt(jnp.mean(o * o, axis=-1, keepdims=True) + EPS) * ng_ref[...]
        y = (o * _silu(z_ref[0, :, sl].astype(F32))).astype(BF16)
        p = jnp.dot(y, w_ref[sl, :], preferred_element_type=F32)
        acc = p if acc is None else acc + p
    gate = mod_ref[0, :, 2 * d:3 * d]
    hn = res_ref[0] + gate * acc
    o_ref[0] = hn * lax.rsqrt(jnp.mean(hn * hn, axis=-1, keepdims=True) + EPS) * fg_ref[...]


def _dn_out(o_f, o_b, z, res, mod, norm_g, w_out, final_g, tm=512):
    b, nh, l, dv = o_f.shape
    d = res.shape[-1]
    tm = min(tm, l)
    tok = lambda wd: pl.BlockSpec((1, tm, wd), lambda i, j: (i, j, 0))
    hs = pl.BlockSpec((1, nh, tm, dv), lambda i, j: (i, 0, j, 0))
    return pl.pallas_call(
        functools.partial(_dn_out_kernel, nheads=nh, dv=dv, d=d),
        out_shape=jax.ShapeDtypeStruct((b, l, d), F32),
        grid=(b, l // tm),
        in_specs=[hs, hs, tok(nh * dv), tok(d),
                  pl.BlockSpec((1, 1, 3 * d), lambda i, j: (i, 0, 0)),
                  _const_spec((1, dv)), _const_spec(w_out.shape), _const_spec((1, d))],
        out_specs=tok(d),
        compiler_params=_cparams("parallel", "parallel"), name="delta_out",
    )(o_f, o_b, z, res, mod, norm_g.reshape(1, dv), w_out, final_g.reshape(1, d))


def _fourier_layer(h_lat, h_ctx, norm_g, mod_lat, mod_ctx, w_in, w_grp, w_out):
    b, l, _ = h_lat.shape
    lc = h_ctx.shape[1]
    width = w_in.shape[1] // 2
    gd = w_grp.shape[-1]
    w_in = w_in.astype(BF16)
    w_out = w_out.astype(BF16)
    gc, gs = _grp(w_grp)

    u, z = _nmm(h_lat, norm_g, mod_lat, w_in, (width, width), (BF16, BF16))
    n2 = DFT_N2
    n1 = l // n2
    y = _dft1(u, n1, n2)
    xr, xi = _dft2_lat(_dft_complex_matrix(n2), y)
    xr = xr.reshape(b, l, width)
    xi = xi.reshape(b, l, width)
    out_lat = _fn_out(xr, xi, z, h_lat, mod_lat, gc, gs, w_out, 1.0 / math.sqrt(l * gd))

    uc, zc = _nmm(h_ctx, norm_g, mod_ctx, w_in, (width, width), (BF16, BF16))
    xrc, xic = _dft2(_dft_real_matrix(lc), uc)
    out_ctx = _fn_out(xrc, xic, zc, h_ctx, mod_ctx, gc, gs, w_out, 1.0 / math.sqrt(lc * gd))
    return out_lat, out_ctx


def _delta_layer_final(h_lat, h_ctx, norm_g, mod_lat, mod_ctx, w_in, conv_w, a_log, dt_bias, dn_norm_g, w_out,
                       final_g):
    b, l, d = h_lat.shape
    lc = h_ctx.shape[1]
    main_w = 2 * DN_HEADS * DN_DK + 2 * DN_HEADS * DN_DV
    n_ab = 4 * DN_HEADS
    w_cat = jnp.concatenate(
        [w_in[:, :main_w], w_in[:, main_w:], jnp.zeros((d, GB_LANES - n_ab), w_in.dtype)], axis=1).astype(BF16)
    pad = jnp.zeros((GB_LANES - 2 * DN_HEADS,), F32)
    abp = jnp.stack([jnp.concatenate([a_log.reshape(-1).astype(F32), pad]),
                     jnp.concatenate([dt_bias.reshape(-1).astype(F32), pad])])
    qc, kc, vc, _, gbc = _proj_conv(h_ctx, norm_g, mod_ctx, w_cat, abp, conv_w, 1, lc)
    ql, kl, vl, zl, gbl = _proj_conv(h_lat, norm_g, mod_lat, w_cat, abp, conv_w, l // GRID_W, GRID_W)
    s0 = jnp.zeros((b, DN_HEADS, 2, DN_DK, DN_DV), F32)
    _, _, s_ctx = _delta(qc, kc, vc, gbc, s0)
    o_f, o_b, _ = _delta(ql, kl, vl, gbl, s_ctx)
    return _dn_out(o_f, o_b, zl, h_lat, mod_lat, dn_norm_g, w_out.astype(BF16), final_g)


def kernel(x, c, ctx, c_ctx, mod_w, mod_b, norm_g, final_g, fn_w_in, fn_w_grp, fn_w_out,
           dn_w_in, dn_conv, dn_a_log, dn_dt_bias, dn_norm_g, dn_w_out):
    b, l, d = x.shape
    cond = jnp.concatenate([c, c_ctx[None, :], jnp.zeros((8 - b - 1, d), c.dtype)], axis=0)
    mods = _mods(cond, mod_w, mod_b)
    mod_lat = [mods[i, :b].reshape(b, 1, 3 * d) for i in range(2)]
    mod_ctx = [mods[i, b:b + 1].reshape(1, 1, 3 * d) for i in range(2)]
    h_lat, h_ctx = _fourier_layer(x, ctx, norm_g[0], mod_lat[0], mod_ctx[0], fn_w_in[0], fn_w_grp[0], fn_w_out[0])
    return _delta_layer_final(h_lat, h_ctx, norm_g[1], mod_lat[1], mod_ctx[1], dn_w_in[0], dn_conv[0],
                              dn_a_log[0], dn_dt_bias[0], dn_norm_g[0], dn_w_out[0], final_g)
```

```python
import functools
import math

import numpy as np
import jax
import jax.numpy as jnp
from jax import lax
from jax.experimental import pallas as pl
from jax.experimental.pallas import tpu as pltpu

F32 = jnp.float32
BF16 = jnp.bfloat16
HI = lax.Precision.HIGHEST
EPS = 1e-6
LOG2E = 1.4426950408889634

GRID_W = 64
FN_GROUPS = 8
DN_HEADS = 8
DN_DK = 128
DN_DV = 256
SCAN_CHUNK = 128
DFT_N2 = 128
GB_LANES = 128
VMEM_LIMIT_BYTES = 48 * 1024 * 1024


def _cparams(*sem):
    return pltpu.CompilerParams(dimension_semantics=sem, vmem_limit_bytes=VMEM_LIMIT_BYTES)


def _const_spec(shape):
    zeros = (0,) * len(shape)
    return pl.BlockSpec(shape, lambda *_: zeros, pipeline_mode=pl.Buffered(1))


def _silu(v):
    return v * jax.nn.sigmoid(v)


def _mods_kernel(cond_ref, w_ref, b_ref, o_ref):
    s = _silu(cond_ref[...])
    o_ref[0] = jnp.dot(s, w_ref[0], preferred_element_type=F32, precision=HI) + b_ref[0]


def _mods(cond, mod_w, mod_b):
    depth, d, d3 = mod_w.shape
    rows = cond.shape[0]
    tn = 512
    return pl.pallas_call(
        _mods_kernel,
        out_shape=jax.ShapeDtypeStruct((depth, rows, d3), F32),
        grid=(depth, d3 // tn),
        in_specs=[pl.BlockSpec((rows, d), lambda i, j: (0, 0)),
                  pl.BlockSpec((1, d, tn), lambda i, j: (i, 0, j)),
                  pl.BlockSpec((1, 1, tn), lambda i, j: (i, 0, j))],
        out_specs=pl.BlockSpec((1, rows, tn), lambda i, j: (i, 0, j)),
        compiler_params=_cparams("parallel", "parallel"), name="adaln_mods",
    )(cond, mod_w, mod_b.reshape(depth, 1, d3))


def _norm_mod(x, g_ref, mod_ref, d):
    y = x * lax.rsqrt(jnp.mean(x * x, axis=-1, keepdims=True) + EPS) * g_ref[...]
    return y * (1.0 + mod_ref[0, :, d:2 * d]) + mod_ref[0, :, 0:d]


def _nmm_kernel(x_ref, g_ref, mod_ref, w_ref, *outs, splits, d):
    n = _norm_mod(x_ref[0], g_ref, mod_ref, d).astype(BF16)
    c0 = 0
    for o_ref, width in zip(outs, splits):
        step = min(width, 1024)
        for s in range(0, width, step):
            r = jnp.dot(n, w_ref[:, c0 + s:c0 + s + step], preferred_element_type=F32)
            o_ref[0, :, s:s + step] = r.astype(o_ref.dtype)
        c0 += width


def _mod_map(mod):
    return (lambda i, j: (i, 0, 0)) if mod.shape[0] > 1 else (lambda i, j: (0, 0, 0))


def _nmm(x, g, mod, w, splits, dtypes, tm=512):
    b, l, d = x.shape
    tm = min(tm, l)
    return pl.pallas_call(
        functools.partial(_nmm_kernel, splits=splits, d=d),
        out_shape=[jax.ShapeDtypeStruct((b, l, wd), dt) for wd, dt in zip(splits, dtypes)],
        grid=(b, l // tm),
        in_specs=[pl.BlockSpec((1, tm, d), lambda i, j: (i, j, 0)),
                  _const_spec((1, d)),
                  pl.BlockSpec((1, 1, 3 * d), _mod_map(mod)),
                  _const_spec(w.shape)],
        out_specs=[pl.BlockSpec((1, tm, wd), lambda i, j: (i, j, 0)) for wd in splits],
        compiler_params=_cparams("parallel", "parallel"), name="norm_proj",
    )(x, g.reshape(1, d), mod, w)


def _dft_cos_sin(n):
    k = np.arange(n)
    ang = 2.0 * np.pi * ((k[:, None] * k[None, :]) % n) / n
    return np.cos(ang), np.sin(ang)


def _dft1_kernel(m_ref, u_ref, cos_ref, sin_ref, o_ref, *, n1, tn2):
    ut = jnp.transpose(u_ref[0], (1, 0, 2))
    for j in range(tn2):
        r = jnp.dot(m_ref[...], ut[j], preferred_element_type=F32)
        yr = r[:n1]
        yi = r[n1:]
        cs = cos_ref[j]
        sn = sin_ref[j]
        o_ref[0, 0, j] = (yr * cs + yi * sn).astype(o_ref.dtype)
        o_ref[0, 1, j] = (yi * cs - yr * sn).astype(o_ref.dtype)


def _dft1(u, n1, n2, tn2=16, tc=1024):
    b, l, c = u.shape
    cm, sm = _dft_cos_sin(n1)
    m1 = jnp.asarray(np.concatenate([cm, -sm], axis=0), F32).astype(BF16)
    ang = 2.0 * np.pi * (np.arange(n2)[:, None] * np.arange(n1)[None, :]) / l
    tw_cos = jnp.asarray(np.cos(ang)[:, :, None], F32)
    tw_sin = jnp.asarray(np.sin(ang)[:, :, None], F32)
    return pl.pallas_call(
        functools.partial(_dft1_kernel, n1=n1, tn2=tn2),
        out_shape=jax.ShapeDtypeStruct((b, 2, n2, n1, c), BF16),
        grid=(b, n2 // tn2, c // tc),
        in_specs=[_const_spec((2 * n1, n1)),
                  pl.BlockSpec((1, n1, tn2, tc), lambda i, j, k: (i, 0, j, k)),
                  pl.BlockSpec((tn2, n1, 1), lambda i, j, k: (j, 0, 0)),
                  pl.BlockSpec((tn2, n1, 1), lambda i, j, k: (j, 0, 0))],
        out_specs=pl.BlockSpec((1, 2, tn2, n1, tc), lambda i, j, k: (i, 0, j, 0, k)),
        compiler_params=_cparams("parallel", "parallel", "parallel"), name="dft_stage1",
    )(m1, u.reshape(b, n1, n2, c), tw_cos, tw_sin)


def _dft2_lat_kernel(m_ref, y_ref, xr_ref, xi_ref, *, n2, tk1):
    yr = jnp.transpose(y_ref[0, 0], (1, 0, 2))
    yi = jnp.transpose(y_ref[0, 1], (1, 0, 2))
    xr, xi = [], []
    for j in range(tk1):
        r = jnp.dot(m_ref[...], jnp.concatenate([yr[j], yi[j]], axis=0), preferred_element_type=F32)
        xr.append(r[:n2].astype(xr_ref.dtype))
        xi.append(r[n2:].astype(xi_ref.dtype))
    xr_ref[0] = jnp.transpose(jnp.stack(xr), (1, 0, 2))
    xi_ref[0] = jnp.transpose(jnp.stack(xi), (1, 0, 2))


def _dft2_lat(m, y, tk1=16, tc=256):
    b, _, n2, n1, c = y.shape
    tk1 = min(tk1, n1)
    return pl.pallas_call(
        functools.partial(_dft2_lat_kernel, n2=n2, tk1=tk1),
        out_shape=[jax.ShapeDtypeStruct((b, n2, n1, c), BF16)] * 2,
        grid=(b, n1 // tk1, c // tc),
        in_specs=[_const_spec(m.shape),
                  pl.BlockSpec((1, 2, n2, tk1, tc), lambda i, j, k: (i, 0, 0, j, k))],
        out_specs=[pl.BlockSpec((1, n2, tk1, tc), lambda i, j, k: (i, 0, j, k))] * 2,
        compiler_params=_cparams("parallel", "parallel", "parallel"), name="dft_stage2_lat",
    )(m, y)


def _dft2_kernel(m_ref, y_ref, xr_ref, xi_ref, *, half):
    r = jnp.dot(m_ref[...], y_ref[0], preferred_element_type=F32)
    xr_ref[0] = r[:half].astype(xr_ref.dtype)
    xi_ref[0] = r[half:].astype(xi_ref.dtype)


def _dft2(m, y, tc=2048):
    b, k, cols = y.shape
    half = m.shape[0] // 2
    return pl.pallas_call(
        functools.partial(_dft2_kernel, half=half),
        out_shape=[jax.ShapeDtypeStruct((b, half, cols), BF16)] * 2,
        grid=(b, cols // tc),
        in_specs=[_const_spec(m.shape),
                  pl.BlockSpec((1, k, tc), lambda i, j: (i, 0, j))],
        out_specs=[pl.BlockSpec((1, half, tc), lambda i, j: (i, 0, j))] * 2,
        compiler_params=_cparams("parallel", "parallel"), name="dft_stage2",
    )(m, y)


def _dft_real_matrix(n):
    cm, sm = _dft_cos_sin(n)
    return jnp.asarray(np.concatenate([cm, -sm], axis=0), F32).astype(BF16)


def _dft_complex_matrix(n):
    cm, sm = _dft_cos_sin(n)
    return jnp.asarray(np.block([[cm, sm], [-sm, cm]]), F32).astype(BF16)


def _grp_kernel(m_ref, w_ref, gc_ref, gs_ref, *, n):
    r = jnp.dot(m_ref[...], w_ref[0], preferred_element_type=F32, precision=HI)
    gc_ref[0] = r[:n].astype(gc_ref.dtype)
    gs_ref[0] = r[n:].astype(gs_ref.dtype)


def _grp(w_grp):
    groups, n, _ = w_grp.shape
    cm, sm = _dft_cos_sin(n)
    m = jnp.asarray(np.concatenate([cm, sm], axis=0), F32)
    return pl.pallas_call(
        functools.partial(_grp_kernel, n=n),
        out_shape=[jax.ShapeDtypeStruct((groups, n, n), BF16)] * 2,
        grid=(groups,),
        in_specs=[_const_spec((2 * n, n)),
                  pl.BlockSpec((1, n, n), lambda g: (g, 0, 0))],
        out_specs=[pl.BlockSpec((1, n, n), lambda g: (g, 0, 0))] * 2,
        compiler_params=_cparams("parallel"), name="group_weights",
    )(m, w_grp)


def _fn_out_kernel(xr_ref, xi_ref, z_ref, res_ref, mod_ref, gc_ref, gs_ref, w_ref, o_ref, *, groups, gd, d, scale):
    acc = None
    for g in range(groups):
        sl = slice(g * gd, (g + 1) * gd)
        f = (jnp.dot(xr_ref[0, :, sl], gc_ref[g], preferred_element_type=F32)
             + jnp.dot(xi_ref[0, :, sl], gs_ref[g], preferred_element_type=F32))
        y = (f * scale * _silu(z_ref[0, :, sl].astype(F32))).astype(BF16)
        p = jnp.dot(y, w_ref[sl, :], preferred_element_type=F32)
        acc = p if acc is None else acc + p
    gate = mod_ref[0, :, 2 * d:3 * d]
    o_ref[0] = res_ref[0] + gate * acc


def _fn_out(xr, xi, z, res, mod, gc, gs, w_out, scale, tm=512):
    b, l, c = xr.shape
    d = res.shape[-1]
    tm = min(tm, l)
    groups, gd, _ = gc.shape
    tok = lambda wd: pl.BlockSpec((1, tm, wd), lambda i, j: (i, j, 0))
    return pl.pallas_call(
        functools.partial(_fn_out_kernel, groups=groups, gd=gd, d=d, scale=scale),
        out_shape=jax.ShapeDtypeStruct((b, l, d), F32),
        grid=(b, l // tm),
        in_specs=[tok(c), tok(c), tok(c), tok(d),
                  pl.BlockSpec((1, 1, 3 * d), _mod_map(mod)),
                  _const_spec(gc.shape), _const_spec(gs.shape), _const_spec(w_out.shape)],
        out_specs=tok(d),
        compiler_params=_cparams("parallel", "parallel"), name="fourier_out",
    )(xr, xi, z, res, mod, gc, gs, w_out)


def _proj_conv_kernel(xp_ref, xm_ref, xn_ref, g_ref, mod_ref, w_ref, abp_ref, cw_ref, ml_ref, mr_ref,
                      q_ref, k_ref, v_ref, z_ref, gb_ref, *, d, cols, tr, rp, nheads, dk, dv, cb):
    i = pl.program_id(1)
    last = pl.num_programs(1) - 1
    n_prev = (_norm_mod(xp_ref[0], g_ref, mod_ref, d) * (i > 0).astype(F32)).astype(BF16)
    n_main = _norm_mod(xm_ref[0], g_ref, mod_ref, d).astype(BF16)
    n_next = (_norm_mod(xn_ref[0], g_ref, mod_ref, d) * (i < last).astype(F32)).astype(BF16)
    ne = jnp.concatenate([n_prev, n_main, n_next], axis=0)
    qk_w = nheads * dk
    v_w = nheads * dv
    z0 = 2 * qk_w + v_w
    n_in = (tr + 2) * cols // rp
    n_out = tr * cols // rp
    need = (rp + 2 * cols + rp - 1) // rp

    def project(c0, p):
        return jnp.dot(ne[p * rp:(p + 1) * rp], w_ref[:, c0:c0 + cb], preferred_element_type=F32)

    pieces = [project(0, p) for p in range(n_in)]
    zstep = v_w * cb // z0
    for c0 in range(0, z0, cb):
        nxt = []
        for j in range(n_out):
            if c0 + cb < z0:
                nxt += [project(c0 + cb, p) for p in range(len(nxt), n_in if j == n_out - 1 else j + 1)]
            if j == n_out // 2:
                zs = c0 // cb * zstep
                z_ref[0, :, zs:zs + zstep] = jnp.dot(
                    n_main, w_ref[:, z0 + zs:z0 + zs + zstep], preferred_element_type=F32).astype(z_ref.dtype)
            e = jnp.concatenate(pieces[j:j + need], axis=0)
            part = []
            for dc in range(3):
                acc = None
                for dr in range(3):
                    term = e[dr * cols:dr * cols + rp] * cw_ref[dr, dc:dc + 1, c0:c0 + cb]
                    acc = term if acc is None else acc + term
                part.append(acc)
            y = _silu(pltpu.roll(part[0], 1, axis=0) * ml_ref[...] + part[1]
                      + pltpu.roll(part[2], rp - 1, axis=0) * mr_ref[...])
            rows = slice(j * rp, (j + 1) * rp)
            if c0 < 2 * qk_w:
                o_ref, scale, base = (q_ref, dk ** -0.5, c0) if c0 < qk_w else (k_ref, 1.0, c0 - qk_w)
                for h in range(cb // dk):
                    yh = y[:, h * dk:(h + 1) * dk]
                    yh = yh * lax.rsqrt(jnp.sum(yh * yh, axis=-1, keepdims=True) + EPS) * scale
                    o_ref[0, base // dk + h, rows, :] = yh.astype(o_ref.dtype)
            else:
                base = c0 - 2 * qk_w
                for h in range(cb // dv):
                    v_ref[0, base // dv + h, rows, :] = y[:, h * dv:(h + 1) * dv].astype(v_ref.dtype)
        pieces = nxt
    r = jnp.dot(n_main, w_ref[:, z0 + v_w:z0 + v_w + GB_LANES], preferred_element_type=F32)
    t = r + abp_ref[1:2, :]
    softplus = jnp.maximum(t, 0.0) + jnp.log(1.0 + jnp.exp(-jnp.abs(t)))
    lane = lax.broadcasted_iota(jnp.int32, r.shape, 1)
    gb_ref[0] = jnp.where(lane < 2 * nheads, (-LOG2E) * jnp.exp(abp_ref[0:1, :]) * softplus, jax.nn.sigmoid(r))


def _proj_conv(x, g, mod, w, abp, conv_w, rows, cols, cb=512):
    b, l, d = x.shape
    tr = min(rows, max(1, 512 // cols))
    n = tr * cols
    rp = cols * min(tr, 2)
    col = np.arange(rp) % cols
    ml = jnp.asarray(np.broadcast_to((col != 0)[:, None], (rp, cb)), F32)
    mr = jnp.asarray(np.broadcast_to((col != cols - 1)[:, None], (rp, cb)), F32)
    heads = lambda wd: pl.BlockSpec((1, DN_HEADS, n, wd), lambda bi, i: (bi, 0, i, 0))
    tok = lambda wd: pl.BlockSpec((1, n, wd), lambda bi, i: (bi, i, 0))
    return pl.pallas_call(
        functools.partial(_proj_conv_kernel, d=d, cols=cols, tr=tr, rp=rp, nheads=DN_HEADS, dk=DN_DK, dv=DN_DV,
                          cb=cb),
        out_shape=[jax.ShapeDtypeStruct((b, DN_HEADS, l, DN_DK), BF16),
                   jax.ShapeDtypeStruct((b, DN_HEADS, l, DN_DK), BF16),
                   jax.ShapeDtypeStruct((b, DN_HEADS, l, DN_DV), BF16),
                   jax.ShapeDtypeStruct((b, l, DN_HEADS * DN_DV), BF16),
                   jax.ShapeDtypeStruct((b, l, GB_LANES), F32)],
        grid=(b, rows // tr),
        in_specs=[pl.BlockSpec((1, cols, d), lambda bi, i: (bi, jnp.maximum(i * tr - 1, 0), 0)),
                  pl.BlockSpec((1, n, d), lambda bi, i: (bi, i, 0)),
                  pl.BlockSpec((1, cols, d), lambda bi, i: (bi, jnp.minimum((i + 1) * tr, rows - 1), 0)),
                  _const_spec((1, d)),
                  pl.BlockSpec((1, 1, 3 * d), _mod_map(mod)),
                  _const_spec(w.shape), _const_spec(abp.shape), _const_spec(conv_w.shape),
                  _const_spec((rp, cb)), _const_spec((rp, cb))],
        out_specs=[heads(DN_DK), heads(DN_DK), heads(DN_DV), tok(DN_HEADS * DN_DV), tok(GB_LANES)],
        compiler_params=_cparams("parallel", "parallel"), name="proj_conv",
    )(x, x, x, g.reshape(1, d), mod, w, abp, conv_w, ml, mr)


def _delta_kernel(qa_ref, ka_ref, va_ref, ga_ref, qb_ref, kb_ref, vb_ref, gb_ref, s0_ref,
                  of_ref, ob_ref, sfin_ref, s_scr, sb_scr, *, nheads, cs):
    n = pl.program_id(1)

    @pl.when(n == 0)
    def _():
        s_scr[...] = s0_ref[0]
        sb_scr[...] = s0_ref[0].astype(sb_scr.dtype)

    ri = lax.broadcasted_iota(jnp.int32, (cs, cs), 0)
    ci = lax.broadcasted_iota(jnp.int32, (cs, cs), 1)
    eye = ri == ci
    xor = ri ^ ci
    nt = (((1,), (1,)), ((), ()))
    levels = int(math.log2(cs))
    chains = []
    for d, (q_ref, k_ref, v_ref, g_ref, o_ref) in enumerate(((qa_ref, ka_ref, va_ref, ga_ref, of_ref),
                                                              (qb_ref, kb_ref, vb_ref, gb_ref, ob_ref))):
        incl = (ri >= ci) if d == 0 else (ri <= ci)
        strict = (ri > ci) if d == 0 else (ri < ci)
        gbv = g_ref[0]
        hi = gbv.astype(BF16)
        rem = gbv - hi.astype(F32)
        mid = rem.astype(BF16)
        t01 = jnp.where(incl, 1.0, 0.0).astype(BF16)
        cum = sum(jnp.dot(t01, p, preferred_element_type=F32)
                  for p in (hi, mid, (rem - mid.astype(F32)).astype(BF16)))
        cum_t = cum.T
        last = cs - 1 if d == 0 else 0
        for hd in range(nheads):
            qb = q_ref[0, hd]
            kb = k_ref[0, hd]
            k = kb.astype(F32)
            v = v_ref[0, hd].astype(F32)
            kk = lax.dot_general(kb, kb, nt, preferred_element_type=F32)
            qk = lax.dot_general(qb, kb, nt, preferred_element_type=F32)
            lg = d * nheads + hd
            lb = (2 + d) * nheads + hd
            gc_col = cum[:, lg:lg + 1]
            gc_row = cum_t[lg:lg + 1, :]
            g_tot = cum[last:last + 1, lg:lg + 1]
            b_col = gbv[:, lb:lb + 1]
            ex = jnp.exp2(gc_col - gc_row)
            egc = jnp.exp2(gc_col)
            chains.append(dict(
                hd=hd, d=d, o_ref=o_ref, eg=jnp.exp2(g_tot),
                qh=(qb.astype(F32) * egc).astype(BF16),
                qkd=jnp.where(incl, qk * ex, 0.0).astype(BF16),
                ktt=(k.T * jnp.exp2(g_tot - gc_row)).astype(BF16),
                rhs=jnp.concatenate([v * b_col, k * (b_col * egc)], axis=1).astype(BF16),
                neg_off=jnp.where(strict, kk * (-b_col) * ex, 0.0)))
    dmats = [jnp.where(eye, 1.0, jnp.where(xor == 1, ch["neg_off"], 0.0)) for ch in chains]
    neg_offs = [ch["neg_off"].astype(BF16) for ch in chains]
    for lv in range(1, levels):
        sel = jnp.where((xor >> lv) == 1, 1.0, 0.0).astype(BF16)
        dmbs = [dm.astype(BF16) for dm in dmats]
        des = [jnp.dot(db, no * sel, preferred_element_type=F32) for db, no in zip(dmbs, neg_offs)]
        dmats = [dm + jnp.dot(de.astype(BF16), db, preferred_element_type=F32)
                 for dm, de, db in zip(dmats, des, dmbs)]
    sols = [jnp.dot(dm.astype(BF16), ch["rhs"], preferred_element_type=F32) for dm, ch in zip(dmats, chains)]
    dv = of_ref.shape[-1]
    r1s = [jnp.dot(jnp.concatenate([sol[:, dv:].astype(BF16), ch["qh"]], axis=0), sb_scr[ch["hd"], ch["d"]],
                   preferred_element_type=F32) for ch, sol in zip(chains, sols)]
    vns = [(sol[:, :dv] - r1[:cs]).astype(BF16) for sol, r1 in zip(sols, r1s)]
    r2s = [jnp.dot(jnp.concatenate([ch["qkd"], ch["ktt"]], axis=0), vn, preferred_element_type=F32)
           for ch, vn in zip(chains, vns)]
    for ch, r1, r2 in zip(chains, r1s, r2s):
        hd, d = ch["hd"], ch["d"]
        ch["o_ref"][0, hd] = (r1[cs:] + r2[:cs]).astype(of_ref.dtype)
        s_new = s_scr[hd, d] * ch["eg"] + r2[cs:]
        s_scr[hd, d] = s_new
        sb_scr[hd, d] = s_new.astype(sb_scr.dtype)

    @pl.when(n == pl.num_programs(1) - 1)
    def _():
        sfin_ref[0] = s_scr[...]


def _delta(q, k, v, gb, s0, cs=SCAN_CHUNK):
    b, nh, l, dk = q.shape
    dv = v.shape[-1]
    nc = l // cs
    fwd = lambda wd: pl.BlockSpec((1, nh, cs, wd), lambda bi, n: (bi, 0, n, 0))
    bwd = lambda wd: pl.BlockSpec((1, nh, cs, wd), lambda bi, n: (bi, 0, nc - 1 - n, 0))
    st = pl.BlockSpec((1, nh, 2, dk, dv), lambda bi, n: (bi, 0, 0, 0, 0))
    return pl.pallas_call(
        functools.partial(_delta_kernel, nheads=nh, cs=cs),
        out_shape=[jax.ShapeDtypeStruct((b, nh, l, dv), BF16),
                   jax.ShapeDtypeStruct((b, nh, l, dv), BF16),
                   jax.ShapeDtypeStruct((b, nh, 2, dk, dv), F32)],
        grid=(b, nc),
        in_specs=[fwd(dk), fwd(dk), fwd(dv), pl.BlockSpec((1, cs, GB_LANES), lambda bi, n: (bi, n, 0)),
                  bwd(dk), bwd(dk), bwd(dv), pl.BlockSpec((1, cs, GB_LANES), lambda bi, n: (bi, nc - 1 - n, 0)),
                  st],
        out_specs=[fwd(dv), bwd(dv), st],
        scratch_shapes=[pltpu.VMEM((nh, 2, dk, dv), F32), pltpu.VMEM((nh, 2, dk, dv), BF16)],
        compiler_params=_cparams("parallel", "arbitrary"), name="delta_rule",
    )(q, k, v, gb, q, k, v, gb, s0)


def _dn_out_kernel(of_ref, ob_ref, z_ref, res_ref, mod_ref, ng_ref, w_ref, fg_ref, o_ref, *, nheads, dv, d):
    acc = None
    for hd in range(nheads):
        sl = slice(hd * dv, (hd + 1) * dv)
        o = of_ref[0, hd].astype(F32) + ob_ref[0, hd].astype(F32)
        o = o * lax.rsqrt(jnp.mean(o * o, axis=-1, keepdims=True) + EPS) * ng_ref[...]
        y = (o * _silu(z_ref[0, :, sl].astype(F32))).astype(BF16)
        p = jnp.dot(y, w_ref[sl, :], preferred_element_type=F32)
        acc = p if acc is None else acc + p
    gate = mod_ref[0, :, 2 * d:3 * d]
    hn = res_ref[0] + gate * acc
    o_ref[0] = hn * lax.rsqrt(jnp.mean(hn * hn, axis=-1, keepdims=True) + EPS) * fg_ref[...]


def _dn_out(o_f, o_b, z, res, mod, norm_g, w_out, final_g, tm=512):
    b, nh, l, dv = o_f.shape
    d = res.shape[-1]
    tm = min(tm, l)
    tok = lambda wd: pl.BlockSpec((1, tm, wd), lambda i, j: (i, j, 0))
    hs = pl.BlockSpec((1, nh, tm, dv), lambda i, j: (i, 0, j, 0))
    return pl.pallas_call(
        functools.partial(_dn_out_kernel, nheads=nh, dv=dv, d=d),
        out_shape=jax.ShapeDtypeStruct((b, l, d), F32),
        grid=(b, l // tm),
        in_specs=[hs, hs, tok(nh * dv), tok(d),
                  pl.BlockSpec((1, 1, 3 * d), lambda i, j: (i, 0, 0)),
                  _const_spec((1, dv)), _const_spec(w_out.shape), _const_spec((1, d))],
        out_specs=tok(d),
        compiler_params=_cparams("parallel", "parallel"), name="delta_out",
    )(o_f, o_b, z, res, mod, norm_g.reshape(1, dv), w_out, final_g.reshape(1, d))


def _fourier_layer(h_lat, h_ctx, norm_g, mod_lat, mod_ctx, w_in, w_grp, w_out):
    b, l, _ = h_lat.shape
    lc = h_ctx.shape[1]
    width = w_in.shape[1] // 2
    gd = w_grp.shape[-1]
    w_in = w_in.astype(BF16)
    w_out = w_out.astype(BF16)
    gc, gs = _grp(w_grp)

    u, z = _nmm(h_lat, norm_g, mod_lat, w_in, (width, width), (BF16, BF16))
    n2 = DFT_N2
    n1 = l // n2
    y = _dft1(u, n1, n2)
    xr, xi = _dft2_lat(_dft_complex_matrix(n2), y)
    xr = xr.reshape(b, l, width)
    xi = xi.reshape(b, l, width)
    out_lat = _fn_out(xr, xi, z, h_lat, mod_lat, gc, gs, w_out, 1.0 / math.sqrt(l * gd))

    uc, zc = _nmm(h_ctx, norm_g, mod_ctx, w_in, (width, width), (BF16, BF16))
    xrc, xic = _dft2(_dft_real_matrix(lc), uc)
    out_ctx = _fn_out(xrc, xic, zc, h_ctx, mod_ctx, gc, gs, w_out, 1.0 / math.sqrt(lc * gd))
    return out_lat, out_ctx


def _delta_layer_final(h_lat, h_ctx, norm_g, mod_lat, mod_ctx, w_in, conv_w, a_log, dt_bias, dn_norm_g, w_out,
                       final_g):
    b, l, d = h_lat.shape
    lc = h_ctx.shape[1]
    main_w = 2 * DN_HEADS * DN_DK + 2 * DN_HEADS * DN_DV
    n_ab = 4 * DN_HEADS
    w_cat = jnp.concatenate(
        [w_in[:, :main_w], w_in[:, main_w:], jnp.zeros((d, GB_LANES - n_ab), w_in.dtype)], axis=1).astype(BF16)
    pad = jnp.zeros((GB_LANES - 2 * DN_HEADS,), F32)
    abp = jnp.stack([jnp.concatenate([a_log.reshape(-1).astype(F32), pad]),
                     jnp.concatenate([dt_bias.reshape(-1).astype(F32), pad])])
    qc, kc, vc, _, gbc = _proj_conv(h_ctx, norm_g, mod_ctx, w_cat, abp, conv_w, 1, lc)
    ql, kl, vl, zl, gbl = _proj_conv(h_lat, norm_g, mod_lat, w_cat, abp, conv_w, l // GRID_W, GRID_W)
    s0 = jnp.zeros((b, DN_HEADS, 2, DN_DK, DN_DV), F32)
    _, _, s_ctx = _delta(qc, kc, vc, gbc, s0)
    o_f, o_b, _ = _delta(ql, kl, vl, gbl, s_ctx)
    return _dn_out(o_f, o_b, zl, h_lat, mod_lat, dn_norm_g, w_out.astype(BF16), final_g)


def kernel(x, c, ctx, c_ctx, mod_w, mod_b, norm_g, final_g, fn_w_in, fn_w_grp, fn_w_out,
           dn_w_in, dn_conv, dn_a_log, dn_dt_bias, dn_norm_g, dn_w_out):
    b, l, d = x.shape
    cond = jnp.concatenate([c, c_ctx[None, :], jnp.zeros((8 - b - 1, d), c.dtype)], axis=0)
    mods = _mods(cond, mod_w, mod_b)
    mod_lat = [mods[i, :b].reshape(b, 1, 3 * d) for i in range(2)]
    mod_ctx = [mods[i, b:b + 1].reshape(1, 1, 3 * d) for i in range(2)]
    h_lat, h_ctx = _fourier_layer(x, ctx, norm_g[0], mod_lat[0], mod_ctx[0], fn_w_in[0], fn_w_grp[0], fn_w_out[0])
    return _delta_layer_final(h_lat, h_ctx, norm_g[1], mod_lat[1], mod_ctx[1], dn_w_in[0], dn_conv[0],
                              dn_a_log[0], dn_dt_bias[0], dn_norm_g[0], dn_w_out[0], final_g)
```

```python
import functools
import math

import numpy as np
import jax
import jax.numpy as jnp
from jax import lax
from jax.experimental import pallas as pl
from jax.experimental.pallas import tpu as pltpu

F32 = jnp.float32
BF16 = jnp.bfloat16
HI = lax.Precision.HIGHEST
EPS = 1e-6
LOG2E = 1.4426950408889634

GRID_W = 64
FN_GROUPS = 8
DN_HEADS = 8
DN_DK = 128
DN_DV = 256
SCAN_CHUNK = 128
DFT_N2 = 128
GB_LANES = 128
VMEM_LIMIT_BYTES = 48 * 1024 * 1024


def _cparams(*sem):
    return pltpu.CompilerParams(dimension_semantics=sem, vmem_limit_bytes=VMEM_LIMIT_BYTES)


def _const_spec(shape):
    zeros = (0,) * len(shape)
    return pl.BlockSpec(shape, lambda *_: zeros, pipeline_mode=pl.Buffered(1))


def _silu(v):
    return v * jax.nn.sigmoid(v)


def _mods_kernel(cond_ref, w_ref, b_ref, o_ref):
    s = _silu(cond_ref[...])
    o_ref[0] = jnp.dot(s, w_ref[0], preferred_element_type=F32, precision=HI) + b_ref[0]


def _mods(cond, mod_w, mod_b):
    depth, d, d3 = mod_w.shape
    rows = cond.shape[0]
    tn = 512
    return pl.pallas_call(
        _mods_kernel,
        out_shape=jax.ShapeDtypeStruct((depth, rows, d3), F32),
        grid=(depth, d3 // tn),
        in_specs=[pl.BlockSpec((rows, d), lambda i, j: (0, 0)),
                  pl.BlockSpec((1, d, tn), lambda i, j: (i, 0, j)),
                  pl.BlockSpec((1, 1, tn), lambda i, j: (i, 0, j))],
        out_specs=pl.BlockSpec((1, rows, tn), lambda i, j: (i, 0, j)),
        compiler_params=_cparams("parallel", "parallel"), name="adaln_mods",
    )(cond, mod_w, mod_b.reshape(depth, 1, d3))


def _norm_mod(x, g_ref, mod_ref, d):
    y = x * lax.rsqrt(jnp.mean(x * x, axis=-1, keepdims=True) + EPS) * g_ref[...]
    return y * (1.0 + mod_ref[0, :, d:2 * d]) + mod_ref[0, :, 0:d]


def _nmm_kernel(x_ref, g_ref, mod_ref, w_ref, *outs, splits, d):
    n = _norm_mod(x_ref[0], g_ref, mod_ref, d).astype(BF16)
    c0 = 0
    for o_ref, width in zip(outs, splits):
        step = min(width, 1024)
        for s in range(0, width, step):
            r = jnp.dot(n, w_ref[:, c0 + s:c0 + s + step], preferred_element_type=F32)
            o_ref[0, :, s:s + step] = r.astype(o_ref.dtype)
        c0 += width


def _mod_map(mod):
    return (lambda i, j: (i, 0, 0)) if mod.shape[0] > 1 else (lambda i, j: (0, 0, 0))


def _nmm(x, g, mod, w, splits, dtypes, tm=512):
    b, l, d = x.shape
    tm = min(tm, l)
    return pl.pallas_call(
        functools.partial(_nmm_kernel, splits=splits, d=d),
        out_shape=[jax.ShapeDtypeStruct((b, l, wd), dt) for wd, dt in zip(splits, dtypes)],
        grid=(b, l // tm),
        in_specs=[pl.BlockSpec((1, tm, d), lambda i, j: (i, j, 0)),
                  _const_spec((1, d)),
                  pl.BlockSpec((1, 1, 3 * d), _mod_map(mod)),
                  _const_spec(w.shape)],
        out_specs=[pl.BlockSpec((1, tm, wd), lambda i, j: (i, j, 0)) for wd in splits],
        compiler_params=_cparams("parallel", "parallel"), name="norm_proj",
    )(x, g.reshape(1, d), mod, w)


def _dft_cos_sin(n):
    k = np.arange(n)
    ang = 2.0 * np.pi * ((k[:, None] * k[None, :]) % n) / n
    return np.cos(ang), np.sin(ang)


def _dft1_kernel(m_ref, u_ref, cos_ref, sin_ref, o_ref, *, n1, tn2):
    ut = jnp.transpose(u_ref[0], (1, 0, 2))
    for j in range(tn2):
        r = jnp.dot(m_ref[...], ut[j], preferred_element_type=F32)
        yr = r[:n1]
        yi = r[n1:]
        cs = cos_ref[j]
        sn = sin_ref[j]
        o_ref[0, 0, j] = (yr * cs + yi * sn).astype(o_ref.dtype)
        o_ref[0, 1, j] = (yi * cs - yr * sn).astype(o_ref.dtype)


def _dft1(u, n1, n2, tn2=16, tc=2048):
    b, l, c = u.shape
    cm, sm = _dft_cos_sin(n1)
    m1 = jnp.asarray(np.concatenate([cm, -sm], axis=0), F32).astype(BF16)
    ang = 2.0 * np.pi * (np.arange(n2)[:, None] * np.arange(n1)[None, :]) / l
    tw_cos = jnp.asarray(np.cos(ang)[:, :, None], F32)
    tw_sin = jnp.asarray(np.sin(ang)[:, :, None], F32)
    return pl.pallas_call(
        functools.partial(_dft1_kernel, n1=n1, tn2=tn2),
        out_shape=jax.ShapeDtypeStruct((b, 2, n2, n1, c), BF16),
        grid=(b, n2 // tn2, c // tc),
        in_specs=[_const_spec((2 * n1, n1)),
                  pl.BlockSpec((1, n1, tn2, tc), lambda i, j, k: (i, 0, j, k)),
                  pl.BlockSpec((tn2, n1, 1), lambda i, j, k: (j, 0, 0)),
                  pl.BlockSpec((tn2, n1, 1), lambda i, j, k: (j, 0, 0))],
        out_specs=pl.BlockSpec((1, 2, tn2, n1, tc), lambda i, j, k: (i, 0, j, 0, k)),
        compiler_params=_cparams("parallel", "parallel", "parallel"), name="dft_stage1",
    )(m1, u.reshape(b, n1, n2, c), tw_cos, tw_sin)


def _dft2_lat_kernel(m_ref, y_ref, xr_ref, xi_ref, *, n2, tk1):
    yr = jnp.transpose(y_ref[0, 0], (1, 0, 2))
    yi = jnp.transpose(y_ref[0, 1], (1, 0, 2))
    xr, xi = [], []
    for j in range(tk1):
        r = jnp.dot(m_ref[...], jnp.concatenate([yr[j], yi[j]], axis=0), preferred_element_type=F32)
        xr.append(r[:n2].astype(xr_ref.dtype))
        xi.append(r[n2:].astype(xi_ref.dtype))
    xr_ref[0] = jnp.transpose(jnp.stack(xr), (1, 0, 2))
    xi_ref[0] = jnp.transpose(jnp.stack(xi), (1, 0, 2))


def _dft2_lat(m, y, tk1=16, tc=512):
    b, _, n2, n1, c = y.shape
    tk1 = min(tk1, n1)
    return pl.pallas_call(
        functools.partial(_dft2_lat_kernel, n2=n2, tk1=tk1),
        out_shape=[jax.ShapeDtypeStruct((b, n2, n1, c), BF16)] * 2,
        grid=(b, n1 // tk1, c // tc),
        in_specs=[_const_spec(m.shape),
                  pl.BlockSpec((1, 2, n2, tk1, tc), lambda i, j, k: (i, 0, 0, j, k))],
        out_specs=[pl.BlockSpec((1, n2, tk1, tc), lambda i, j, k: (i, 0, j, k))] * 2,
        compiler_params=_cparams("parallel", "parallel", "parallel"), name="dft_stage2_lat",
    )(m, y)


def _dft2_kernel(m_ref, y_ref, xr_ref, xi_ref, *, half):
    r = jnp.dot(m_ref[...], y_ref[0], preferred_element_type=F32)
    xr_ref[0] = r[:half].astype(xr_ref.dtype)
    xi_ref[0] = r[half:].astype(xi_ref.dtype)


def _dft2(m, y, tc=2048):
    b, k, cols = y.shape
    half = m.shape[0] // 2
    return pl.pallas_call(
        functools.partial(_dft2_kernel, half=half),
        out_shape=[jax.ShapeDtypeStruct((b, half, cols), BF16)] * 2,
        grid=(b, cols // tc),
        in_specs=[_const_spec(m.shape),
                  pl.BlockSpec((1, k, tc), lambda i, j: (i, 0, j))],
        out_specs=[pl.BlockSpec((1, half, tc), lambda i, j: (i, 0, j))] * 2,
        compiler_params=_cparams("parallel", "parallel"), name="dft_stage2",
    )(m, y)


def _dft_real_matrix(n):
    cm, sm = _dft_cos_sin(n)
    return jnp.asarray(np.concatenate([cm, -sm], axis=0), F32).astype(BF16)


def _dft_complex_matrix(n):
    cm, sm = _dft_cos_sin(n)
    return jnp.asarray(np.block([[cm, sm], [-sm, cm]]), F32).astype(BF16)


def _grp_kernel(m_ref, w_ref, g_ref):
    g_ref[0] = jnp.dot(m_ref[...], w_ref[0], preferred_element_type=F32, precision=HI).astype(g_ref.dtype)


def _grp(w_grp):
    groups, n, _ = w_grp.shape
    cm, sm = _dft_cos_sin(n)
    m = jnp.asarray(np.concatenate([cm, sm], axis=0), F32)
    return pl.pallas_call(
        _grp_kernel,
        out_shape=jax.ShapeDtypeStruct((groups, 2 * n, n), BF16),
        grid=(groups,),
        in_specs=[_const_spec((2 * n, n)),
                  pl.BlockSpec((1, n, n), lambda g: (g, 0, 0))],
        out_specs=pl.BlockSpec((1, 2 * n, n), lambda g: (g, 0, 0)),
        compiler_params=_cparams("parallel"), name="group_weights",
    )(m, w_grp)


def _fn_out_kernel(xr_ref, xi_ref, z_ref, res_ref, mod_ref, gw_ref, w_ref, o_ref, *, groups, gd, d, scale):
    acc = None
    for g in range(groups):
        sl = slice(g * gd, (g + 1) * gd)
        f = jnp.dot(jnp.concatenate([xr_ref[0, :, sl], xi_ref[0, :, sl]], axis=1), gw_ref[g],
                    preferred_element_type=F32)
        y = (f * scale * _silu(z_ref[0, :, sl].astype(F32))).astype(BF16)
        p = jnp.dot(y, w_ref[sl, :], preferred_element_type=F32)
        acc = p if acc is None else acc + p
    gate = mod_ref[0, :, 2 * d:3 * d]
    o_ref[0] = res_ref[0] + gate * acc


def _fn_out(xr, xi, z, res, mod, gw, w_out, scale, tm=512):
    b, l, c = xr.shape
    d = res.shape[-1]
    tm = min(tm, l)
    groups, _, gd = gw.shape
    tok = lambda wd: pl.BlockSpec((1, tm, wd), lambda i, j: (i, j, 0))
    return pl.pallas_call(
        functools.partial(_fn_out_kernel, groups=groups, gd=gd, d=d, scale=scale),
        out_shape=jax.ShapeDtypeStruct((b, l, d), F32),
        grid=(b, l // tm),
        in_specs=[tok(c), tok(c), tok(c), tok(d),
                  pl.BlockSpec((1, 1, 3 * d), _mod_map(mod)),
                  _const_spec(gw.shape), _const_spec(w_out.shape)],
        out_specs=tok(d),
        compiler_params=_cparams("parallel", "parallel"), name="fourier_out",
    )(xr, xi, z, res, mod, gw, w_out)


def _proj_conv_kernel(xp_ref, xm_ref, xn_ref, g_ref, mod_ref, w_ref, wab_ref, abp_ref, cw_ref, ml_ref, mr_ref,
                      q_ref, k_ref, v_ref, z_ref, gb_ref, *, d, cols, tr, rp, halo, nheads, dk, dv, cb):
    i = pl.program_id(1)
    last = pl.num_programs(1) - 1
    if halo:
        n_prev = (_norm_mod(xp_ref[0], g_ref, mod_ref, d) * (i > 0).astype(F32)).astype(BF16)
    n_main = _norm_mod(xm_ref[0], g_ref, mod_ref, d).astype(BF16)
    if halo:
        n_next = (_norm_mod(xn_ref[0], g_ref, mod_ref, d) * (i < last).astype(F32)).astype(BF16)
        ne = jnp.concatenate([n_prev, n_main, n_next], axis=0)
    qk_w = nheads * dk
    v_w = nheads * dv
    z0 = 2 * qk_w + v_w
    n_in = (tr + 2) * cols // rp
    n_out = tr * cols // rp
    need = (rp + 2 * cols + rp - 1) // rp

    def project(c0, p):
        if halo:
            rows = ne[p * rp:(p + 1) * rp]
        elif 0 < p < n_in - 1:
            rows = n_main[(p - 1) * rp:p * rp]
        else:
            return jnp.zeros((rp, cb), F32)
        return jnp.dot(rows, w_ref[:, c0:c0 + cb], preferred_element_type=F32)

    pieces = [project(0, p) for p in range(n_in)]
    zstep = v_w * cb // z0
    for c0 in range(0, z0, cb):
        nxt = []
        for j in range(n_out):
            if c0 + cb < z0:
                nxt += [project(c0 + cb, p) for p in range(len(nxt), n_in if j == n_out - 1 else j + 1)]
            if j == n_out // 2:
                zs = c0 // cb * zstep
                z_ref[0, :, zs:zs + zstep] = jnp.dot(
                    n_main, w_ref[:, z0 + zs:z0 + zs + zstep], preferred_element_type=F32).astype(z_ref.dtype)
            e = jnp.concatenate(pieces[j:j + need], axis=0)
            part = []
            for dc in range(3):
                acc = None
                for dr in range(3):
                    term = e[dr * cols:dr * cols + rp] * cw_ref[dr, dc:dc + 1, c0:c0 + cb]
                    acc = term if acc is None else acc + term
                part.append(acc)
            y = _silu(pltpu.roll(part[0], 1, axis=0) * ml_ref[...] + part[1]
                      + pltpu.roll(part[2], rp - 1, axis=0) * mr_ref[...])
            rows = slice(j * rp, (j + 1) * rp)
            if c0 < 2 * qk_w:
                o_ref, scale, base = (q_ref, dk ** -0.5, c0) if c0 < qk_w else (k_ref, 1.0, c0 - qk_w)
                for h in range(cb // dk):
                    yh = y[:, h * dk:(h + 1) * dk]
                    yh = yh * lax.rsqrt(jnp.sum(yh * yh, axis=-1, keepdims=True) + EPS) * scale
                    o_ref[0, base // dk + h, rows, :] = yh.astype(o_ref.dtype)
            else:
                base = c0 - 2 * qk_w
                for h in range(cb // dv):
                    v_ref[0, base // dv + h, rows, :] = y[:, h * dv:(h + 1) * dv].astype(v_ref.dtype)
        pieces = nxt
    r = jnp.dot(n_main, wab_ref[...], preferred_element_type=F32)
    t = r + abp_ref[1:2, :]
    softplus = jnp.maximum(t, 0.0) + jnp.log(1.0 + jnp.exp(-jnp.abs(t)))
    lane = lax.broadcasted_iota(jnp.int32, r.shape, 1)
    gb_ref[0] = jnp.where(lane < 2 * nheads, (-LOG2E) * jnp.exp(abp_ref[0:1, :]) * softplus, jax.nn.sigmoid(r))


def _proj_conv(x, g, mod, w, w_ab, abp, conv_w, rows, cols, cb=512):
    b, l, d = x.shape
    tr = min(rows, max(1, 512 // cols))
    n = tr * cols
    rp = cols * min(tr, 2)
    col = np.arange(rp) % cols
    ml = jnp.asarray(np.broadcast_to((col != 0)[:, None], (rp, cb)), F32)
    mr = jnp.asarray(np.broadcast_to((col != cols - 1)[:, None], (rp, cb)), F32)
    heads = lambda wd: pl.BlockSpec((1, DN_HEADS, n, wd), lambda bi, i: (bi, 0, i, 0))
    tok = lambda wd: pl.BlockSpec((1, n, wd), lambda bi, i: (bi, i, 0))
    return pl.pallas_call(
        functools.partial(_proj_conv_kernel, d=d, cols=cols, tr=tr, rp=rp, halo=rows > tr or rp != cols,
                          nheads=DN_HEADS, dk=DN_DK, dv=DN_DV, cb=cb),
        out_shape=[jax.ShapeDtypeStruct((b, DN_HEADS, l, DN_DK), BF16),
                   jax.ShapeDtypeStruct((b, DN_HEADS, l, DN_DK), BF16),
                   jax.ShapeDtypeStruct((b, DN_HEADS, l, DN_DV), BF16),
                   jax.ShapeDtypeStruct((b, l, DN_HEADS * DN_DV), BF16),
                   jax.ShapeDtypeStruct((b, l, GB_LANES), F32)],
        grid=(b, rows // tr),
        in_specs=[pl.BlockSpec((1, cols, d), lambda bi, i: (bi, jnp.maximum(i * tr - 1, 0), 0)),
                  pl.BlockSpec((1, n, d), lambda bi, i: (bi, i, 0)),
                  pl.BlockSpec((1, cols, d), lambda bi, i: (bi, jnp.minimum((i + 1) * tr, rows - 1), 0)),
                  _const_spec((1, d)),
                  pl.BlockSpec((1, 1, 3 * d), _mod_map(mod)),
                  _const_spec(w.shape), _const_spec(w_ab.shape), _const_spec(abp.shape), _const_spec(conv_w.shape),
                  _const_spec((rp, cb)), _const_spec((rp, cb))],
        out_specs=[heads(DN_DK), heads(DN_DK), heads(DN_DV), tok(DN_HEADS * DN_DV), tok(GB_LANES)],
        compiler_params=_cparams("parallel", "parallel"), name="proj_conv",
    )(x, x, x, g.reshape(1, d), mod, w, w_ab, abp, conv_w, ml, mr)


def _delta_kernel(qa_ref, ka_ref, va_ref, ga_ref, qb_ref, kb_ref, vb_ref, gb_ref, s0_ref,
                  of_ref, ob_ref, sfin_ref, s_scr, sb_scr, *, nheads, cs):
    n = pl.program_id(1)

    @pl.when(n == 0)
    def _():
        s_scr[...] = s0_ref[0]
        sb_scr[...] = s0_ref[0].astype(sb_scr.dtype)

    ri = lax.broadcasted_iota(jnp.int32, (cs, cs), 0)
    ci = lax.broadcasted_iota(jnp.int32, (cs, cs), 1)
    eye = ri == ci
    xor = ri ^ ci
    nt = (((1,), (1,)), ((), ()))
    levels = int(math.log2(cs))
    chains = []
    for d, (q_ref, k_ref, v_ref, g_ref, o_ref) in enumerate(((qa_ref, ka_ref, va_ref, ga_ref, of_ref),
                                                              (qb_ref, kb_ref, vb_ref, gb_ref, ob_ref))):
        incl = (ri >= ci) if d == 0 else (ri <= ci)
        strict = (ri > ci) if d == 0 else (ri < ci)
        gbv = g_ref[0]
        hi = gbv.astype(BF16)
        rem = gbv - hi.astype(F32)
        mid = rem.astype(BF16)
        t01 = jnp.where(incl, 1.0, 0.0).astype(BF16)
        cum = sum(jnp.dot(t01, p, preferred_element_type=F32)
                  for p in (hi, mid, (rem - mid.astype(F32)).astype(BF16)))
        cum_t = cum.T
        last = cs - 1 if d == 0 else 0
        for hd in range(nheads):
            qb = q_ref[0, hd]
            kb = k_ref[0, hd]
            k = kb.astype(F32)
            v = v_ref[0, hd].astype(F32)
            kk = lax.dot_general(kb, kb, nt, preferred_element_type=F32)
            qk = lax.dot_general(qb, kb, nt, preferred_element_type=F32)
            lg = d * nheads + hd
            lb = (2 + d) * nheads + hd
            gc_col = cum[:, lg:lg + 1]
            gc_row = cum_t[lg:lg + 1, :]
            g_tot = cum[last:last + 1, lg:lg + 1]
            b_col = gbv[:, lb:lb + 1]
            ex = jnp.exp2(gc_col - gc_row)
            egc = jnp.exp2(gc_col)
            chains.append(dict(
                hd=hd, d=d, o_ref=o_ref, eg=jnp.exp2(g_tot),
                qh=(qb.astype(F32) * egc).astype(BF16),
                qkd=jnp.where(incl, qk * ex, 0.0).astype(BF16),
                ktt=(k.T * jnp.exp2(g_tot - gc_row)).astype(BF16),
                rhs=jnp.concatenate([v * b_col, k * (b_col * egc)], axis=1).astype(BF16),
                neg_off=jnp.where(strict, kk * (-b_col) * ex, 0.0)))
    dmats = [jnp.where(eye, 1.0, jnp.where(xor == 1, ch["neg_off"], 0.0)) for ch in chains]
    neg_offs = [ch["neg_off"].astype(BF16) for ch in chains]
    for lv in range(1, levels):
        sel = jnp.where((xor >> lv) == 1, 1.0, 0.0).astype(BF16)
        dmbs = [dm.astype(BF16) for dm in dmats]
        des = [jnp.dot(db, no * sel, preferred_element_type=F32) for db, no in zip(dmbs, neg_offs)]
        dmats = [dm + jnp.dot(de.astype(BF16), db, preferred_element_type=F32)
                 for dm, de, db in zip(dmats, des, dmbs)]
    sols = [jnp.dot(dm.astype(BF16), ch["rhs"], preferred_element_type=F32) for dm, ch in zip(dmats, chains)]
    dv = of_ref.shape[-1]
    r1s = [jnp.dot(jnp.concatenate([sol[:, dv:].astype(BF16), ch["qh"]], axis=0), sb_scr[ch["hd"], ch["d"]],
                   preferred_element_type=F32) for ch, sol in zip(chains, sols)]
    vns = [(sol[:, :dv] - r1[:cs]).astype(BF16) for sol, r1 in zip(sols, r1s)]
    r2s = [jnp.dot(jnp.concatenate([ch["qkd"], ch["ktt"]], axis=0), vn, preferred_element_type=F32)
           for ch, vn in zip(chains, vns)]
    for ch, r1, r2 in zip(chains, r1s, r2s):
        hd, d = ch["hd"], ch["d"]
        ch["o_ref"][0, hd] = (r1[cs:] + r2[:cs]).astype(of_ref.dtype)
        s_new = s_scr[hd, d] * ch["eg"] + r2[cs:]
        s_scr[hd, d] = s_new
        sb_scr[hd, d] = s_new.astype(sb_scr.dtype)

    @pl.when(n == pl.num_programs(1) - 1)
    def _():
        sfin_ref[0] = s_scr[...]


def _delta(q, k, v, gb, s0, cs=SCAN_CHUNK):
    b, nh, l, dk = q.shape
    dv = v.shape[-1]
    nc = l // cs
    fwd = lambda wd: pl.BlockSpec((1, nh, cs, wd), lambda bi, n: (bi, 0, n, 0))
    bwd = lambda wd: pl.BlockSpec((1, nh, cs, wd), lambda bi, n: (bi, 0, nc - 1 - n, 0))
    st = pl.BlockSpec((1, nh, 2, dk, dv), lambda bi, n: (bi, 0, 0, 0, 0))
    return pl.pallas_call(
        functools.partial(_delta_kernel, nheads=nh, cs=cs),
        out_shape=[jax.ShapeDtypeStruct((b, nh, l, dv), BF16),
                   jax.ShapeDtypeStruct((b, nh, l, dv), BF16),
                   jax.ShapeDtypeStruct((b, nh, 2, dk, dv), F32)],
        grid=(b, nc),
        in_specs=[fwd(dk), fwd(dk), fwd(dv), pl.BlockSpec((1, cs, GB_LANES), lambda bi, n: (bi, n, 0)),
                  bwd(dk), bwd(dk), bwd(dv), pl.BlockSpec((1, cs, GB_LANES), lambda bi, n: (bi, nc - 1 - n, 0)),
                  st],
        out_specs=[fwd(dv), bwd(dv), st],
        scratch_shapes=[pltpu.VMEM((nh, 2, dk, dv), F32), pltpu.VMEM((nh, 2, dk, dv), BF16)],
        compiler_params=_cparams("parallel", "arbitrary"), name="delta_rule",
    )(q, k, v, gb, q, k, v, gb, s0)


def _dn_out_kernel(of_ref, ob_ref, z_ref, res_ref, mod_ref, ng_ref, w_ref, fg_ref, o_ref, *, nheads, dv, d):
    acc = None
    for hd in range(nheads):
        sl = slice(hd * dv, (hd + 1) * dv)
        o = of_ref[0, hd].astype(F32) + ob_ref[0, hd].astype(F32)
        o = o * lax.rsqrt(jnp.mean(o * o, axis=-1, keepdims=True) + EPS) * ng_ref[...]
        y = (o * _silu(z_ref[0, :, sl].astype(F32))).astype(BF16)
        p = jnp.dot(y, w_ref[sl, :], preferred_element_type=F32)
        acc = p if acc is None else acc + p
    gate = mod_ref[0, :, 2 * d:3 * d]
    hn = res_ref[0] + gate * acc
    o_ref[0] = hn * lax.rsqrt(jnp.mean(hn * hn, axis=-1, keepdims=True) + EPS) * fg_ref[...]


def _dn_out(o_f, o_b, z, res, mod, norm_g, w_out, final_g, tm=512):
    b, nh, l, dv = o_f.shape
    d = res.shape[-1]
    tm = min(tm, l)
    tok = lambda wd: pl.BlockSpec((1, tm, wd), lambda i, j: (i, j, 0))
    hs = pl.BlockSpec((1, nh, tm, dv), lambda i, j: (i, 0, j, 0))
    return pl.pallas_call(
        functools.partial(_dn_out_kernel, nheads=nh, dv=dv, d=d),
        out_shape=jax.ShapeDtypeStruct((b, l, d), F32),
        grid=(b, l // tm),
        in_specs=[hs, hs, tok(nh * dv), tok(d),
                  pl.BlockSpec((1, 1, 3 * d), lambda i, j: (i, 0, 0)),
                  _const_spec((1, dv)), _const_spec(w_out.shape), _const_spec((1, d))],
        out_specs=tok(d),
        compiler_params=_cparams("parallel", "parallel"), name="delta_out",
    )(o_f, o_b, z, res, mod, norm_g.reshape(1, dv), w_out, final_g.reshape(1, d))


def _fourier_layer(h_lat, h_ctx, norm_g, mod_lat, mod_ctx, w_in, w_grp, w_out):
    b, l, _ = h_lat.shape
    lc = h_ctx.shape[1]
    width = w_in.shape[1] // 2
    gd = w_grp.shape[-1]
    w_in = w_in.astype(BF16)
    w_out = w_out.astype(BF16)
    gw = _grp(w_grp)

    u, z = _nmm(h_lat, norm_g, mod_lat, w_in, (width, width), (BF16, BF16))
    n2 = DFT_N2
    n1 = l // n2
    y = _dft1(u, n1, n2)
    xr, xi = _dft2_lat(_dft_complex_matrix(n2), y)
    xr = xr.reshape(b, l, width)
    xi = xi.reshape(b, l, width)
    out_lat = _fn_out(xr, xi, z, h_lat, mod_lat, gw, w_out, 1.0 / math.sqrt(l * gd))

    uc, zc = _nmm(h_ctx, norm_g, mod_ctx, w_in, (width, width), (BF16, BF16))
    xrc, xic = _dft2(_dft_real_matrix(lc), uc)
    out_ctx = _fn_out(xrc, xic, zc, h_ctx, mod_ctx, gw, w_out, 1.0 / math.sqrt(lc * gd))
    return out_lat, out_ctx


def _delta_layer_final(h_lat, h_ctx, norm_g, mod_lat, mod_ctx, w_in, conv_w, a_log, dt_bias, dn_norm_g, w_out,
                       final_g):
    b, l, d = h_lat.shape
    lc = h_ctx.shape[1]
    main_w = 2 * DN_HEADS * DN_DK + 2 * DN_HEADS * DN_DV
    n_ab = 4 * DN_HEADS
    w_main = w_in.astype(BF16)
    w_ab = jnp.pad(w_in[:, main_w:], ((0, 0), (0, GB_LANES - n_ab))).astype(BF16)
    pad = jnp.zeros((GB_LANES - 2 * DN_HEADS,), F32)
    abp = jnp.stack([jnp.concatenate([a_log.reshape(-1).astype(F32), pad]),
                     jnp.concatenate([dt_bias.reshape(-1).astype(F32), pad])])
    qc, kc, vc, _, gbc = _proj_conv(h_ctx, norm_g, mod_ctx, w_main, w_ab, abp, conv_w, 1, lc)
    ql, kl, vl, zl, gbl = _proj_conv(h_lat, norm_g, mod_lat, w_main, w_ab, abp, conv_w, l // GRID_W, GRID_W)
    s0 = jnp.zeros((b, DN_HEADS, 2, DN_DK, DN_DV), F32)
    _, _, s_ctx = _delta(qc, kc, vc, gbc, s0)
    o_f, o_b, _ = _delta(ql, kl, vl, gbl, s_ctx)
    return _dn_out(o_f, o_b, zl, h_lat, mod_lat, dn_norm_g, w_out.astype(BF16), final_g)


def kernel(x, c, ctx, c_ctx, mod_w, mod_b, norm_g, final_g, fn_w_in, fn_w_grp, fn_w_out,
           dn_w_in, dn_conv, dn_a_log, dn_dt_bias, dn_norm_g, dn_w_out):
    b, l, d = x.shape
    cond = jnp.concatenate([c, c_ctx[None, :], jnp.zeros((8 - b - 1, d), c.dtype)], axis=0)
    mods = _mods(cond, mod_w, mod_b)
    mod_lat = [mods[i, :b].reshape(b, 1, 3 * d) for i in range(2)]
    mod_ctx = [mods[i, b:b + 1].reshape(1, 1, 3 * d) for i in range(2)]
    h_lat, h_ctx = _fourier_layer(x, ctx, norm_g[0], mod_lat[0], mod_ctx[0], fn_w_in[0], fn_w_grp[0], fn_w_out[0])
    return _delta_layer_final(h_lat, h_ctx, norm_g[1], mod_lat[1], mod_ctx[1], dn_w_in[0], dn_conv[0],
                              dn_a_log[0], dn_dt_bias[0], dn_norm_g[0], dn_w_out[0], final_g)
```

```python
import functools
import math

import numpy as np
import jax
import jax.numpy as jnp
from jax import lax
from jax.experimental import pallas as pl
from jax.experimental.pallas import tpu as pltpu

F32 = jnp.float32
BF16 = jnp.bfloat16
HI = lax.Precision.HIGHEST
EPS = 1e-6
LOG2E = 1.4426950408889634

GRID_W = 64
FN_GROUPS = 8
DN_HEADS = 8
DN_DK = 128
DN_DV = 256
SCAN_CHUNK = 128
DFT_N2 = 128
GB_LANES = 128
VMEM_LIMIT_BYTES = 48 * 1024 * 1024


def _cparams(*sem):
    return pltpu.CompilerParams(dimension_semantics=sem, vmem_limit_bytes=VMEM_LIMIT_BYTES)


def _const_spec(shape):
    zeros = (0,) * len(shape)
    return pl.BlockSpec(shape, lambda *_: zeros, pipeline_mode=pl.Buffered(1))


def _silu(v):
    return v * jax.nn.sigmoid(v)


def _mods_kernel(cond_ref, w_ref, b_ref, o_ref):
    s = _silu(cond_ref[...])
    o_ref[0] = jnp.dot(s, w_ref[0], preferred_element_type=F32, precision=HI) + b_ref[0]


def _mods(cond, mod_w, mod_b):
    depth, d, d3 = mod_w.shape
    rows = cond.shape[0]
    tn = 512
    return pl.pallas_call(
        _mods_kernel,
        out_shape=jax.ShapeDtypeStruct((depth, rows, d3), F32),
        grid=(depth, d3 // tn),
        in_specs=[pl.BlockSpec((rows, d), lambda i, j: (0, 0)),
                  pl.BlockSpec((1, d, tn), lambda i, j: (i, 0, j)),
                  pl.BlockSpec((1, 1, tn), lambda i, j: (i, 0, j))],
        out_specs=pl.BlockSpec((1, rows, tn), lambda i, j: (i, 0, j)),
        compiler_params=_cparams("parallel", "parallel"), name="adaln_mods",
    )(cond, mod_w, mod_b.reshape(depth, 1, d3))


def _norm_mod(x, g_ref, mod_ref, d):
    y = x * lax.rsqrt(jnp.mean(x * x, axis=-1, keepdims=True) + EPS) * g_ref[...]
    return y * (1.0 + mod_ref[0, :, d:2 * d]) + mod_ref[0, :, 0:d]


def _nmm_kernel(x_ref, g_ref, mod_ref, w_ref, *outs, splits, d):
    n = _norm_mod(x_ref[0], g_ref, mod_ref, d).astype(BF16)
    c0 = 0
    for o_ref, width in zip(outs, splits):
        step = min(width, 1024)
        for s in range(0, width, step):
            r = jnp.dot(n, w_ref[:, c0 + s:c0 + s + step], preferred_element_type=F32)
            o_ref[0, :, s:s + step] = r.astype(o_ref.dtype)
        c0 += width


def _mod_map(mod):
    return (lambda i, j: (i, 0, 0)) if mod.shape[0] > 1 else (lambda i, j: (0, 0, 0))


def _nmm(x, g, mod, w, splits, dtypes, tm=512):
    b, l, d = x.shape
    tm = min(tm, l)
    return pl.pallas_call(
        functools.partial(_nmm_kernel, splits=splits, d=d),
        out_shape=[jax.ShapeDtypeStruct((b, l, wd), dt) for wd, dt in zip(splits, dtypes)],
        grid=(b, l // tm),
        in_specs=[pl.BlockSpec((1, tm, d), lambda i, j: (i, j, 0)),
                  _const_spec((1, d)),
                  pl.BlockSpec((1, 1, 3 * d), _mod_map(mod)),
                  _const_spec(w.shape)],
        out_specs=[pl.BlockSpec((1, tm, wd), lambda i, j: (i, j, 0)) for wd in splits],
        compiler_params=_cparams("parallel", "parallel"), name="norm_proj",
    )(x, g.reshape(1, d), mod, w)


def _dft_cos_sin(n):
    k = np.arange(n)
    ang = 2.0 * np.pi * ((k[:, None] * k[None, :]) % n) / n
    return np.cos(ang), np.sin(ang)


def _dft1_kernel(m_ref, u_ref, cos_ref, sin_ref, o_ref, *, n1, tn2):
    ut = jnp.transpose(u_ref[0], (1, 0, 2))
    for j in range(tn2):
        r = jnp.dot(m_ref[...], ut[j], preferred_element_type=F32)
        yr = r[:n1]
        yi = r[n1:]
        cs = cos_ref[j]
        sn = sin_ref[j]
        o_ref[0, 0, j] = (yr * cs + yi * sn).astype(o_ref.dtype)
        o_ref[0, 1, j] = (yi * cs - yr * sn).astype(o_ref.dtype)


def _dft1(u, n1, n2, tn2=16, tc=2048):
    b, l, c = u.shape
    cm, sm = _dft_cos_sin(n1)
    m1 = jnp.asarray(np.concatenate([cm, -sm], axis=0), F32).astype(BF16)
    ang = 2.0 * np.pi * (np.arange(n2)[:, None] * np.arange(n1)[None, :]) / l
    tw_cos = jnp.asarray(np.cos(ang)[:, :, None], F32)
    tw_sin = jnp.asarray(np.sin(ang)[:, :, None], F32)
    return pl.pallas_call(
        functools.partial(_dft1_kernel, n1=n1, tn2=tn2),
        out_shape=jax.ShapeDtypeStruct((b, 2, n2, n1, c), BF16),
        grid=(b, n2 // tn2, c // tc),
        in_specs=[_const_spec((2 * n1, n1)),
                  pl.BlockSpec((1, n1, tn2, tc), lambda i, j, k: (i, 0, j, k)),
                  pl.BlockSpec((tn2, n1, 1), lambda i, j, k: (j, 0, 0)),
                  pl.BlockSpec((tn2, n1, 1), lambda i, j, k: (j, 0, 0))],
        out_specs=pl.BlockSpec((1, 2, tn2, n1, tc), lambda i, j, k: (i, 0, j, 0, k)),
        compiler_params=_cparams("parallel", "parallel", "parallel"), name="dft_stage1",
    )(m1, u.reshape(b, n1, n2, c), tw_cos, tw_sin)


def _dft2_lat_kernel(m_ref, y_ref, xr_ref, xi_ref, *, n2, tk1):
    yr = jnp.transpose(y_ref[0, 0], (1, 0, 2))
    yi = jnp.transpose(y_ref[0, 1], (1, 0, 2))
    xr, xi = [], []
    for j in range(tk1):
        r = jnp.dot(m_ref[...], jnp.concatenate([yr[j], yi[j]], axis=0), preferred_element_type=F32)
        xr.append(r[:n2].astype(xr_ref.dtype))
        xi.append(r[n2:].astype(xi_ref.dtype))
    xr_ref[0] = jnp.transpose(jnp.stack(xr), (1, 0, 2))
    xi_ref[0] = jnp.transpose(jnp.stack(xi), (1, 0, 2))


def _dft2_lat(m, y, tk1=16, tc=512):
    b, _, n2, n1, c = y.shape
    tk1 = min(tk1, n1)
    return pl.pallas_call(
        functools.partial(_dft2_lat_kernel, n2=n2, tk1=tk1),
        out_shape=[jax.ShapeDtypeStruct((b, n2, n1, c), BF16)] * 2,
        grid=(b, n1 // tk1, c // tc),
        in_specs=[_const_spec(m.shape),
                  pl.BlockSpec((1, 2, n2, tk1, tc), lambda i, j, k: (i, 0, 0, j, k))],
        out_specs=[pl.BlockSpec((1, n2, tk1, tc), lambda i, j, k: (i, 0, j, k))] * 2,
        compiler_params=_cparams("parallel", "parallel", "parallel"), name="dft_stage2_lat",
    )(m, y)


def _dft2_kernel(m_ref, y_ref, xr_ref, xi_ref, *, half):
    r = jnp.dot(m_ref[...], y_ref[0], preferred_element_type=F32)
    xr_ref[0] = r[:half].astype(xr_ref.dtype)
    xi_ref[0] = r[half:].astype(xi_ref.dtype)


def _dft2(m, y, tc=2048):
    b, k, cols = y.shape
    half = m.shape[0] // 2
    return pl.pallas_call(
        functools.partial(_dft2_kernel, half=half),
        out_shape=[jax.ShapeDtypeStruct((b, half, cols), BF16)] * 2,
        grid=(b, cols // tc),
        in_specs=[_const_spec(m.shape),
                  pl.BlockSpec((1, k, tc), lambda i, j: (i, 0, j))],
        out_specs=[pl.BlockSpec((1, half, tc), lambda i, j: (i, 0, j))] * 2,
        compiler_params=_cparams("parallel", "parallel"), name="dft_stage2",
    )(m, y)


def _dft_real_matrix(n):
    cm, sm = _dft_cos_sin(n)
    return jnp.asarray(np.concatenate([cm, -sm], axis=0), F32).astype(BF16)


def _dft_complex_matrix(n):
    cm, sm = _dft_cos_sin(n)
    return jnp.asarray(np.block([[cm, sm], [-sm, cm]]), F32).astype(BF16)


def _grp_kernel(m_ref, w_ref, g_ref):
    g_ref[0] = jnp.dot(m_ref[...], w_ref[0], preferred_element_type=F32, precision=HI).astype(g_ref.dtype)


def _grp(w_grp):
    groups, n, _ = w_grp.shape
    cm, sm = _dft_cos_sin(n)
    m = jnp.asarray(np.concatenate([cm, sm], axis=0), F32)
    return pl.pallas_call(
        _grp_kernel,
        out_shape=jax.ShapeDtypeStruct((groups, 2 * n, n), BF16),
        grid=(groups,),
        in_specs=[_const_spec((2 * n, n)),
                  pl.BlockSpec((1, n, n), lambda g: (g, 0, 0))],
        out_specs=pl.BlockSpec((1, 2 * n, n), lambda g: (g, 0, 0)),
        compiler_params=_cparams("parallel"), name="group_weights",
    )(m, w_grp)


def _fn_out_kernel(xr_ref, xi_ref, z_ref, res_ref, mod_ref, gw_ref, w_ref, o_ref, *, groups, gd, d, scale):
    acc = None
    for g in range(groups):
        sl = slice(g * gd, (g + 1) * gd)
        f = jnp.dot(jnp.concatenate([xr_ref[0, :, sl], xi_ref[0, :, sl]], axis=1), gw_ref[g],
                    preferred_element_type=F32)
        y = (f * scale * _silu(z_ref[0, :, sl].astype(F32))).astype(BF16)
        p = jnp.dot(y, w_ref[sl, :], preferred_element_type=F32)
        acc = p if acc is None else acc + p
    gate = mod_ref[0, :, 2 * d:3 * d]
    o_ref[0] = res_ref[0] + gate * acc


def _fn_out(xr, xi, z, res, mod, gw, w_out, scale, tm=512):
    b, l, c = xr.shape
    d = res.shape[-1]
    tm = min(tm, l)
    groups, _, gd = gw.shape
    tok = lambda wd: pl.BlockSpec((1, tm, wd), lambda i, j: (i, j, 0))
    return pl.pallas_call(
        functools.partial(_fn_out_kernel, groups=groups, gd=gd, d=d, scale=scale),
        out_shape=jax.ShapeDtypeStruct((b, l, d), F32),
        grid=(b, l // tm),
        in_specs=[tok(c), tok(c), tok(c), tok(d),
                  pl.BlockSpec((1, 1, 3 * d), _mod_map(mod)),
                  _const_spec(gw.shape), _const_spec(w_out.shape)],
        out_specs=tok(d),
        compiler_params=_cparams("parallel", "parallel"), name="fourier_out",
    )(xr, xi, z, res, mod, gw, w_out)


def _proj_conv_kernel(xp_ref, xm_ref, xn_ref, g_ref, mod_ref, w_ref, wab_ref, abp_ref, cw_ref, sh_ref,
                      q_ref, k_ref, v_ref, z_ref, gb_ref, *, d, cols, tr, rp, pr, halo, nheads, dk, dv, cb):
    i = pl.program_id(1)
    last = pl.num_programs(1) - 1
    if halo:
        n_prev = (_norm_mod(xp_ref[0], g_ref, mod_ref, d) * (i > 0).astype(F32)).astype(BF16)
    n_main = _norm_mod(xm_ref[0], g_ref, mod_ref, d).astype(BF16)
    if halo:
        n_next = (_norm_mod(xn_ref[0], g_ref, mod_ref, d) * (i < last).astype(F32)).astype(BF16)
        ne = jnp.concatenate([n_prev, n_main, n_next], axis=0)
    qk_w = nheads * dk
    v_w = nheads * dv
    z0 = 2 * qk_w + v_w
    n_in = (tr + 2) * cols // pr
    n_out = tr * cols // rp

    def project(c0, p):
        if halo:
            rows = ne[p * pr:(p + 1) * pr]
        elif 0 < p < n_in - 1:
            rows = n_main[(p - 1) * pr:p * pr]
        else:
            return jnp.zeros((pr, cb), F32)
        return jnp.dot(rows, w_ref[:, c0:c0 + cb], preferred_element_type=F32)

    pieces = [project(0, p) for p in range(n_in)]
    zstep = v_w * cb // z0
    for c0 in range(0, z0, cb):
        nxt = []
        for j in range(n_out):
            if c0 + cb < z0:
                nxt += [project(c0 + cb, p) for p in range(len(nxt), (j + 1) * n_in // n_out)]
            if j == n_out // 2:
                zs = c0 // cb * zstep
                z_ref[0, :, zs:zs + zstep] = jnp.dot(
                    n_main, w_ref[:, z0 + zs:z0 + zs + zstep], preferred_element_type=F32).astype(z_ref.dtype)
            lo, hi = j * rp, j * rp + rp + 2 * cols
            p_lo, p_hi = lo // pr, (hi - 1) // pr
            e = jnp.concatenate(pieces[p_lo:p_hi + 1], axis=0)[lo - p_lo * pr:hi - p_lo * pr]
            part = []
            for dc in range(3):
                acc = None
                for dr in range(3):
                    term = e[dr * cols:dr * cols + rp] * cw_ref[dr, dc:dc + 1, c0:c0 + cb]
                    acc = term if acc is None else acc + term
                part.append(acc)
            y = _silu(jnp.dot(sh_ref[...], jnp.concatenate(part, axis=0).astype(BF16), preferred_element_type=F32))
            rows = slice(j * rp, (j + 1) * rp)
            if c0 < 2 * qk_w:
                o_ref, scale, base = (q_ref, dk ** -0.5, c0) if c0 < qk_w else (k_ref, 1.0, c0 - qk_w)
                for h in range(cb // dk):
                    yh = y[:, h * dk:(h + 1) * dk]
                    yh = yh * lax.rsqrt(jnp.sum(yh * yh, axis=-1, keepdims=True) + EPS) * scale
                    o_ref[0, base // dk + h, rows, :] = yh.astype(o_ref.dtype)
            else:
                base = c0 - 2 * qk_w
                for h in range(cb // dv):
                    v_ref[0, base // dv + h, rows, :] = y[:, h * dv:(h + 1) * dv].astype(v_ref.dtype)
        pieces = nxt
    r = jnp.dot(n_main, wab_ref[...], preferred_element_type=F32)
    t = r + abp_ref[1:2, :]
    softplus = jnp.maximum(t, 0.0) + jnp.log(1.0 + jnp.exp(-jnp.abs(t)))
    lane = lax.broadcasted_iota(jnp.int32, r.shape, 1)
    gb_ref[0] = jnp.where(lane < 2 * nheads, (-LOG2E) * jnp.exp(abp_ref[0:1, :]) * softplus, jax.nn.sigmoid(r))


def _proj_conv(x, g, mod, w, w_ab, abp, conv_w, rows, cols, cb=512):
    b, l, d = x.shape
    tr = min(rows, max(1, 512 // cols))
    n = tr * cols
    rp = cols * min(tr, 2)
    halo = rows > tr or rp != cols
    half = (tr + 2) * cols // 2
    pr = half if halo and half % 16 == 0 else rp
    t = np.arange(rp)
    left = (t[:, None] - 1 == t[None, :]) & (t[:, None] % cols != 0)
    right = (t[:, None] + 1 == t[None, :]) & (t[:, None] % cols != cols - 1)
    shift = jnp.asarray(np.concatenate([left, np.eye(rp, dtype=bool), right], axis=1), F32).astype(BF16)
    heads = lambda wd: pl.BlockSpec((1, DN_HEADS, n, wd), lambda bi, i: (bi, 0, i, 0))
    tok = lambda wd: pl.BlockSpec((1, n, wd), lambda bi, i: (bi, i, 0))
    return pl.pallas_call(
        functools.partial(_proj_conv_kernel, d=d, cols=cols, tr=tr, rp=rp, pr=pr, halo=halo,
                          nheads=DN_HEADS, dk=DN_DK, dv=DN_DV, cb=cb),
        out_shape=[jax.ShapeDtypeStruct((b, DN_HEADS, l, DN_DK), BF16),
                   jax.ShapeDtypeStruct((b, DN_HEADS, l, DN_DK), BF16),
                   jax.ShapeDtypeStruct((b, DN_HEADS, l, DN_DV), BF16),
                   jax.ShapeDtypeStruct((b, l, DN_HEADS * DN_DV), BF16),
                   jax.ShapeDtypeStruct((b, l, GB_LANES), F32)],
        grid=(b, rows // tr),
        in_specs=[pl.BlockSpec((1, cols, d), lambda bi, i: (bi, jnp.maximum(i * tr - 1, 0), 0)),
                  pl.BlockSpec((1, n, d), lambda bi, i: (bi, i, 0)),
                  pl.BlockSpec((1, cols, d), lambda bi, i: (bi, jnp.minimum((i + 1) * tr, rows - 1), 0)),
                  _const_spec((1, d)),
                  pl.BlockSpec((1, 1, 3 * d), _mod_map(mod)),
                  _const_spec(w.shape), _const_spec(w_ab.shape), _const_spec(abp.shape), _const_spec(conv_w.shape),
                  _const_spec(shift.shape)],
        out_specs=[heads(DN_DK), heads(DN_DK), heads(DN_DV), tok(DN_HEADS * DN_DV), tok(GB_LANES)],
        compiler_params=_cparams("parallel", "parallel"), name="proj_conv",
    )(x, x, x, g.reshape(1, d), mod, w, w_ab, abp, conv_w, shift)


def _delta_kernel(qa_ref, ka_ref, va_ref, ga_ref, qb_ref, kb_ref, vb_ref, gb_ref, s0_ref,
                  of_ref, ob_ref, sfin_ref, s_scr, sb_scr, *, nheads, cs):
    n = pl.program_id(1)

    @pl.when(n == 0)
    def _():
        s_scr[...] = s0_ref[0]
        sb_scr[...] = s0_ref[0].astype(sb_scr.dtype)

    ri = lax.broadcasted_iota(jnp.int32, (cs, cs), 0)
    ci = lax.broadcasted_iota(jnp.int32, (cs, cs), 1)
    eye = ri == ci
    xor = ri ^ ci
    nt = (((1,), (1,)), ((), ()))
    levels = int(math.log2(cs))
    chains = []
    for d, (q_ref, k_ref, v_ref, g_ref, o_ref) in enumerate(((qa_ref, ka_ref, va_ref, ga_ref, of_ref),
                                                              (qb_ref, kb_ref, vb_ref, gb_ref, ob_ref))):
        incl = (ri >= ci) if d == 0 else (ri <= ci)
        strict = (ri > ci) if d == 0 else (ri < ci)
        gbv = g_ref[0]
        hi = gbv.astype(BF16)
        rem = gbv - hi.astype(F32)
        mid = rem.astype(BF16)
        t01 = jnp.where(incl, 1.0, 0.0).astype(BF16)
        cum = sum(jnp.dot(t01, p, preferred_element_type=F32)
                  for p in (hi, mid, (rem - mid.astype(F32)).astype(BF16)))
        cum_t = cum.T
        last = cs - 1 if d == 0 else 0
        for hd in range(nheads):
            qb = q_ref[0, hd]
            kb = k_ref[0, hd]
            k = kb.astype(F32)
            v = v_ref[0, hd].astype(F32)
            kk = lax.dot_general(kb, kb, nt, preferred_element_type=F32)
            qk = lax.dot_general(qb, kb, nt, preferred_element_type=F32)
            lg = d * nheads + hd
            lb = (2 + d) * nheads + hd
            gc_col = cum[:, lg:lg + 1]
            gc_row = cum_t[lg:lg + 1, :]
            g_tot = cum[last:last + 1, lg:lg + 1]
            b_col = gbv[:, lb:lb + 1]
            ex = jnp.exp2(gc_col - gc_row)
            egc = jnp.exp2(gc_col)
            chains.append(dict(
                hd=hd, d=d, o_ref=o_ref, eg=jnp.exp2(g_tot),
                qh=(qb.astype(F32) * egc).astype(BF16),
                qkd=jnp.where(incl, qk * ex, 0.0).astype(BF16),
                ktt=(k.T * jnp.exp2(g_tot - gc_row)).astype(BF16),
                rhs=jnp.concatenate([v * b_col, k * (b_col * egc)], axis=1).astype(BF16),
                neg_off=jnp.where(strict, kk * (-b_col) * ex, 0.0)))
    dmats = [jnp.where(eye, 1.0, jnp.where(xor == 1, ch["neg_off"], 0.0)) for ch in chains]
    neg_offs = [ch["neg_off"].astype(BF16) for ch in chains]
    for lv in range(1, levels):
        sel = jnp.where((xor >> lv) == 1, 1.0, 0.0).astype(BF16)
        dmbs = [dm.astype(BF16) for dm in dmats]
        des = [jnp.dot(db, no * sel, preferred_element_type=F32) for db, no in zip(dmbs, neg_offs)]
        dmats = [dm + jnp.dot(de.astype(BF16), db, preferred_element_type=F32)
                 for dm, de, db in zip(dmats, des, dmbs)]
    sols = [jnp.dot(dm.astype(BF16), ch["rhs"], preferred_element_type=F32) for dm, ch in zip(dmats, chains)]
    dv = of_ref.shape[-1]
    r1s = [jnp.dot(jnp.concatenate([sol[:, dv:].astype(BF16), ch["qh"]], axis=0), sb_scr[ch["hd"], ch["d"]],
                   preferred_element_type=F32) for ch, sol in zip(chains, sols)]
    vns = [(sol[:, :dv] - r1[:cs]).astype(BF16) for sol, r1 in zip(sols, r1s)]
    r2s = [jnp.dot(jnp.concatenate([ch["qkd"], ch["ktt"]], axis=0), vn, preferred_element_type=F32)
           for ch, vn in zip(chains, vns)]
    for ch, r1, r2 in zip(chains, r1s, r2s):
        hd, d = ch["hd"], ch["d"]
        ch["o_ref"][0, hd] = (r1[cs:] + r2[:cs]).astype(of_ref.dtype)
        s_new = s_scr[hd, d] * ch["eg"] + r2[cs:]
        s_scr[hd, d] = s_new
        sb_scr[hd, d] = s_new.astype(sb_scr.dtype)

    @pl.when(n == pl.num_programs(1) - 1)
    def _():
        sfin_ref[0] = s_scr[...]


def _delta(q, k, v, gb, s0, cs=SCAN_CHUNK):
    b, nh, l, dk = q.shape
    dv = v.shape[-1]
    nc = l // cs
    fwd = lambda wd: pl.BlockSpec((1, nh, cs, wd), lambda bi, n: (bi, 0, n, 0))
    bwd = lambda wd: pl.BlockSpec((1, nh, cs, wd), lambda bi, n: (bi, 0, nc - 1 - n, 0))
    st = pl.BlockSpec((1, nh, 2, dk, dv), lambda bi, n: (bi, 0, 0, 0, 0))
    return pl.pallas_call(
        functools.partial(_delta_kernel, nheads=nh, cs=cs),
        out_shape=[jax.ShapeDtypeStruct((b, nh, l, dv), BF16),
                   jax.ShapeDtypeStruct((b, nh, l, dv), BF16),
                   jax.ShapeDtypeStruct((b, nh, 2, dk, dv), F32)],
        grid=(b, nc),
        in_specs=[fwd(dk), fwd(dk), fwd(dv), pl.BlockSpec((1, cs, GB_LANES), lambda bi, n: (bi, n, 0)),
                  bwd(dk), bwd(dk), bwd(dv), pl.BlockSpec((1, cs, GB_LANES), lambda bi, n: (bi, nc - 1 - n, 0)),
                  st],
        out_specs=[fwd(dv), bwd(dv), st],
        scratch_shapes=[pltpu.VMEM((nh, 2, dk, dv), F32), pltpu.VMEM((nh, 2, dk, dv), BF16)],
        compiler_params=_cparams("parallel", "arbitrary"), name="delta_rule",
    )(q, k, v, gb, q, k, v, gb, s0)


def _dn_out_kernel(of_ref, ob_ref, z_ref, res_ref, mod_ref, ng_ref, w_ref, fg_ref, o_ref, *, nheads, dv, d):
    acc = None
    for hd in range(nheads):
        sl = slice(hd * dv, (hd + 1) * dv)
        o = of_ref[0, hd].astype(F32) + ob_ref[0, hd].astype(F32)
        o = o * lax.rsqrt(jnp.mean(o * o, axis=-1, keepdims=True) + EPS) * ng_ref[...]
        y = (o * _silu(z_ref[0, :, sl].astype(F32))).astype(BF16)
        p = jnp.dot(y, w_ref[sl, :], preferred_element_type=F32)
        acc = p if acc is None else acc + p
    gate = mod_ref[0, :, 2 * d:3 * d]
    hn = res_ref[0] + gate * acc
    o_ref[0] = hn * lax.rsqrt(jnp.mean(hn * hn, axis=-1, keepdims=True) + EPS) * fg_ref[...]


def _dn_out(o_f, o_b, z, res, mod, norm_g, w_out, final_g, tm=512):
    b, nh, l, dv = o_f.shape
    d = res.shape[-1]
    tm = min(tm, l)
    tok = lambda wd: pl.BlockSpec((1, tm, wd), lambda i, j: (i, j, 0))
    hs = pl.BlockSpec((1, nh, tm, dv), lambda i, j: (i, 0, j, 0))
    return pl.pallas_call(
        functools.partial(_dn_out_kernel, nheads=nh, dv=dv, d=d),
        out_shape=jax.ShapeDtypeStruct((b, l, d), F32),
        grid=(b, l // tm),
        in_specs=[hs, hs, tok(nh * dv), tok(d),
                  pl.BlockSpec((1, 1, 3 * d), lambda i, j: (i, 0, 0)),
                  _const_spec((1, dv)), _const_spec(w_out.shape), _const_spec((1, d))],
        out_specs=tok(d),
        compiler_params=_cparams("parallel", "parallel"), name="delta_out",
    )(o_f, o_b, z, res, mod, norm_g.reshape(1, dv), w_out, final_g.reshape(1, d))


def _fourier_layer(h_lat, h_ctx, norm_g, mod_lat, mod_ctx, w_in, w_grp, w_out):
    b, l, _ = h_lat.shape
    lc = h_ctx.shape[1]
    width = w_in.shape[1] // 2
    gd = w_grp.shape[-1]
    w_in = w_in.astype(BF16)
    w_out = w_out.astype(BF16)
    gw = _grp(w_grp)

    u, z = _nmm(h_lat, norm_g, mod_lat, w_in, (width, width), (BF16, BF16))
    n2 = DFT_N2
    n1 = l // n2
    y = _dft1(u, n1, n2)
    xr, xi = _dft2_lat(_dft_complex_matrix(n2), y)
    xr = xr.reshape(b, l, width)
    xi = xi.reshape(b, l, width)
    out_lat = _fn_out(xr, xi, z, h_lat, mod_lat, gw, w_out, 1.0 / math.sqrt(l * gd))

    uc, zc = _nmm(h_ctx, norm_g, mod_ctx, w_in, (width, width), (BF16, BF16))
    xrc, xic = _dft2(_dft_real_matrix(lc), uc)
    out_ctx = _fn_out(xrc, xic, zc, h_ctx, mod_ctx, gw, w_out, 1.0 / math.sqrt(lc * gd))
    return out_lat, out_ctx


def _delta_layer_final(h_lat, h_ctx, norm_g, mod_lat, mod_ctx, w_in, conv_w, a_log, dt_bias, dn_norm_g, w_out,
                       final_g):
    b, l, d = h_lat.shape
    lc = h_ctx.shape[1]
    main_w = 2 * DN_HEADS * DN_DK + 2 * DN_HEADS * DN_DV
    n_ab = 4 * DN_HEADS
    w_main = w_in.astype(BF16)
    w_ab = jnp.pad(w_in[:, main_w:], ((0, 0), (0, GB_LANES - n_ab))).astype(BF16)
    pad = jnp.zeros((GB_LANES - 2 * DN_HEADS,), F32)
    abp = jnp.stack([jnp.concatenate([a_log.reshape(-1).astype(F32), pad]),
                     jnp.concatenate([dt_bias.reshape(-1).astype(F32), pad])])
    qc, kc, vc, _, gbc = _proj_conv(h_ctx, norm_g, mod_ctx, w_main, w_ab, abp, conv_w, 1, lc)
    ql, kl, vl, zl, gbl = _proj_conv(h_lat, norm_g, mod_lat, w_main, w_ab, abp, conv_w, l // GRID_W, GRID_W)
    s0 = jnp.zeros((b, DN_HEADS, 2, DN_DK, DN_DV), F32)
    _, _, s_ctx = _delta(qc, kc, vc, gbc, s0)
    o_f, o_b, _ = _delta(ql, kl, vl, gbl, s_ctx)
    return _dn_out(o_f, o_b, zl, h_lat, mod_lat, dn_norm_g, w_out.astype(BF16), final_g)


def kernel(x, c, ctx, c_ctx, mod_w, mod_b, norm_g, final_g, fn_w_in, fn_w_grp, fn_w_out,
           dn_w_in, dn_conv, dn_a_log, dn_dt_bias, dn_norm_g, dn_w_out):
    b, l, d = x.shape
    cond = jnp.concatenate([c, c_ctx[None, :], jnp.zeros((8 - b - 1, d), c.dtype)], axis=0)
    mods = _mods(cond, mod_w, mod_b)
    mod_lat = [mods[i, :b].reshape(b, 1, 3 * d) for i in range(2)]
    mod_ctx = [mods[i, b:b + 1].reshape(1, 1, 3 * d) for i in range(2)]
    h_lat, h_ctx = _fourier_layer(x, ctx, norm_g[0], mod_lat[0], mod_ctx[0], fn_w_in[0], fn_w_grp[0], fn_w_out[0])
    return _delta_layer_final(h_lat, h_ctx, norm_g[1], mod_lat[1], mod_ctx[1], dn_w_in[0], dn_conv[0],
                              dn_a_log[0], dn_dt_bias[0], dn_norm_g[0], dn_w_out[0], final_g)
```

```python
import functools
import math

import numpy as np
import jax
import jax.numpy as jnp
from jax import lax
from jax.experimental import pallas as pl
from jax.experimental.pallas import tpu as pltpu

F32 = jnp.float32
BF16 = jnp.bfloat16
HI = lax.Precision.HIGHEST
EPS = 1e-6
LOG2E = 1.4426950408889634

GRID_W = 64
FN_GROUPS = 8
DN_HEADS = 8
DN_DK = 128
DN_DV = 256
SCAN_CHUNK = 128
DFT_N2 = 128
GB_LANES = 128
VMEM_LIMIT_BYTES = 48 * 1024 * 1024


def _cparams(*sem):
    return pltpu.CompilerParams(dimension_semantics=sem, vmem_limit_bytes=VMEM_LIMIT_BYTES)


def _const_spec(shape):
    zeros = (0,) * len(shape)
    return pl.BlockSpec(shape, lambda *_: zeros, pipeline_mode=pl.Buffered(1))


def _silu(v):
    return v * jax.nn.sigmoid(v)


def _mods_kernel(cond_ref, w_ref, b_ref, o_ref):
    s = _silu(cond_ref[...])
    o_ref[0] = jnp.dot(s, w_ref[0], preferred_element_type=F32, precision=HI) + b_ref[0]


def _mods(cond, mod_w, mod_b):
    depth, d, d3 = mod_w.shape
    rows = cond.shape[0]
    tn = 1024
    return pl.pallas_call(
        _mods_kernel,
        out_shape=jax.ShapeDtypeStruct((depth, rows, d3), F32),
        grid=(depth, d3 // tn),
        in_specs=[pl.BlockSpec((rows, d), lambda i, j: (0, 0)),
                  pl.BlockSpec((1, d, tn), lambda i, j: (i, 0, j)),
                  pl.BlockSpec((1, 1, tn), lambda i, j: (i, 0, j))],
        out_specs=pl.BlockSpec((1, rows, tn), lambda i, j: (i, 0, j)),
        compiler_params=_cparams("parallel", "parallel"), name="adaln_mods",
    )(cond, mod_w, mod_b.reshape(depth, 1, d3))


def _norm_mod(x, g_ref, mod_ref, d):
    y = x * lax.rsqrt(jnp.mean(x * x, axis=-1, keepdims=True) + EPS) * g_ref[...]
    return y * (1.0 + mod_ref[0, :, d:2 * d]) + mod_ref[0, :, 0:d]


def _nmm_kernel(x_ref, g_ref, mod_ref, w_ref, *outs, splits, d):
    n = _norm_mod(x_ref[0], g_ref, mod_ref, d).astype(BF16)
    c0 = 0
    for o_ref, width in zip(outs, splits):
        step = min(width, 1024)
        for s in range(0, width, step):
            r = jnp.dot(n, w_ref[:, c0 + s:c0 + s + step].astype(BF16), preferred_element_type=F32)
            o_ref[0, :, s:s + step] = r.astype(o_ref.dtype)
        c0 += width


def _mod_map(mod):
    return (lambda i, j: (i, 0, 0)) if mod.shape[0] > 1 else (lambda i, j: (0, 0, 0))


def _nmm(x, g, mod, w, splits, dtypes, tm=512):
    b, l, d = x.shape
    tm = min(tm, l)
    return pl.pallas_call(
        functools.partial(_nmm_kernel, splits=splits, d=d),
        out_shape=[jax.ShapeDtypeStruct((b, l, wd), dt) for wd, dt in zip(splits, dtypes)],
        grid=(b, l // tm),
        in_specs=[pl.BlockSpec((1, tm, d), lambda i, j: (i, j, 0)),
                  _const_spec((1, d)),
                  pl.BlockSpec((1, 1, 3 * d), _mod_map(mod)),
                  _const_spec(w.shape)],
        out_specs=[pl.BlockSpec((1, tm, wd), lambda i, j: (i, j, 0)) for wd in splits],
        compiler_params=_cparams("parallel", "parallel"), name="norm_proj",
    )(x, g.reshape(1, d), mod, w)


def _dft_cos_sin(n):
    k = np.arange(n)
    ang = 2.0 * np.pi * ((k[:, None] * k[None, :]) % n) / n
    return np.cos(ang), np.sin(ang)


def _dft1_kernel(m_ref, u_ref, cos_ref, sin_ref, o_ref, *, n1, tn2):
    ut = jnp.transpose(u_ref[0], (1, 0, 2))
    for j in range(tn2):
        r = jnp.dot(m_ref[...], ut[j], preferred_element_type=F32)
        yr = r[:n1]
        yi = r[n1:]
        cs = cos_ref[j]
        sn = sin_ref[j]
        o_ref[0, 0, j] = (yr * cs + yi * sn).astype(o_ref.dtype)
        o_ref[0, 1, j] = (yi * cs - yr * sn).astype(o_ref.dtype)


def _dft1(u, n1, n2, tn2=16, tc=2048):
    b, l, c = u.shape
    cm, sm = _dft_cos_sin(n1)
    m1 = jnp.asarray(np.concatenate([cm, -sm], axis=0), F32).astype(BF16)
    ang = 2.0 * np.pi * (np.arange(n2)[:, None] * np.arange(n1)[None, :]) / l
    tw_cos = jnp.asarray(np.cos(ang)[:, :, None], F32)
    tw_sin = jnp.asarray(np.sin(ang)[:, :, None], F32)
    return pl.pallas_call(
        functools.partial(_dft1_kernel, n1=n1, tn2=tn2),
        out_shape=jax.ShapeDtypeStruct((b, 2, n2, n1, c), BF16),
        grid=(b, n2 // tn2, c // tc),
        in_specs=[_const_spec((2 * n1, n1)),
                  pl.BlockSpec((1, n1, tn2, tc), lambda i, j, k: (i, 0, j, k)),
                  pl.BlockSpec((tn2, n1, 1), lambda i, j, k: (j, 0, 0)),
                  pl.BlockSpec((tn2, n1, 1), lambda i, j, k: (j, 0, 0))],
        out_specs=pl.BlockSpec((1, 2, tn2, n1, tc), lambda i, j, k: (i, 0, j, 0, k)),
        compiler_params=_cparams("parallel", "parallel", "parallel"), name="dft_stage1",
    )(m1, u.reshape(b, n1, n2, c), tw_cos, tw_sin)


def _dft2_lat_kernel(m_ref, y_ref, xr_ref, xi_ref, *, n2, tk1):
    yr = jnp.transpose(y_ref[0, 0], (1, 0, 2))
    yi = jnp.transpose(y_ref[0, 1], (1, 0, 2))
    xr, xi = [], []
    for j in range(tk1):
        r = jnp.dot(m_ref[...], jnp.concatenate([yr[j], yi[j]], axis=0), preferred_element_type=F32)
        xr.append(r[:n2].astype(xr_ref.dtype))
        xi.append(r[n2:].astype(xi_ref.dtype))
    xr_ref[0] = jnp.transpose(jnp.stack(xr), (1, 0, 2))
    xi_ref[0] = jnp.transpose(jnp.stack(xi), (1, 0, 2))


def _dft2_lat(m, y, tk1=16, tc=512):
    b, _, n2, n1, c = y.shape
    tk1 = min(tk1, n1)
    return pl.pallas_call(
        functools.partial(_dft2_lat_kernel, n2=n2, tk1=tk1),
        out_shape=[jax.ShapeDtypeStruct((b, n2, n1, c), BF16)] * 2,
        grid=(b, n1 // tk1, c // tc),
        in_specs=[_const_spec(m.shape),
                  pl.BlockSpec((1, 2, n2, tk1, tc), lambda i, j, k: (i, 0, 0, j, k))],
        out_specs=[pl.BlockSpec((1, n2, tk1, tc), lambda i, j, k: (i, 0, j, k))] * 2,
        compiler_params=_cparams("parallel", "parallel", "parallel"), name="dft_stage2_lat",
    )(m, y)


def _dft2_kernel(m_ref, y_ref, xr_ref, xi_ref, *, half):
    r = jnp.dot(m_ref[...], y_ref[0], preferred_element_type=F32)
    xr_ref[0] = r[:half].astype(xr_ref.dtype)
    xi_ref[0] = r[half:].astype(xi_ref.dtype)


def _dft2(m, y, tc=2048):
    b, k, cols = y.shape
    half = m.shape[0] // 2
    return pl.pallas_call(
        functools.partial(_dft2_kernel, half=half),
        out_shape=[jax.ShapeDtypeStruct((b, half, cols), BF16)] * 2,
        grid=(b, cols // tc),
        in_specs=[_const_spec(m.shape),
                  pl.BlockSpec((1, k, tc), lambda i, j: (i, 0, j))],
        out_specs=[pl.BlockSpec((1, half, tc), lambda i, j: (i, 0, j))] * 2,
        compiler_params=_cparams("parallel", "parallel"), name="dft_stage2",
    )(m, y)


def _dft_real_matrix(n):
    cm, sm = _dft_cos_sin(n)
    return jnp.asarray(np.concatenate([cm, -sm], axis=0), F32).astype(BF16)


def _dft_complex_matrix(n):
    cm, sm = _dft_cos_sin(n)
    return jnp.asarray(np.block([[cm, sm], [-sm, cm]]), F32).astype(BF16)


def _grp_kernel(m_ref, w_ref, g_ref):
    g_ref[0] = jnp.dot(m_ref[...], w_ref[0], preferred_element_type=F32, precision=HI).astype(g_ref.dtype)


def _grp(w_grp):
    groups, n, _ = w_grp.shape
    cm, sm = _dft_cos_sin(n)
    m = jnp.asarray(np.concatenate([cm, sm], axis=0), F32)
    return pl.pallas_call(
        _grp_kernel,
        out_shape=jax.ShapeDtypeStruct((groups, 2 * n, n), BF16),
        grid=(groups,),
        in_specs=[_const_spec((2 * n, n)),
                  pl.BlockSpec((1, n, n), lambda g: (g, 0, 0))],
        out_specs=pl.BlockSpec((1, 2 * n, n), lambda g: (g, 0, 0)),
        compiler_params=_cparams("parallel"), name="group_weights",
    )(m, w_grp)


def _fn_out_kernel(xr_ref, xi_ref, z_ref, res_ref, mod_ref, gw_ref, w_ref, o_ref, *, groups, gd, d, scale):
    acc = None
    for g in range(groups):
        sl = slice(g * gd, (g + 1) * gd)
        f = jnp.dot(jnp.concatenate([xr_ref[0, :, sl], xi_ref[0, :, sl]], axis=1), gw_ref[g],
                    preferred_element_type=F32)
        y = (f * scale * _silu(z_ref[0, :, sl].astype(F32))).astype(BF16)
        p = jnp.dot(y, w_ref[sl, :].astype(BF16), preferred_element_type=F32)
        acc = p if acc is None else acc + p
    gate = mod_ref[0, :, 2 * d:3 * d]
    o_ref[0] = res_ref[0] + gate * acc


def _fn_out(xr, xi, z, res, mod, gw, w_out, scale, tm=512):
    b, l, c = xr.shape
    d = res.shape[-1]
    tm = min(tm, l)
    groups, _, gd = gw.shape
    tok = lambda wd: pl.BlockSpec((1, tm, wd), lambda i, j: (i, j, 0))
    return pl.pallas_call(
        functools.partial(_fn_out_kernel, groups=groups, gd=gd, d=d, scale=scale),
        out_shape=jax.ShapeDtypeStruct((b, l, d), F32),
        grid=(b, l // tm),
        in_specs=[tok(c), tok(c), tok(c), tok(d),
                  pl.BlockSpec((1, 1, 3 * d), _mod_map(mod)),
                  _const_spec(gw.shape), _const_spec(w_out.shape)],
        out_specs=tok(d),
        compiler_params=_cparams("parallel", "parallel"), name="fourier_out",
    )(xr, xi, z, res, mod, gw, w_out)


def _proj_conv_kernel(xp_ref, xm_ref, xn_ref, g_ref, mod_ref, w_ref, wab_ref, abp_ref, cw_ref, sh_ref,
                      q_ref, k_ref, v_ref, z_ref, gb_ref, *, d, cols, tr, rp, pr, halo, nheads, dk, dv, cb):
    i = pl.program_id(1)
    last = pl.num_programs(1) - 1
    if halo:
        n_prev = (_norm_mod(xp_ref[0], g_ref, mod_ref, d) * (i > 0).astype(F32)).astype(BF16)
    n_main = _norm_mod(xm_ref[0], g_ref, mod_ref, d).astype(BF16)
    if halo:
        n_next = (_norm_mod(xn_ref[0], g_ref, mod_ref, d) * (i < last).astype(F32)).astype(BF16)
        ne = jnp.concatenate([n_prev, n_main, n_next], axis=0)
    qk_w = nheads * dk
    v_w = nheads * dv
    z0 = 2 * qk_w + v_w
    n_in = (tr + 2) * cols // pr
    n_out = tr * cols // rp

    def project(c0, p):
        if halo:
            rows = ne[p * pr:(p + 1) * pr]
        elif 0 < p < n_in - 1:
            rows = n_main[(p - 1) * pr:p * pr]
        else:
            return jnp.zeros((pr, cb), F32)
        return jnp.dot(rows, w_ref[:, c0:c0 + cb], preferred_element_type=F32)

    pieces = [project(0, p) for p in range(n_in)]
    zstep = v_w * cb // z0
    for c0 in range(0, z0, cb):
        nxt = []
        for j in range(n_out):
            if c0 + cb < z0:
                nxt += [project(c0 + cb, p) for p in range(len(nxt), (j + 1) * n_in // n_out)]
            if j == n_out // 2:
                zs = c0 // cb * zstep
                z_ref[0, :, zs:zs + zstep] = jnp.dot(
                    n_main, w_ref[:, z0 + zs:z0 + zs + zstep], preferred_element_type=F32).astype(z_ref.dtype)
            lo, hi = j * rp, j * rp + rp + 2 * cols
            p_lo, p_hi = lo // pr, (hi - 1) // pr
            e = jnp.concatenate(pieces[p_lo:p_hi + 1], axis=0)[lo - p_lo * pr:hi - p_lo * pr]
            part = []
            for dc in range(3):
                acc = None
                for dr in range(3):
                    term = e[dr * cols:dr * cols + rp] * cw_ref[dr, dc:dc + 1, c0:c0 + cb]
                    acc = term if acc is None else acc + term
                part.append(acc)
            y = _silu(jnp.dot(sh_ref[...], jnp.concatenate(part, axis=0).astype(BF16), preferred_element_type=F32))
            rows = slice(j * rp, (j + 1) * rp)
            if c0 < 2 * qk_w:
                o_ref, scale, base = (q_ref, dk ** -0.5, c0) if c0 < qk_w else (k_ref, 1.0, c0 - qk_w)
                for h in range(cb // dk):
                    yh = y[:, h * dk:(h + 1) * dk]
                    yh = yh * lax.rsqrt(jnp.sum(yh * yh, axis=-1, keepdims=True) + EPS) * scale
                    o_ref[0, base // dk + h, rows, :] = yh.astype(o_ref.dtype)
            else:
                base = c0 - 2 * qk_w
                for h in range(cb // dv):
                    v_ref[0, base // dv + h, rows, :] = y[:, h * dv:(h + 1) * dv].astype(v_ref.dtype)
        pieces = nxt
    r = jnp.dot(n_main, wab_ref[...], preferred_element_type=F32)
    t = r + abp_ref[1:2, :]
    softplus = jnp.maximum(t, 0.0) + jnp.log(1.0 + jnp.exp(-jnp.abs(t)))
    lane = lax.broadcasted_iota(jnp.int32, r.shape, 1)
    gb_ref[0] = jnp.where(lane < 2 * nheads, (-LOG2E) * jnp.exp(abp_ref[0:1, :]) * softplus, jax.nn.sigmoid(r))


def _proj_conv(x, g, mod, w, w_ab, abp, conv_w, rows, cols, cb=512):
    b, l, d = x.shape
    tr = min(rows, max(1, 512 // cols))
    n = tr * cols
    rp = cols * min(tr, 2)
    halo = rows > tr or rp != cols
    half = (tr + 2) * cols // 2
    pr = half if halo and half % 16 == 0 else rp
    t = np.arange(rp)
    left = (t[:, None] - 1 == t[None, :]) & (t[:, None] % cols != 0)
    right = (t[:, None] + 1 == t[None, :]) & (t[:, None] % cols != cols - 1)
    shift = jnp.asarray(np.concatenate([left, np.eye(rp, dtype=bool), right], axis=1), F32).astype(BF16)
    heads = lambda wd: pl.BlockSpec((1, DN_HEADS, n, wd), lambda bi, i: (bi, 0, i, 0))
    tok = lambda wd: pl.BlockSpec((1, n, wd), lambda bi, i: (bi, i, 0))
    return pl.pallas_call(
        functools.partial(_proj_conv_kernel, d=d, cols=cols, tr=tr, rp=rp, pr=pr, halo=halo,
                          nheads=DN_HEADS, dk=DN_DK, dv=DN_DV, cb=cb),
        out_shape=[jax.ShapeDtypeStruct((b, DN_HEADS, l, DN_DK), BF16),
                   jax.ShapeDtypeStruct((b, DN_HEADS, l, DN_DK), BF16),
                   jax.ShapeDtypeStruct((b, DN_HEADS, l, DN_DV), BF16),
                   jax.ShapeDtypeStruct((b, l, DN_HEADS * DN_DV), BF16),
                   jax.ShapeDtypeStruct((b, l, GB_LANES), F32)],
        grid=(b, rows // tr),
        in_specs=[pl.BlockSpec((1, cols, d), lambda bi, i: (bi, jnp.maximum(i * tr - 1, 0), 0)),
                  pl.BlockSpec((1, n, d), lambda bi, i: (bi, i, 0)),
                  pl.BlockSpec((1, cols, d), lambda bi, i: (bi, jnp.minimum((i + 1) * tr, rows - 1), 0)),
                  _const_spec((1, d)),
                  pl.BlockSpec((1, 1, 3 * d), _mod_map(mod)),
                  _const_spec(w.shape), _const_spec(w_ab.shape), _const_spec(abp.shape), _const_spec(conv_w.shape),
                  _const_spec(shift.shape)],
        out_specs=[heads(DN_DK), heads(DN_DK), heads(DN_DV), tok(DN_HEADS * DN_DV), tok(GB_LANES)],
        compiler_params=_cparams("parallel", "parallel"), name="proj_conv",
    )(x, x, x, g.reshape(1, d), mod, w, w_ab, abp, conv_w, shift)


def _delta_kernel(qa_ref, ka_ref, va_ref, ga_ref, qb_ref, kb_ref, vb_ref, gb_ref, s0_ref,
                  of_ref, ob_ref, sfin_ref, s_scr, sb_scr, *, nheads, cs):
    n = pl.program_id(1)

    @pl.when(n == 0)
    def _():
        s_scr[...] = s0_ref[0]
        sb_scr[...] = s0_ref[0].astype(sb_scr.dtype)

    ri = lax.broadcasted_iota(jnp.int32, (cs, cs), 0)
    ci = lax.broadcasted_iota(jnp.int32, (cs, cs), 1)
    eye = ri == ci
    xor = ri ^ ci
    nt = (((1,), (1,)), ((), ()))
    levels = int(math.log2(cs))
    chains = []
    for d, (q_ref, k_ref, v_ref, g_ref, o_ref) in enumerate(((qa_ref, ka_ref, va_ref, ga_ref, of_ref),
                                                              (qb_ref, kb_ref, vb_ref, gb_ref, ob_ref))):
        incl = (ri >= ci) if d == 0 else (ri <= ci)
        strict = (ri > ci) if d == 0 else (ri < ci)
        gbv = g_ref[0]
        hi = gbv.astype(BF16)
        rem = gbv - hi.astype(F32)
        mid = rem.astype(BF16)
        t01 = jnp.where(incl, 1.0, 0.0).astype(BF16)
        cum = sum(jnp.dot(t01, p, preferred_element_type=F32)
                  for p in (hi, mid, (rem - mid.astype(F32)).astype(BF16)))
        cum_t = cum.T
        last = cs - 1 if d == 0 else 0
        for hd in range(nheads):
            qb = q_ref[0, hd]
            kb = k_ref[0, hd]
            k = kb.astype(F32)
            v = v_ref[0, hd].astype(F32)
            kk = lax.dot_general(kb, kb, nt, preferred_element_type=F32)
            qk = lax.dot_general(qb, kb, nt, preferred_element_type=F32)
            lg = d * nheads + hd
            lb = (2 + d) * nheads + hd
            gc_col = cum[:, lg:lg + 1]
            gc_row = cum_t[lg:lg + 1, :]
            g_tot = cum[last:last + 1, lg:lg + 1]
            b_col = gbv[:, lb:lb + 1]
            ex = jnp.exp2(gc_col - gc_row)
            egc = jnp.exp2(gc_col)
            chains.append(dict(
                hd=hd, d=d, o_ref=o_ref, eg=jnp.exp2(g_tot),
                qh=(qb.astype(F32) * egc).astype(BF16),
                qkd=jnp.where(incl, qk * ex, 0.0).astype(BF16),
                ktt=(k.T * jnp.exp2(g_tot - gc_row)).astype(BF16),
                rhs=jnp.concatenate([v * b_col, k * (b_col * egc)], axis=1).astype(BF16),
                neg_off=jnp.where(strict, kk * (-b_col) * ex, 0.0)))
    dmats = [jnp.where(eye, 1.0, jnp.where(xor == 1, ch["neg_off"], 0.0)) for ch in chains]
    neg_offs = [ch["neg_off"].astype(BF16) for ch in chains]
    for lv in range(1, levels):
        sel = jnp.where((xor >> lv) == 1, 1.0, 0.0).astype(BF16)
        dmbs = [dm.astype(BF16) for dm in dmats]
        des = [jnp.dot(db, no * sel, preferred_element_type=F32) for db, no in zip(dmbs, neg_offs)]
        dmats = [dm + jnp.dot(de.astype(BF16), db, preferred_element_type=F32)
                 for dm, de, db in zip(dmats, des, dmbs)]
    sols = [jnp.dot(dm.astype(BF16), ch["rhs"], preferred_element_type=F32) for dm, ch in zip(dmats, chains)]
    dv = of_ref.shape[-1]
    r1s = [jnp.dot(jnp.concatenate([sol[:, dv:].astype(BF16), ch["qh"]], axis=0), sb_scr[ch["hd"], ch["d"]],
                   preferred_element_type=F32) for ch, sol in zip(chains, sols)]
    vns = [(sol[:, :dv] - r1[:cs]).astype(BF16) for sol, r1 in zip(sols, r1s)]
    r2s = [jnp.dot(jnp.concatenate([ch["qkd"], ch["ktt"]], axis=0), vn, preferred_element_type=F32)
           for ch, vn in zip(chains, vns)]
    for ch, r1, r2 in zip(chains, r1s, r2s):
        hd, d = ch["hd"], ch["d"]
        ch["o_ref"][0, hd] = (r1[cs:] + r2[:cs]).astype(of_ref.dtype)
        s_new = s_scr[hd, d] * ch["eg"] + r2[cs:]
        s_scr[hd, d] = s_new
        sb_scr[hd, d] = s_new.astype(sb_scr.dtype)

    @pl.when(n == pl.num_programs(1) - 1)
    def _():
        sfin_ref[0] = s_scr[...]


def _delta(q, k, v, gb, s0, cs=SCAN_CHUNK):
    b, nh, l, dk = q.shape
    dv = v.shape[-1]
    nc = l // cs
    fwd = lambda wd: pl.BlockSpec((1, nh, cs, wd), lambda bi, n: (bi, 0, n, 0))
    bwd = lambda wd: pl.BlockSpec((1, nh, cs, wd), lambda bi, n: (bi, 0, nc - 1 - n, 0))
    st = pl.BlockSpec((1, nh, 2, dk, dv), lambda bi, n: (bi, 0, 0, 0, 0))
    return pl.pallas_call(
        functools.partial(_delta_kernel, nheads=nh, cs=cs),
        out_shape=[jax.ShapeDtypeStruct((b, nh, l, dv), BF16),
                   jax.ShapeDtypeStruct((b, nh, l, dv), BF16),
                   jax.ShapeDtypeStruct((b, nh, 2, dk, dv), F32)],
        grid=(b, nc),
        in_specs=[fwd(dk), fwd(dk), fwd(dv), pl.BlockSpec((1, cs, GB_LANES), lambda bi, n: (bi, n, 0)),
                  bwd(dk), bwd(dk), bwd(dv), pl.BlockSpec((1, cs, GB_LANES), lambda bi, n: (bi, nc - 1 - n, 0)),
                  st],
        out_specs=[fwd(dv), bwd(dv), st],
        scratch_shapes=[pltpu.VMEM((nh, 2, dk, dv), F32), pltpu.VMEM((nh, 2, dk, dv), BF16)],
        compiler_params=_cparams("parallel", "arbitrary"), name="delta_rule",
    )(q, k, v, gb, q, k, v, gb, s0)


def _dn_out_kernel(of_ref, ob_ref, z_ref, res_ref, mod_ref, ng_ref, w_ref, fg_ref, o_ref, *, nheads, dv, d):
    acc = None
    for hd in range(nheads):
        sl = slice(hd * dv, (hd + 1) * dv)
        o = of_ref[0, hd].astype(F32) + ob_ref[0, hd].astype(F32)
        o = o * lax.rsqrt(jnp.mean(o * o, axis=-1, keepdims=True) + EPS) * ng_ref[...]
        y = (o * _silu(z_ref[0, :, sl].astype(F32))).astype(BF16)
        p = jnp.dot(y, w_ref[sl, :].astype(BF16), preferred_element_type=F32)
        acc = p if acc is None else acc + p
    gate = mod_ref[0, :, 2 * d:3 * d]
    hn = res_ref[0] + gate * acc
    o_ref[0] = hn * lax.rsqrt(jnp.mean(hn * hn, axis=-1, keepdims=True) + EPS) * fg_ref[...]


def _dn_out(o_f, o_b, z, res, mod, norm_g, w_out, final_g, tm=512):
    b, nh, l, dv = o_f.shape
    d = res.shape[-1]
    tm = min(tm, l)
    tok = lambda wd: pl.BlockSpec((1, tm, wd), lambda i, j: (i, j, 0))
    hs = pl.BlockSpec((1, nh, tm, dv), lambda i, j: (i, 0, j, 0))
    return pl.pallas_call(
        functools.partial(_dn_out_kernel, nheads=nh, dv=dv, d=d),
        out_shape=jax.ShapeDtypeStruct((b, l, d), F32),
        grid=(b, l // tm),
        in_specs=[hs, hs, tok(nh * dv), tok(d),
                  pl.BlockSpec((1, 1, 3 * d), lambda i, j: (i, 0, 0)),
                  _const_spec((1, dv)), _const_spec(w_out.shape), _const_spec((1, d))],
        out_specs=tok(d),
        compiler_params=_cparams("parallel", "parallel"), name="delta_out",
    )(o_f, o_b, z, res, mod, norm_g.reshape(1, dv), w_out, final_g.reshape(1, d))


def _fourier_layer(h_lat, h_ctx, norm_g, mod_lat, mod_ctx, w_in, w_grp, w_out):
    b, l, _ = h_lat.shape
    lc = h_ctx.shape[1]
    width = w_in.shape[1] // 2
    gd = w_grp.shape[-1]
    gw = _grp(w_grp)

    u, z = _nmm(h_lat, norm_g, mod_lat, w_in, (width, width), (BF16, BF16))
    n2 = DFT_N2
    n1 = l // n2
    y = _dft1(u, n1, n2)
    xr, xi = _dft2_lat(_dft_complex_matrix(n2), y)
    xr = xr.reshape(b, l, width)
    xi = xi.reshape(b, l, width)
    out_lat = _fn_out(xr, xi, z, h_lat, mod_lat, gw, w_out, 1.0 / math.sqrt(l * gd))

    uc, zc = _nmm(h_ctx, norm_g, mod_ctx, w_in, (width, width), (BF16, BF16))
    xrc, xic = _dft2(_dft_real_matrix(lc), uc)
    out_ctx = _fn_out(xrc, xic, zc, h_ctx, mod_ctx, gw, w_out, 1.0 / math.sqrt(lc * gd))
    return out_lat, out_ctx


def _delta_layer_final(h_lat, h_ctx, norm_g, mod_lat, mod_ctx, w_in, conv_w, a_log, dt_bias, dn_norm_g, w_out,
                       final_g):
    b, l, d = h_lat.shape
    lc = h_ctx.shape[1]
    main_w = 2 * DN_HEADS * DN_DK + 2 * DN_HEADS * DN_DV
    n_ab = 4 * DN_HEADS
    w_main = w_in.astype(BF16)
    w_ab = jnp.pad(w_in[:, main_w:], ((0, 0), (0, GB_LANES - n_ab))).astype(BF16)
    pad = jnp.zeros((GB_LANES - 2 * DN_HEADS,), F32)
    abp = jnp.stack([jnp.concatenate([a_log.reshape(-1).astype(F32), pad]),
                     jnp.concatenate([dt_bias.reshape(-1).astype(F32), pad])])
    qc, kc, vc, _, gbc = _proj_conv(h_ctx, norm_g, mod_ctx, w_main, w_ab, abp, conv_w, 1, lc)
    ql, kl, vl, zl, gbl = _proj_conv(h_lat, norm_g, mod_lat, w_main, w_ab, abp, conv_w, l // GRID_W, GRID_W)
    s0 = jnp.zeros((b, DN_HEADS, 2, DN_DK, DN_DV), F32)
    _, _, s_ctx = _delta(qc, kc, vc, gbc, s0)
    o_f, o_b, _ = _delta(ql, kl, vl, gbl, s_ctx)
    return _dn_out(o_f, o_b, zl, h_lat, mod_lat, dn_norm_g, w_out, final_g)


def kernel(x, c, ctx, c_ctx, mod_w, mod_b, norm_g, final_g, fn_w_in, fn_w_grp, fn_w_out,
           dn_w_in, dn_conv, dn_a_log, dn_dt_bias, dn_norm_g, dn_w_out):
    b, l, d = x.shape
    cond = jnp.concatenate([c, c_ctx[None, :], jnp.zeros((8 - b - 1, d), c.dtype)], axis=0)
    mods = _mods(cond, mod_w, mod_b)
    mod_lat = [mods[i, :b].reshape(b, 1, 3 * d) for i in range(2)]
    mod_ctx = [mods[i, b:b + 1].reshape(1, 1, 3 * d) for i in range(2)]
    h_lat, h_ctx = _fourier_layer(x, ctx, norm_g[0], mod_lat[0], mod_ctx[0], fn_w_in[0], fn_w_grp[0], fn_w_out[0])
    return _delta_layer_final(h_lat, h_ctx, norm_g[1], mod_lat[1], mod_ctx[1], dn_w_in[0], dn_conv[0],
                              dn_a_log[0], dn_dt_bias[0], dn_norm_g[0], dn_w_out[0], final_g)
```

```python
import functools
import math

import numpy as np
import jax
import jax.numpy as jnp
from jax import lax
from jax.experimental import pallas as pl
from jax.experimental.pallas import tpu as pltpu

F32 = jnp.float32
BF16 = jnp.bfloat16
HI = lax.Precision.HIGHEST
EPS = 1e-6
LOG2E = 1.4426950408889634

GRID_W = 64
FN_GROUPS = 8
DN_HEADS = 8
DN_DK = 128
DN_DV = 256
SCAN_CHUNK = 128
DFT_N2 = 128
GB_LANES = 128
VMEM_LIMIT_BYTES = 48 * 1024 * 1024


def _cparams(*sem):
    return pltpu.CompilerParams(dimension_semantics=sem, vmem_limit_bytes=VMEM_LIMIT_BYTES)


def _const_spec(shape):
    zeros = (0,) * len(shape)
    return pl.BlockSpec(shape, lambda *_: zeros, pipeline_mode=pl.Buffered(1))


def _silu(v):
    return v * jax.nn.sigmoid(v)


def _mods_kernel(cond_ref, w_ref, b_ref, o_ref):
    s = _silu(cond_ref[...])
    o_ref[0] = jnp.dot(s, w_ref[0], preferred_element_type=F32, precision=HI) + b_ref[0]


def _mods(cond, mod_w, mod_b):
    depth, d, d3 = mod_w.shape
    rows = cond.shape[0]
    tn = 1024
    return pl.pallas_call(
        _mods_kernel,
        out_shape=jax.ShapeDtypeStruct((depth, rows, d3), F32),
        grid=(depth, d3 // tn),
        in_specs=[pl.BlockSpec((rows, d), lambda i, j: (0, 0)),
                  pl.BlockSpec((1, d, tn), lambda i, j: (i, 0, j)),
                  pl.BlockSpec((1, 1, tn), lambda i, j: (i, 0, j))],
        out_specs=pl.BlockSpec((1, rows, tn), lambda i, j: (i, 0, j)),
        compiler_params=_cparams("parallel", "parallel"), name="adaln_mods",
    )(cond, mod_w, mod_b.reshape(depth, 1, d3))


def _norm_mod(x, g_ref, mod_ref, d):
    y = x * lax.rsqrt(jnp.mean(x * x, axis=-1, keepdims=True) + EPS) * g_ref[...]
    return y * (1.0 + mod_ref[0, :, d:2 * d]) + mod_ref[0, :, 0:d]


def _nmm_kernel(x_ref, g_ref, mod_ref, w_ref, *outs, splits, d):
    n = _norm_mod(x_ref[0], g_ref, mod_ref, d).astype(BF16)
    c0 = 0
    for o_ref, width in zip(outs, splits):
        step = min(width, 1024)
        for s in range(0, width, step):
            r = jnp.dot(n, w_ref[:, c0 + s:c0 + s + step], preferred_element_type=F32)
            o_ref[0, :, s:s + step] = r.astype(o_ref.dtype)
        c0 += width


def _mod_map(mod):
    return (lambda i, j: (i, 0, 0)) if mod.shape[0] > 1 else (lambda i, j: (0, 0, 0))


def _nmm(x, g, mod, w, splits, dtypes, tm=1024):
    b, l, d = x.shape
    tm = min(tm, l)
    return pl.pallas_call(
        functools.partial(_nmm_kernel, splits=splits, d=d),
        out_shape=[jax.ShapeDtypeStruct((b, l, wd), dt) for wd, dt in zip(splits, dtypes)],
        grid=(b, l // tm),
        in_specs=[pl.BlockSpec((1, tm, d), lambda i, j: (i, j, 0)),
                  _const_spec((1, d)),
                  pl.BlockSpec((1, 1, 3 * d), _mod_map(mod)),
                  _const_spec(w.shape)],
        out_specs=[pl.BlockSpec((1, tm, wd), lambda i, j: (i, j, 0)) for wd in splits],
        compiler_params=_cparams("parallel", "parallel"), name="norm_proj",
    )(x, g.reshape(1, d), mod, w)


def _dft_cos_sin(n):
    k = np.arange(n)
    ang = 2.0 * np.pi * ((k[:, None] * k[None, :]) % n) / n
    return np.cos(ang), np.sin(ang)


def _dft1_kernel(m_ref, u_ref, cos_ref, sin_ref, o_ref, *, n1, tn2):
    ut = jnp.transpose(u_ref[0], (1, 0, 2))
    for j in range(tn2):
        r = jnp.dot(m_ref[...], ut[j], preferred_element_type=F32)
        yr = r[:n1]
        yi = r[n1:]
        cs = cos_ref[j]
        sn = sin_ref[j]
        o_ref[0, 0, j] = (yr * cs + yi * sn).astype(o_ref.dtype)
        o_ref[0, 1, j] = (yi * cs - yr * sn).astype(o_ref.dtype)


def _dft1(u, n1, n2, tn2=16, tc=2048):
    b, l, c = u.shape
    cm, sm = _dft_cos_sin(n1)
    m1 = jnp.asarray(np.concatenate([cm, -sm], axis=0), F32).astype(BF16)
    ang = 2.0 * np.pi * (np.arange(n2)[:, None] * np.arange(n1)[None, :]) / l
    tw_cos = jnp.asarray(np.cos(ang)[:, :, None], F32)
    tw_sin = jnp.asarray(np.sin(ang)[:, :, None], F32)
    return pl.pallas_call(
        functools.partial(_dft1_kernel, n1=n1, tn2=tn2),
        out_shape=jax.ShapeDtypeStruct((b, 2, n2, n1, c), BF16),
        grid=(b, n2 // tn2, c // tc),
        in_specs=[_const_spec((2 * n1, n1)),
                  pl.BlockSpec((1, n1, tn2, tc), lambda i, j, k: (i, 0, j, k)),
                  pl.BlockSpec((tn2, n1, 1), lambda i, j, k: (j, 0, 0)),
                  pl.BlockSpec((tn2, n1, 1), lambda i, j, k: (j, 0, 0))],
        out_specs=pl.BlockSpec((1, 2, tn2, n1, tc), lambda i, j, k: (i, 0, j, 0, k)),
        compiler_params=_cparams("parallel", "parallel", "parallel"), name="dft_stage1",
    )(m1, u.reshape(b, n1, n2, c), tw_cos, tw_sin)


def _dft2_lat_kernel(m_ref, y_ref, xr_ref, xi_ref, *, n2, tk1):
    yr = jnp.transpose(y_ref[0, 0], (1, 0, 2))
    yi = jnp.transpose(y_ref[0, 1], (1, 0, 2))
    xr, xi = [], []
    for j in range(tk1):
        r = jnp.dot(m_ref[...], jnp.concatenate([yr[j], yi[j]], axis=0), preferred_element_type=F32)
        xr.append(r[:n2].astype(xr_ref.dtype))
        xi.append(r[n2:].astype(xi_ref.dtype))
    xr_ref[0] = jnp.transpose(jnp.stack(xr), (1, 0, 2))
    xi_ref[0] = jnp.transpose(jnp.stack(xi), (1, 0, 2))


def _dft2_lat(m, y, tk1=16, tc=1024):
    b, _, n2, n1, c = y.shape
    tk1 = min(tk1, n1)
    return pl.pallas_call(
        functools.partial(_dft2_lat_kernel, n2=n2, tk1=tk1),
        out_shape=[jax.ShapeDtypeStruct((b, n2, n1, c), BF16)] * 2,
        grid=(b, n1 // tk1, c // tc),
        in_specs=[_const_spec(m.shape),
                  pl.BlockSpec((1, 2, n2, tk1, tc), lambda i, j, k: (i, 0, 0, j, k))],
        out_specs=[pl.BlockSpec((1, n2, tk1, tc), lambda i, j, k: (i, 0, j, k))] * 2,
        compiler_params=_cparams("parallel", "parallel", "parallel"), name="dft_stage2_lat",
    )(m, y)


def _dft2_kernel(m_ref, y_ref, xr_ref, xi_ref, *, half):
    r = jnp.dot(m_ref[...], y_ref[0], preferred_element_type=F32)
    xr_ref[0] = r[:half].astype(xr_ref.dtype)
    xi_ref[0] = r[half:].astype(xi_ref.dtype)


def _dft2(m, y, tc=2048):
    b, k, cols = y.shape
    half = m.shape[0] // 2
    return pl.pallas_call(
        functools.partial(_dft2_kernel, half=half),
        out_shape=[jax.ShapeDtypeStruct((b, half, cols), BF16)] * 2,
        grid=(b, cols // tc),
        in_specs=[_const_spec(m.shape),
                  pl.BlockSpec((1, k, tc), lambda i, j: (i, 0, j))],
        out_specs=[pl.BlockSpec((1, half, tc), lambda i, j: (i, 0, j))] * 2,
        compiler_params=_cparams("parallel", "parallel"), name="dft_stage2",
    )(m, y)


def _dft_real_matrix(n):
    cm, sm = _dft_cos_sin(n)
    return jnp.asarray(np.concatenate([cm, -sm], axis=0), F32).astype(BF16)


def _dft_complex_matrix(n):
    cm, sm = _dft_cos_sin(n)
    return jnp.asarray(np.block([[cm, sm], [-sm, cm]]), F32).astype(BF16)


def _grp_kernel(m_ref, w_ref, g_ref):
    g_ref[0] = jnp.dot(m_ref[...], w_ref[0], preferred_element_type=F32, precision=HI).astype(g_ref.dtype)


def _grp(w_grp):
    groups, n, _ = w_grp.shape
    cm, sm = _dft_cos_sin(n)
    m = jnp.asarray(np.concatenate([cm, sm], axis=0), F32)
    return pl.pallas_call(
        _grp_kernel,
        out_shape=jax.ShapeDtypeStruct((groups, 2 * n, n), BF16),
        grid=(groups,),
        in_specs=[_const_spec((2 * n, n)),
                  pl.BlockSpec((1, n, n), lambda g: (g, 0, 0))],
        out_specs=pl.BlockSpec((1, 2 * n, n), lambda g: (g, 0, 0)),
        compiler_params=_cparams("parallel"), name="group_weights",
    )(m, w_grp)


def _fn_out_kernel(xr_ref, xi_ref, z_ref, res_ref, mod_ref, gw_ref, w_ref, o_ref, *, groups, gd, d, scale):
    acc = None
    for g in range(groups):
        sl = slice(g * gd, (g + 1) * gd)
        f = jnp.dot(jnp.concatenate([xr_ref[0, :, sl], xi_ref[0, :, sl]], axis=1), gw_ref[g],
                    preferred_element_type=F32)
        y = (f * scale * _silu(z_ref[0, :, sl].astype(F32))).astype(BF16)
        p = jnp.dot(y, w_ref[sl, :], preferred_element_type=F32)
        acc = p if acc is None else acc + p
    gate = mod_ref[0, :, 2 * d:3 * d]
    o_ref[0] = res_ref[0] + gate * acc


def _fn_out(xr, xi, z, res, mod, gw, w_out, scale, tm=512):
    b, l, c = xr.shape
    d = res.shape[-1]
    tm = min(tm, l)
    groups, _, gd = gw.shape
    tok = lambda wd: pl.BlockSpec((1, tm, wd), lambda i, j: (i, j, 0))
    return pl.pallas_call(
        functools.partial(_fn_out_kernel, groups=groups, gd=gd, d=d, scale=scale),
        out_shape=jax.ShapeDtypeStruct((b, l, d), F32),
        grid=(b, l // tm),
        in_specs=[tok(c), tok(c), tok(c), tok(d),
                  pl.BlockSpec((1, 1, 3 * d), _mod_map(mod)),
                  _const_spec(gw.shape), _const_spec(w_out.shape)],
        out_specs=tok(d),
        compiler_params=_cparams("parallel", "parallel"), name="fourier_out",
    )(xr, xi, z, res, mod, gw, w_out)


def _proj_conv_kernel(xp_ref, xm_ref, xn_ref, g_ref, mod_ref, w_ref, wab_ref, abp_ref, cw_ref, sh_ref,
                      q_ref, k_ref, v_ref, z_ref, gb_ref, *, d, cols, tr, rp, pr, halo, nheads, dk, dv, cb):
    i = pl.program_id(1)
    last = pl.num_programs(1) - 1
    if halo:
        n_prev = (_norm_mod(xp_ref[0], g_ref, mod_ref, d) * (i > 0).astype(F32)).astype(BF16)
    n_main = _norm_mod(xm_ref[0], g_ref, mod_ref, d).astype(BF16)
    if halo:
        n_next = (_norm_mod(xn_ref[0], g_ref, mod_ref, d) * (i < last).astype(F32)).astype(BF16)
        ne = jnp.concatenate([n_prev, n_main, n_next], axis=0)
    qk_w = nheads * dk
    v_w = nheads * dv
    z0 = 2 * qk_w + v_w
    n_in = (tr + 2) * cols // pr
    n_out = tr * cols // rp

    def project(c0, p):
        if halo:
            rows = ne[p * pr:(p + 1) * pr]
        elif 0 < p < n_in - 1:
            rows = n_main[(p - 1) * pr:p * pr]
        else:
            return jnp.zeros((pr, cb), F32)
        return jnp.dot(rows, w_ref[:, c0:c0 + cb], preferred_element_type=F32)

    pieces = [project(0, p) for p in range(n_in)]
    zstep = v_w * cb // z0
    for c0 in range(0, z0, cb):
        nxt = []
        for j in range(n_out):
            if c0 + cb < z0:
                nxt += [project(c0 + cb, p) for p in range(len(nxt), (j + 1) * n_in // n_out)]
            if j == n_out // 2:
                zs = c0 // cb * zstep
                z_ref[0, :, zs:zs + zstep] = jnp.dot(
                    n_main, w_ref[:, z0 + zs:z0 + zs + zstep], preferred_element_type=F32).astype(z_ref.dtype)
            lo, hi = j * rp, j * rp + rp + 2 * cols
            p_lo, p_hi = lo // pr, (hi - 1) // pr
            e = jnp.concatenate(pieces[p_lo:p_hi + 1], axis=0)[lo - p_lo * pr:hi - p_lo * pr]
            part = []
            for dc in range(3):
                acc = None
                for dr in range(3):
                    term = e[dr * cols:dr * cols + rp] * cw_ref[dr, dc:dc + 1, c0:c0 + cb]
                    acc = term if acc is None else acc + term
                part.append(acc)
            y = _silu(jnp.dot(sh_ref[...], jnp.concatenate(part, axis=0).astype(BF16), preferred_element_type=F32))
            rows = slice(j * rp, (j + 1) * rp)
            if c0 < 2 * qk_w:
                o_ref, scale, base = (q_ref, dk ** -0.5, c0) if c0 < qk_w else (k_ref, 1.0, c0 - qk_w)
                for h in range(cb // dk):
                    yh = y[:, h * dk:(h + 1) * dk]
                    yh = yh * lax.rsqrt(jnp.sum(yh * yh, axis=-1, keepdims=True) + EPS) * scale
                    o_ref[0, base // dk + h, rows, :] = yh.astype(o_ref.dtype)
            else:
                base = c0 - 2 * qk_w
                for h in range(cb // dv):
                    v_ref[0, base // dv + h, rows, :] = y[:, h * dv:(h + 1) * dv].astype(v_ref.dtype)
        pieces = nxt
    r = jnp.dot(n_main, wab_ref[...], preferred_element_type=F32)
    t = r + abp_ref[1:2, :]
    softplus = jnp.maximum(t, 0.0) + jnp.log(1.0 + jnp.exp(-jnp.abs(t)))
    lane = lax.broadcasted_iota(jnp.int32, r.shape, 1)
    gb_ref[0] = jnp.where(lane < 2 * nheads, (-LOG2E) * jnp.exp(abp_ref[0:1, :]) * softplus, jax.nn.sigmoid(r))


def _proj_conv(x, g, mod, w, w_ab, abp, conv_w, rows, cols, cb=512):
    b, l, d = x.shape
    tr = min(rows, max(1, 512 // cols))
    n = tr * cols
    rp = cols * min(tr, 2)
    halo = rows > tr or rp != cols
    half = (tr + 2) * cols // 2
    pr = half if halo and half % 16 == 0 else rp
    t = np.arange(rp)
    left = (t[:, None] - 1 == t[None, :]) & (t[:, None] % cols != 0)
    right = (t[:, None] + 1 == t[None, :]) & (t[:, None] % cols != cols - 1)
    shift = jnp.asarray(np.concatenate([left, np.eye(rp, dtype=bool), right], axis=1), F32).astype(BF16)
    heads = lambda wd: pl.BlockSpec((1, DN_HEADS, n, wd), lambda bi, i: (bi, 0, i, 0))
    tok = lambda wd: pl.BlockSpec((1, n, wd), lambda bi, i: (bi, i, 0))
    return pl.pallas_call(
        functools.partial(_proj_conv_kernel, d=d, cols=cols, tr=tr, rp=rp, pr=pr, halo=halo,
                          nheads=DN_HEADS, dk=DN_DK, dv=DN_DV, cb=cb),
        out_shape=[jax.ShapeDtypeStruct((b, DN_HEADS, l, DN_DK), BF16),
                   jax.ShapeDtypeStruct((b, DN_HEADS, l, DN_DK), BF16),
                   jax.ShapeDtypeStruct((b, DN_HEADS, l, DN_DV), BF16),
                   jax.ShapeDtypeStruct((b, l, DN_HEADS * DN_DV), BF16),
                   jax.ShapeDtypeStruct((b, l, GB_LANES), F32)],
        grid=(b, rows // tr),
        in_specs=[pl.BlockSpec((1, cols, d), lambda bi, i: (bi, jnp.maximum(i * tr - 1, 0), 0)),
                  pl.BlockSpec((1, n, d), lambda bi, i: (bi, i, 0)),
                  pl.BlockSpec((1, cols, d), lambda bi, i: (bi, jnp.minimum((i + 1) * tr, rows - 1), 0)),
                  _const_spec((1, d)),
                  pl.BlockSpec((1, 1, 3 * d), _mod_map(mod)),
                  _const_spec(w.shape), _const_spec(w_ab.shape), _const_spec(abp.shape), _const_spec(conv_w.shape),
                  _const_spec(shift.shape)],
        out_specs=[heads(DN_DK), heads(DN_DK), heads(DN_DV), tok(DN_HEADS * DN_DV), tok(GB_LANES)],
        compiler_params=_cparams("parallel", "parallel"), name="proj_conv",
    )(x, x, x, g.reshape(1, d), mod, w, w_ab, abp, conv_w, shift)


def _delta_kernel(qa_ref, ka_ref, va_ref, ga_ref, qb_ref, kb_ref, vb_ref, gb_ref, s0_ref,
                  of_ref, ob_ref, sfin_ref, s_scr, sb_scr, *, nheads, cs):
    n = pl.program_id(1)

    @pl.when(n == 0)
    def _():
        s_scr[...] = s0_ref[0]
        sb_scr[...] = s0_ref[0].astype(sb_scr.dtype)

    ri = lax.broadcasted_iota(jnp.int32, (cs, cs), 0)
    ci = lax.broadcasted_iota(jnp.int32, (cs, cs), 1)
    eye = ri == ci
    xor = ri ^ ci
    nt = (((1,), (1,)), ((), ()))
    levels = int(math.log2(cs))
    chains = []
    for d, (q_ref, k_ref, v_ref, g_ref, o_ref) in enumerate(((qa_ref, ka_ref, va_ref, ga_ref, of_ref),
                                                              (qb_ref, kb_ref, vb_ref, gb_ref, ob_ref))):
        incl = (ri >= ci) if d == 0 else (ri <= ci)
        strict = (ri > ci) if d == 0 else (ri < ci)
        gbv = g_ref[0]
        hi = gbv.astype(BF16)
        rem = gbv - hi.astype(F32)
        mid = rem.astype(BF16)
        t01 = jnp.where(incl, 1.0, 0.0).astype(BF16)
        cum = sum(jnp.dot(t01, p, preferred_element_type=F32)
                  for p in (hi, mid, (rem - mid.astype(F32)).astype(BF16)))
        cum_t = cum.T
        last = cs - 1 if d == 0 else 0
        for hd in range(nheads):
            qb = q_ref[0, hd]
            kb = k_ref[0, hd]
            k = kb.astype(F32)
            v = v_ref[0, hd].astype(F32)
            kk = lax.dot_general(kb, kb, nt, preferred_element_type=F32)
            qk = lax.dot_general(qb, kb, nt, preferred_element_type=F32)
            lg = d * nheads + hd
            lb = (2 + d) * nheads + hd
            gc_col = cum[:, lg:lg + 1]
            gc_row = cum_t[lg:lg + 1, :]
            g_tot = cum[last:last + 1, lg:lg + 1]
            b_col = gbv[:, lb:lb + 1]
            ex = jnp.exp2(gc_col - gc_row)
            egc = jnp.exp2(gc_col)
            chains.append(dict(
                hd=hd, d=d, o_ref=o_ref, eg=jnp.exp2(g_tot),
                qh=(qb.astype(F32) * egc).astype(BF16),
                qkd=jnp.where(incl, qk * ex, 0.0).astype(BF16),
                ktt=(k.T * jnp.exp2(g_tot - gc_row)).astype(BF16),
                rhs=jnp.concatenate([v * b_col, k * (b_col * egc)], axis=1).astype(BF16),
                neg_off=jnp.where(strict, kk * (-b_col) * ex, 0.0)))
    dmats = [jnp.where(eye, 1.0, jnp.where(xor == 1, ch["neg_off"], 0.0)) for ch in chains]
    neg_offs = [ch["neg_off"].astype(BF16) for ch in chains]
    for lv in range(1, levels):
        sel = jnp.where((xor >> lv) == 1, 1.0, 0.0).astype(BF16)
        dmbs = [dm.astype(BF16) for dm in dmats]
        des = [jnp.dot(db, no * sel, preferred_element_type=F32) for db, no in zip(dmbs, neg_offs)]
        dmats = [dm + jnp.dot(de.astype(BF16), db, preferred_element_type=F32)
                 for dm, de, db in zip(dmats, des, dmbs)]
    sols = [jnp.dot(dm.astype(BF16), ch["rhs"], preferred_element_type=F32) for dm, ch in zip(dmats, chains)]
    dv = of_ref.shape[-1]
    r1s = [jnp.dot(jnp.concatenate([sol[:, dv:].astype(BF16), ch["qh"]], axis=0), sb_scr[ch["hd"], ch["d"]],
                   preferred_element_type=F32) for ch, sol in zip(chains, sols)]
    vns = [(sol[:, :dv] - r1[:cs]).astype(BF16) for sol, r1 in zip(sols, r1s)]
    r2s = [jnp.dot(jnp.concatenate([ch["qkd"], ch["ktt"]], axis=0), vn, preferred_element_type=F32)
           for ch, vn in zip(chains, vns)]
    for ch, r1, r2 in zip(chains, r1s, r2s):
        hd, d = ch["hd"], ch["d"]
        ch["o_ref"][0, hd] = (r1[cs:] + r2[:cs]).astype(of_ref.dtype)
        s_new = s_scr[hd, d] * ch["eg"] + r2[cs:]
        s_scr[hd, d] = s_new
        sb_scr[hd, d] = s_new.astype(sb_scr.dtype)

    @pl.when(n == pl.num_programs(1) - 1)
    def _():
        sfin_ref[0] = s_scr[...]


def _delta(q, k, v, gb, s0, cs=SCAN_CHUNK):
    b, nh, l, dk = q.shape
    dv = v.shape[-1]
    nc = l // cs
    fwd = lambda wd: pl.BlockSpec((1, nh, cs, wd), lambda bi, n: (bi, 0, n, 0))
    bwd = lambda wd: pl.BlockSpec((1, nh, cs, wd), lambda bi, n: (bi, 0, nc - 1 - n, 0))
    st = pl.BlockSpec((1, nh, 2, dk, dv), lambda bi, n: (bi, 0, 0, 0, 0))
    return pl.pallas_call(
        functools.partial(_delta_kernel, nheads=nh, cs=cs),
        out_shape=[jax.ShapeDtypeStruct((b, nh, l, dv), BF16),
                   jax.ShapeDtypeStruct((b, nh, l, dv), BF16),
                   jax.ShapeDtypeStruct((b, nh, 2, dk, dv), F32)],
        grid=(b, nc),
        in_specs=[fwd(dk), fwd(dk), fwd(dv), pl.BlockSpec((1, cs, GB_LANES), lambda bi, n: (bi, n, 0)),
                  bwd(dk), bwd(dk), bwd(dv), pl.BlockSpec((1, cs, GB_LANES), lambda bi, n: (bi, nc - 1 - n, 0)),
                  st],
        out_specs=[fwd(dv), bwd(dv), st],
        scratch_shapes=[pltpu.VMEM((nh, 2, dk, dv), F32), pltpu.VMEM((nh, 2, dk, dv), BF16)],
        compiler_params=_cparams("parallel", "arbitrary"), name="delta_rule",
    )(q, k, v, gb, q, k, v, gb, s0)


def _dn_out_kernel(of_ref, ob_ref, z_ref, res_ref, mod_ref, ng_ref, w_ref, fg_ref, o_ref, *, nheads, dv, d):
    acc = None
    for hd in range(nheads):
        sl = slice(hd * dv, (hd + 1) * dv)
        o = of_ref[0, hd].astype(F32) + ob_ref[0, hd].astype(F32)
        o = o * lax.rsqrt(jnp.mean(o * o, axis=-1, keepdims=True) + EPS) * ng_ref[...]
        y = (o * _silu(z_ref[0, :, sl].astype(F32))).astype(BF16)
        p = jnp.dot(y, w_ref[sl, :], preferred_element_type=F32)
        acc = p if acc is None else acc + p
    gate = mod_ref[0, :, 2 * d:3 * d]
    hn = res_ref[0] + gate * acc
    o_ref[0] = hn * lax.rsqrt(jnp.mean(hn * hn, axis=-1, keepdims=True) + EPS) * fg_ref[...]


def _dn_out(o_f, o_b, z, res, mod, norm_g, w_out, final_g, tm=512):
    b, nh, l, dv = o_f.shape
    d = res.shape[-1]
    tm = min(tm, l)
    tok = lambda wd: pl.BlockSpec((1, tm, wd), lambda i, j: (i, j, 0))
    hs = pl.BlockSpec((1, nh, tm, dv), lambda i, j: (i, 0, j, 0))
    return pl.pallas_call(
        functools.partial(_dn_out_kernel, nheads=nh, dv=dv, d=d),
        out_shape=jax.ShapeDtypeStruct((b, l, d), F32),
        grid=(b, l // tm),
        in_specs=[hs, hs, tok(nh * dv), tok(d),
                  pl.BlockSpec((1, 1, 3 * d), lambda i, j: (i, 0, 0)),
                  _const_spec((1, dv)), _const_spec(w_out.shape), _const_spec((1, d))],
        out_specs=tok(d),
        compiler_params=_cparams("parallel", "parallel"), name="delta_out",
    )(o_f, o_b, z, res, mod, norm_g.reshape(1, dv), w_out, final_g.reshape(1, d))


def _fourier_layer(h_lat, h_ctx, norm_g, mod_lat, mod_ctx, w_in, w_grp, w_out):
    b, l, _ = h_lat.shape
    lc = h_ctx.shape[1]
    width = w_in.shape[1] // 2
    gd = w_grp.shape[-1]
    w_in = w_in.astype(BF16)
    w_out = w_out.astype(BF16)
    gw = _grp(w_grp)

    u, z = _nmm(h_lat, norm_g, mod_lat, w_in, (width, width), (BF16, BF16))
    n2 = DFT_N2
    n1 = l // n2
    y = _dft1(u, n1, n2)
    xr, xi = _dft2_lat(_dft_complex_matrix(n2), y)
    xr = xr.reshape(b, l, width)
    xi = xi.reshape(b, l, width)
    out_lat = _fn_out(xr, xi, z, h_lat, mod_lat, gw, w_out, 1.0 / math.sqrt(l * gd))

    uc, zc = _nmm(h_ctx, norm_g, mod_ctx, w_in, (width, width), (BF16, BF16))
    xrc, xic = _dft2(_dft_real_matrix(lc), uc)
    out_ctx = _fn_out(xrc, xic, zc, h_ctx, mod_ctx, gw, w_out, 1.0 / math.sqrt(lc * gd))
    return out_lat, out_ctx


def _delta_layer_final(h_lat, h_ctx, norm_g, mod_lat, mod_ctx, w_in, conv_w, a_log, dt_bias, dn_norm_g, w_out,
                       final_g):
    b, l, d = h_lat.shape
    lc = h_ctx.shape[1]
    main_w = 2 * DN_HEADS * DN_DK + 2 * DN_HEADS * DN_DV
    n_ab = 4 * DN_HEADS
    w_main = w_in.astype(BF16)
    w_ab = jnp.pad(w_in[:, main_w:], ((0, 0), (0, GB_LANES - n_ab))).astype(BF16)
    pad = jnp.zeros((GB_LANES - 2 * DN_HEADS,), F32)
    abp = jnp.stack([jnp.concatenate([a_log.reshape(-1).astype(F32), pad]),
                     jnp.concatenate([dt_bias.reshape(-1).astype(F32), pad])])
    qc, kc, vc, _, gbc = _proj_conv(h_ctx, norm_g, mod_ctx, w_main, w_ab, abp, conv_w, 1, lc)
    ql, kl, vl, zl, gbl = _proj_conv(h_lat, norm_g, mod_lat, w_main, w_ab, abp, conv_w, l // GRID_W, GRID_W)
    s0 = jnp.zeros((b, DN_HEADS, 2, DN_DK, DN_DV), F32)
    _, _, s_ctx = _delta(qc, kc, vc, gbc, s0)
    o_f, o_b, _ = _delta(ql, kl, vl, gbl, s_ctx)
    return _dn_out(o_f, o_b, zl, h_lat, mod_lat, dn_norm_g, w_out.astype(BF16), final_g)


def kernel(x, c, ctx, c_ctx, mod_w, mod_b, norm_g, final_g, fn_w_in, fn_w_grp, fn_w_out,
           dn_w_in, dn_conv, dn_a_log, dn_dt_bias, dn_norm_g, dn_w_out):
    b, l, d = x.shape
    cond = jnp.concatenate([c, c_ctx[None, :], jnp.zeros((8 - b - 1, d), c.dtype)], axis=0)
    mods = _mods(cond, mod_w, mod_b)
    mod_lat = [mods[i, :b].reshape(b, 1, 3 * d) for i in range(2)]
    mod_ctx = [mods[i, b:b + 1].reshape(1, 1, 3 * d) for i in range(2)]
    h_lat, h_ctx = _fourier_layer(x, ctx, norm_g[0], mod_lat[0], mod_ctx[0], fn_w_in[0], fn_w_grp[0], fn_w_out[0])
    return _delta_layer_final(h_lat, h_ctx, norm_g[1], mod_lat[1], mod_ctx[1], dn_w_in[0], dn_conv[0],
                              dn_a_log[0], dn_dt_bias[0], dn_norm_g[0], dn_w_out[0], final_g)
```

```python
import functools
import math

import numpy as np
import jax
import jax.numpy as jnp
from jax import lax
from jax.experimental import pallas as pl
from jax.experimental.pallas import tpu as pltpu

F32 = jnp.float32
BF16 = jnp.bfloat16
HI = lax.Precision.HIGHEST
EPS = 1e-6
LOG2E = 1.4426950408889634

GRID_W = 64
FN_GROUPS = 8
DN_HEADS = 8
DN_DK = 128
DN_DV = 256
SCAN_CHUNK = 128
DFT_N2 = 128
GB_LANES = 128
VMEM_LIMIT_BYTES = 48 * 1024 * 1024


def _cparams(*sem):
    return pltpu.CompilerParams(dimension_semantics=sem, vmem_limit_bytes=VMEM_LIMIT_BYTES)


def _const_spec(shape):
    zeros = (0,) * len(shape)
    return pl.BlockSpec(shape, lambda *_: zeros, pipeline_mode=pl.Buffered(1))


def _silu(v):
    return v * jax.nn.sigmoid(v)


def _mods_kernel(cond_ref, w_ref, b_ref, o_ref):
    s = _silu(cond_ref[...])
    o_ref[0] = jnp.dot(s, w_ref[0], preferred_element_type=F32, precision=HI) + b_ref[0]


def _mods(cond, mod_w, mod_b):
    depth, d, d3 = mod_w.shape
    rows = cond.shape[0]
    tn = 1024
    return pl.pallas_call(
        _mods_kernel,
        out_shape=jax.ShapeDtypeStruct((depth, rows, d3), F32),
        grid=(depth, d3 // tn),
        in_specs=[pl.BlockSpec((rows, d), lambda i, j: (0, 0)),
                  pl.BlockSpec((1, d, tn), lambda i, j: (i, 0, j)),
                  pl.BlockSpec((1, 1, tn), lambda i, j: (i, 0, j))],
        out_specs=pl.BlockSpec((1, rows, tn), lambda i, j: (i, 0, j)),
        compiler_params=_cparams("parallel", "parallel"), name="adaln_mods",
    )(cond, mod_w, mod_b.reshape(depth, 1, d3))


def _norm_mod(x, g_ref, mod_ref, d):
    y = x * lax.rsqrt(jnp.mean(x * x, axis=-1, keepdims=True) + EPS) * g_ref[...]
    return y * (1.0 + mod_ref[0, :, d:2 * d]) + mod_ref[0, :, 0:d]


def _nmm_kernel(x_ref, g_ref, mod_ref, w_ref, *outs, splits, d):
    n = _norm_mod(x_ref[0], g_ref, mod_ref, d).astype(BF16)
    c0 = 0
    for o_ref, width in zip(outs, splits):
        step = min(width, 1024)
        for s in range(0, width, step):
            r = jnp.dot(n, w_ref[:, c0 + s:c0 + s + step], preferred_element_type=F32)
            o_ref[0, :, s:s + step] = r.astype(o_ref.dtype)
        c0 += width


def _mod_map(mod):
    return (lambda i, j: (i, 0, 0)) if mod.shape[0] > 1 else (lambda i, j: (0, 0, 0))


def _nmm(x, g, mod, w, splits, dtypes, tm=1024):
    b, l, d = x.shape
    tm = min(tm, l)
    return pl.pallas_call(
        functools.partial(_nmm_kernel, splits=splits, d=d),
        out_shape=[jax.ShapeDtypeStruct((b, l, wd), dt) for wd, dt in zip(splits, dtypes)],
        grid=(b, l // tm),
        in_specs=[pl.BlockSpec((1, tm, d), lambda i, j: (i, j, 0)),
                  _const_spec((1, d)),
                  pl.BlockSpec((1, 1, 3 * d), _mod_map(mod)),
                  _const_spec(w.shape)],
        out_specs=[pl.BlockSpec((1, tm, wd), lambda i, j: (i, j, 0)) for wd in splits],
        compiler_params=_cparams("parallel", "parallel"), name="norm_proj",
    )(x, g.reshape(1, d), mod, w)


def _dft_cos_sin(n):
    k = np.arange(n)
    ang = 2.0 * np.pi * ((k[:, None] * k[None, :]) % n) / n
    return np.cos(ang), np.sin(ang)


def _dft1_kernel(m_ref, u_ref, cos_ref, sin_ref, o_ref, *, n1, tn2):
    ut = jnp.transpose(u_ref[0], (1, 0, 2))
    for j in range(tn2):
        r = jnp.dot(m_ref[...], ut[j], preferred_element_type=F32)
        yr = r[:n1]
        yi = r[n1:]
        cs = cos_ref[j]
        sn = sin_ref[j]
        o_ref[0, 0, j] = (yr * cs + yi * sn).astype(o_ref.dtype)
        o_ref[0, 1, j] = (yi * cs - yr * sn).astype(o_ref.dtype)


def _dft1(u, n1, n2, tn2=16, tc=2048):
    b, l, c = u.shape
    cm, sm = _dft_cos_sin(n1)
    m1 = jnp.asarray(np.concatenate([cm, -sm], axis=0), F32).astype(BF16)
    ang = 2.0 * np.pi * (np.arange(n2)[:, None] * np.arange(n1)[None, :]) / l
    tw_cos = jnp.asarray(np.cos(ang)[:, :, None], F32)
    tw_sin = jnp.asarray(np.sin(ang)[:, :, None], F32)
    return pl.pallas_call(
        functools.partial(_dft1_kernel, n1=n1, tn2=tn2),
        out_shape=jax.ShapeDtypeStruct((b, 2, n2, n1, c), BF16),
        grid=(b, n2 // tn2, c // tc),
        in_specs=[_const_spec((2 * n1, n1)),
                  pl.BlockSpec((1, n1, tn2, tc), lambda i, j, k: (i, 0, j, k)),
                  pl.BlockSpec((tn2, n1, 1), lambda i, j, k: (j, 0, 0)),
                  pl.BlockSpec((tn2, n1, 1), lambda i, j, k: (j, 0, 0))],
        out_specs=pl.BlockSpec((1, 2, tn2, n1, tc), lambda i, j, k: (i, 0, j, 0, k)),
        compiler_params=_cparams("parallel", "parallel", "parallel"), name="dft_stage1",
    )(m1, u.reshape(b, n1, n2, c), tw_cos, tw_sin)


def _dft2_lat_kernel(m_ref, y_ref, xr_ref, xi_ref, *, n2, tk1):
    yr = jnp.transpose(y_ref[0, 0], (1, 0, 2))
    yi = jnp.transpose(y_ref[0, 1], (1, 0, 2))
    xr, xi = [], []
    for j in range(tk1):
        r = jnp.dot(m_ref[...], jnp.concatenate([yr[j], yi[j]], axis=0), preferred_element_type=F32)
        xr.append(r[:n2].astype(xr_ref.dtype))
        xi.append(r[n2:].astype(xi_ref.dtype))
    xr_ref[0] = jnp.transpose(jnp.stack(xr), (1, 0, 2))
    xi_ref[0] = jnp.transpose(jnp.stack(xi), (1, 0, 2))


def _dft2_lat(m, y, tk1=16, tc=1024):
    b, _, n2, n1, c = y.shape
    tk1 = min(tk1, n1)
    return pl.pallas_call(
        functools.partial(_dft2_lat_kernel, n2=n2, tk1=tk1),
        out_shape=[jax.ShapeDtypeStruct((b, n2, n1, c), BF16)] * 2,
        grid=(b, n1 // tk1, c // tc),
        in_specs=[_const_spec(m.shape),
                  pl.BlockSpec((1, 2, n2, tk1, tc), lambda i, j, k: (i, 0, 0, j, k))],
        out_specs=[pl.BlockSpec((1, n2, tk1, tc), lambda i, j, k: (i, 0, j, k))] * 2,
        compiler_params=_cparams("parallel", "parallel", "parallel"), name="dft_stage2_lat",
    )(m, y)


def _dft2_kernel(m_ref, y_ref, xr_ref, xi_ref, *, half):
    r = jnp.dot(m_ref[...], y_ref[0], preferred_element_type=F32)
    xr_ref[0] = r[:half].astype(xr_ref.dtype)
    xi_ref[0] = r[half:].astype(xi_ref.dtype)


def _dft2(m, y, tc=2048):
    b, k, cols = y.shape
    half = m.shape[0] // 2
    return pl.pallas_call(
        functools.partial(_dft2_kernel, half=half),
        out_shape=[jax.ShapeDtypeStruct((b, half, cols), BF16)] * 2,
        grid=(b, cols // tc),
        in_specs=[_const_spec(m.shape),
                  pl.BlockSpec((1, k, tc), lambda i, j: (i, 0, j))],
        out_specs=[pl.BlockSpec((1, half, tc), lambda i, j: (i, 0, j))] * 2,
        compiler_params=_cparams("parallel", "parallel"), name="dft_stage2",
    )(m, y)


def _dft_real_matrix(n):
    cm, sm = _dft_cos_sin(n)
    return jnp.asarray(np.concatenate([cm, -sm], axis=0), F32).astype(BF16)


def _dft_complex_matrix(n):
    cm, sm = _dft_cos_sin(n)
    return jnp.asarray(np.block([[cm, sm], [-sm, cm]]), F32).astype(BF16)


def _grp_kernel(m_ref, w_ref, g_ref):
    g_ref[0] = jnp.dot(m_ref[...], w_ref[0], preferred_element_type=F32, precision=HI).astype(g_ref.dtype)


def _grp(w_grp):
    groups, n, _ = w_grp.shape
    cm, sm = _dft_cos_sin(n)
    m = jnp.asarray(np.concatenate([cm, sm], axis=0), F32)
    return pl.pallas_call(
        _grp_kernel,
        out_shape=jax.ShapeDtypeStruct((groups, 2 * n, n), BF16),
        grid=(groups,),
        in_specs=[_const_spec((2 * n, n)),
                  pl.BlockSpec((1, n, n), lambda g: (g, 0, 0))],
        out_specs=pl.BlockSpec((1, 2 * n, n), lambda g: (g, 0, 0)),
        compiler_params=_cparams("parallel"), name="group_weights",
    )(m, w_grp)


def _fn_out_kernel(xr_ref, xi_ref, z_ref, res_ref, mod_ref, gw_ref, w_ref, o_ref, *, groups, gd, d, scale):
    ys = []
    for g in range(groups):
        sl = slice(g * gd, (g + 1) * gd)
        f = jnp.dot(jnp.concatenate([xr_ref[0, :, sl], xi_ref[0, :, sl]], axis=1), gw_ref[g],
                    preferred_element_type=F32)
        ys.append((f * scale * _silu(z_ref[0, :, sl].astype(F32))).astype(BF16))
    acc = jnp.dot(jnp.concatenate(ys, axis=1), w_ref[...], preferred_element_type=F32)
    gate = mod_ref[0, :, 2 * d:3 * d]
    o_ref[0] = res_ref[0] + gate * acc


def _fn_out(xr, xi, z, res, mod, gw, w_out, scale, tm=512):
    b, l, c = xr.shape
    d = res.shape[-1]
    tm = min(tm, l)
    groups, _, gd = gw.shape
    tok = lambda wd: pl.BlockSpec((1, tm, wd), lambda i, j: (i, j, 0))
    return pl.pallas_call(
        functools.partial(_fn_out_kernel, groups=groups, gd=gd, d=d, scale=scale),
        out_shape=jax.ShapeDtypeStruct((b, l, d), F32),
        grid=(b, l // tm),
        in_specs=[tok(c), tok(c), tok(c), tok(d),
                  pl.BlockSpec((1, 1, 3 * d), _mod_map(mod)),
                  _const_spec(gw.shape), _const_spec(w_out.shape)],
        out_specs=tok(d),
        compiler_params=_cparams("parallel", "parallel"), name="fourier_out",
    )(xr, xi, z, res, mod, gw, w_out)


def _proj_conv_kernel(xp_ref, xm_ref, xn_ref, g_ref, mod_ref, w_ref, wab_ref, abp_ref, cw_ref, sh_ref,
                      q_ref, k_ref, v_ref, z_ref, gb_ref, *, d, cols, tr, rp, pr, halo, nheads, dk, dv, cb):
    i = pl.program_id(1)
    last = pl.num_programs(1) - 1
    if halo:
        n_prev = (_norm_mod(xp_ref[0], g_ref, mod_ref, d) * (i > 0).astype(F32)).astype(BF16)
    n_main = _norm_mod(xm_ref[0], g_ref, mod_ref, d).astype(BF16)
    if halo:
        n_next = (_norm_mod(xn_ref[0], g_ref, mod_ref, d) * (i < last).astype(F32)).astype(BF16)
        ne = jnp.concatenate([n_prev, n_main, n_next], axis=0)
    qk_w = nheads * dk
    v_w = nheads * dv
    z0 = 2 * qk_w + v_w
    n_in = (tr + 2) * cols // pr
    n_out = tr * cols // rp

    def project(c0, p):
        if halo:
            rows = ne[p * pr:(p + 1) * pr]
        elif 0 < p < n_in - 1:
            rows = n_main[(p - 1) * pr:p * pr]
        else:
            return jnp.zeros((pr, cb), F32)
        return jnp.dot(rows, w_ref[:, c0:c0 + cb], preferred_element_type=F32)

    pieces = [project(0, p) for p in range(n_in)]
    zstep = v_w * cb // z0
    for c0 in range(0, z0, cb):
        nxt = []
        for j in range(n_out):
            if c0 + cb < z0:
                nxt += [project(c0 + cb, p) for p in range(len(nxt), (j + 1) * n_in // n_out)]
            if j == n_out // 2:
                zs = c0 // cb * zstep
                z_ref[0, :, zs:zs + zstep] = jnp.dot(
                    n_main, w_ref[:, z0 + zs:z0 + zs + zstep], preferred_element_type=F32).astype(z_ref.dtype)
            lo, hi = j * rp, j * rp + rp + 2 * cols
            p_lo, p_hi = lo // pr, (hi - 1) // pr
            e = jnp.concatenate(pieces[p_lo:p_hi + 1], axis=0)[lo - p_lo * pr:hi - p_lo * pr]
            part = []
            for dc in range(3):
                acc = None
                for dr in range(3):
                    term = e[dr * cols:dr * cols + rp] * cw_ref[dr, dc:dc + 1, c0:c0 + cb]
                    acc = term if acc is None else acc + term
                part.append(acc)
            y = _silu(jnp.dot(sh_ref[...], jnp.concatenate(part, axis=0).astype(BF16), preferred_element_type=F32))
            rows = slice(j * rp, (j + 1) * rp)
            if c0 < 2 * qk_w:
                o_ref, scale, base = (q_ref, dk ** -0.5, c0) if c0 < qk_w else (k_ref, 1.0, c0 - qk_w)
                for h in range(cb // dk):
                    yh = y[:, h * dk:(h + 1) * dk]
                    yh = yh * lax.rsqrt(jnp.sum(yh * yh, axis=-1, keepdims=True) + EPS) * scale
                    o_ref[0, base // dk + h, rows, :] = yh.astype(o_ref.dtype)
            else:
                base = c0 - 2 * qk_w
                for h in range(cb // dv):
                    v_ref[0, base // dv + h, rows, :] = y[:, h * dv:(h + 1) * dv].astype(v_ref.dtype)
        pieces = nxt
    r = jnp.dot(n_main, wab_ref[...], preferred_element_type=F32)
    t = r + abp_ref[1:2, :]
    softplus = jnp.maximum(t, 0.0) + jnp.log(1.0 + jnp.exp(-jnp.abs(t)))
    lane = lax.broadcasted_iota(jnp.int32, r.shape, 1)
    gb_ref[0] = jnp.where(lane < 2 * nheads, (-LOG2E) * jnp.exp(abp_ref[0:1, :]) * softplus, jax.nn.sigmoid(r))


def _proj_conv(x, g, mod, w, w_ab, abp, conv_w, rows, cols, cb=512):
    b, l, d = x.shape
    tr = min(rows, max(1, 512 // cols))
    n = tr * cols
    rp = cols * min(tr, 2)
    halo = rows > tr or rp != cols
    half = (tr + 2) * cols // 2
    pr = half if halo and half % 16 == 0 else rp
    t = np.arange(rp)
    left = (t[:, None] - 1 == t[None, :]) & (t[:, None] % cols != 0)
    right = (t[:, None] + 1 == t[None, :]) & (t[:, None] % cols != cols - 1)
    shift = jnp.asarray(np.concatenate([left, np.eye(rp, dtype=bool), right], axis=1), F32).astype(BF16)
    heads = lambda wd: pl.BlockSpec((1, DN_HEADS, n, wd), lambda bi, i: (bi, 0, i, 0))
    tok = lambda wd: pl.BlockSpec((1, n, wd), lambda bi, i: (bi, i, 0))
    return pl.pallas_call(
        functools.partial(_proj_conv_kernel, d=d, cols=cols, tr=tr, rp=rp, pr=pr, halo=halo,
                          nheads=DN_HEADS, dk=DN_DK, dv=DN_DV, cb=cb),
        out_shape=[jax.ShapeDtypeStruct((b, DN_HEADS, l, DN_DK), BF16),
                   jax.ShapeDtypeStruct((b, DN_HEADS, l, DN_DK), BF16),
                   jax.ShapeDtypeStruct((b, DN_HEADS, l, DN_DV), BF16),
                   jax.ShapeDtypeStruct((b, l, DN_HEADS * DN_DV), BF16),
                   jax.ShapeDtypeStruct((b, l, GB_LANES), F32)],
        grid=(b, rows // tr),
        in_specs=[pl.BlockSpec((1, cols, d), lambda bi, i: (bi, jnp.maximum(i * tr - 1, 0), 0)),
                  pl.BlockSpec((1, n, d), lambda bi, i: (bi, i, 0)),
                  pl.BlockSpec((1, cols, d), lambda bi, i: (bi, jnp.minimum((i + 1) * tr, rows - 1), 0)),
                  _const_spec((1, d)),
                  pl.BlockSpec((1, 1, 3 * d), _mod_map(mod)),
                  _const_spec(w.shape), _const_spec(w_ab.shape), _const_spec(abp.shape), _const_spec(conv_w.shape),
                  _const_spec(shift.shape)],
        out_specs=[heads(DN_DK), heads(DN_DK), heads(DN_DV), tok(DN_HEADS * DN_DV), tok(GB_LANES)],
        compiler_params=_cparams("parallel", "parallel"), name="proj_conv",
    )(x, x, x, g.reshape(1, d), mod, w, w_ab, abp, conv_w, shift)


def _delta_kernel(qa_ref, ka_ref, va_ref, ga_ref, qb_ref, kb_ref, vb_ref, gb_ref, s0_ref,
                  of_ref, ob_ref, sfin_ref, s_scr, sb_scr, *, nheads, cs):
    n = pl.program_id(1)

    @pl.when(n == 0)
    def _():
        s_scr[...] = s0_ref[0]
        sb_scr[...] = s0_ref[0].astype(sb_scr.dtype)

    ri = lax.broadcasted_iota(jnp.int32, (cs, cs), 0)
    ci = lax.broadcasted_iota(jnp.int32, (cs, cs), 1)
    eye = ri == ci
    xor = ri ^ ci
    nt = (((1,), (1,)), ((), ()))
    levels = int(math.log2(cs))
    chains = []
    for d, (q_ref, k_ref, v_ref, g_ref, o_ref) in enumerate(((qa_ref, ka_ref, va_ref, ga_ref, of_ref),
                                                              (qb_ref, kb_ref, vb_ref, gb_ref, ob_ref))):
        incl = (ri >= ci) if d == 0 else (ri <= ci)
        strict = (ri > ci) if d == 0 else (ri < ci)
        gbv = g_ref[0]
        hi = gbv.astype(BF16)
        rem = gbv - hi.astype(F32)
        mid = rem.astype(BF16)
        t01 = jnp.where(incl, 1.0, 0.0).astype(BF16)
        cum = sum(jnp.dot(t01, p, preferred_element_type=F32)
                  for p in (hi, mid, (rem - mid.astype(F32)).astype(BF16)))
        cum_t = cum.T
        last = cs - 1 if d == 0 else 0
        for hd in range(nheads):
            qb = q_ref[0, hd]
            kb = k_ref[0, hd]
            k = kb.astype(F32)
            v = v_ref[0, hd].astype(F32)
            kk = lax.dot_general(kb, kb, nt, preferred_element_type=F32)
            qk = lax.dot_general(qb, kb, nt, preferred_element_type=F32)
            lg = d * nheads + hd
            lb = (2 + d) * nheads + hd
            gc_col = cum[:, lg:lg + 1]
            gc_row = cum_t[lg:lg + 1, :]
            g_tot = cum[last:last + 1, lg:lg + 1]
            b_col = gbv[:, lb:lb + 1]
            ex = jnp.exp2(gc_col - gc_row)
            egc = jnp.exp2(gc_col)
            chains.append(dict(
                hd=hd, d=d, o_ref=o_ref, eg=jnp.exp2(g_tot),
                qh=(qb.astype(F32) * egc).astype(BF16),
                qkd=jnp.where(incl, qk * ex, 0.0).astype(BF16),
                ktt=(k.T * jnp.exp2(g_tot - gc_row)).astype(BF16),
                rhs=jnp.concatenate([v * b_col, k * (b_col * egc)], axis=1).astype(BF16),
                neg_off=jnp.where(strict, kk * (-b_col) * ex, 0.0)))
    dmats = [jnp.where(eye, 1.0, jnp.where(xor == 1, ch["neg_off"], 0.0)) for ch in chains]
    neg_offs = [ch["neg_off"].astype(BF16) for ch in chains]
    for lv in range(1, levels):
        sel = jnp.where((xor >> lv) == 1, 1.0, 0.0).astype(BF16)
        dmbs = [dm.astype(BF16) for dm in dmats]
        des = [jnp.dot(db, no * sel, preferred_element_type=F32) for db, no in zip(dmbs, neg_offs)]
        dmats = [dm + jnp.dot(de.astype(BF16), db, preferred_element_type=F32)
                 for dm, de, db in zip(dmats, des, dmbs)]
    sols = [jnp.dot(dm.astype(BF16), ch["rhs"], preferred_element_type=F32) for dm, ch in zip(dmats, chains)]
    dv = of_ref.shape[-1]
    r1s = [jnp.dot(jnp.concatenate([sol[:, dv:].astype(BF16), ch["qh"]], axis=0), sb_scr[ch["hd"], ch["d"]],
                   preferred_element_type=F32) for ch, sol in zip(chains, sols)]
    vns = [(sol[:, :dv] - r1[:cs]).astype(BF16) for sol, r1 in zip(sols, r1s)]
    r2s = [jnp.dot(jnp.concatenate([ch["qkd"], ch["ktt"]], axis=0), vn, preferred_element_type=F32)
           for ch, vn in zip(chains, vns)]
    for ch, r1, r2 in zip(chains, r1s, r2s):
        hd, d = ch["hd"], ch["d"]
        ch["o_ref"][0, hd] = (r1[cs:] + r2[:cs]).astype(of_ref.dtype)
        s_new = s_scr[hd, d] * ch["eg"] + r2[cs:]
        s_scr[hd, d] = s_new
        sb_scr[hd, d] = s_new.astype(sb_scr.dtype)

    @pl.when(n == pl.num_programs(1) - 1)
    def _():
        sfin_ref[0] = s_scr[...]


def _delta(q, k, v, gb, s0, cs=SCAN_CHUNK):
    b, nh, l, dk = q.shape
    dv = v.shape[-1]
    nc = l // cs
    fwd = lambda wd: pl.BlockSpec((1, nh, cs, wd), lambda bi, n: (bi, 0, n, 0))
    bwd = lambda wd: pl.BlockSpec((1, nh, cs, wd), lambda bi, n: (bi, 0, nc - 1 - n, 0))
    st = pl.BlockSpec((1, nh, 2, dk, dv), lambda bi, n: (bi, 0, 0, 0, 0))
    return pl.pallas_call(
        functools.partial(_delta_kernel, nheads=nh, cs=cs),
        out_shape=[jax.ShapeDtypeStruct((b, nh, l, dv), BF16),
                   jax.ShapeDtypeStruct((b, nh, l, dv), BF16),
                   jax.ShapeDtypeStruct((b, nh, 2, dk, dv), F32)],
        grid=(b, nc),
        in_specs=[fwd(dk), fwd(dk), fwd(dv), pl.BlockSpec((1, cs, GB_LANES), lambda bi, n: (bi, n, 0)),
                  bwd(dk), bwd(dk), bwd(dv), pl.BlockSpec((1, cs, GB_LANES), lambda bi, n: (bi, nc - 1 - n, 0)),
                  st],
        out_specs=[fwd(dv), bwd(dv), st],
        scratch_shapes=[pltpu.VMEM((nh, 2, dk, dv), F32), pltpu.VMEM((nh, 2, dk, dv), BF16)],
        compiler_params=_cparams("parallel", "arbitrary"), name="delta_rule",
    )(q, k, v, gb, q, k, v, gb, s0)


def _dn_out_kernel(of_ref, ob_ref, z_ref, res_ref, mod_ref, ng_ref, w_ref, fg_ref, o_ref, *, nheads, dv, d):
    acc = None
    for hd in range(nheads):
        sl = slice(hd * dv, (hd + 1) * dv)
        o = of_ref[0, hd].astype(F32) + ob_ref[0, hd].astype(F32)
        o = o * lax.rsqrt(jnp.mean(o * o, axis=-1, keepdims=True) + EPS) * ng_ref[...]
        y = (o * _silu(z_ref[0, :, sl].astype(F32))).astype(BF16)
        p = jnp.dot(y, w_ref[sl, :], preferred_element_type=F32)
        acc = p if acc is None else acc + p
    gate = mod_ref[0, :, 2 * d:3 * d]
    hn = res_ref[0] + gate * acc
    o_ref[0] = hn * lax.rsqrt(jnp.mean(hn * hn, axis=-1, keepdims=True) + EPS) * fg_ref[...]


def _dn_out(o_f, o_b, z, res, mod, norm_g, w_out, final_g, tm=512):
    b, nh, l, dv = o_f.shape
    d = res.shape[-1]
    tm = min(tm, l)
    tok = lambda wd: pl.BlockSpec((1, tm, wd), lambda i, j: (i, j, 0))
    hs = pl.BlockSpec((1, nh, tm, dv), lambda i, j: (i, 0, j, 0))
    return pl.pallas_call(
        functools.partial(_dn_out_kernel, nheads=nh, dv=dv, d=d),
        out_shape=jax.ShapeDtypeStruct((b, l, d), F32),
        grid=(b, l // tm),
        in_specs=[hs, hs, tok(nh * dv), tok(d),
                  pl.BlockSpec((1, 1, 3 * d), lambda i, j: (i, 0, 0)),
                  _const_spec((1, dv)), _const_spec(w_out.shape), _const_spec((1, d))],
        out_specs=tok(d),
        compiler_params=_cparams("parallel", "parallel"), name="delta_out",
    )(o_f, o_b, z, res, mod, norm_g.reshape(1, dv), w_out, final_g.reshape(1, d))


def _fourier_layer(h_lat, h_ctx, norm_g, mod_lat, mod_ctx, w_in, w_grp, w_out):
    b, l, _ = h_lat.shape
    lc = h_ctx.shape[1]
    width = w_in.shape[1] // 2
    gd = w_grp.shape[-1]
    w_in = w_in.astype(BF16)
    w_out = w_out.astype(BF16)
    gw = _grp(w_grp)

    u, z = _nmm(h_lat, norm_g, mod_lat, w_in, (width, width), (BF16, BF16))
    n2 = DFT_N2
    n1 = l // n2
    y = _dft1(u, n1, n2)
    xr, xi = _dft2_lat(_dft_complex_matrix(n2), y)
    xr = xr.reshape(b, l, width)
    xi = xi.reshape(b, l, width)
    out_lat = _fn_out(xr, xi, z, h_lat, mod_lat, gw, w_out, 1.0 / math.sqrt(l * gd))

    uc, zc = _nmm(h_ctx, norm_g, mod_ctx, w_in, (width, width), (BF16, BF16))
    xrc, xic = _dft2(_dft_real_matrix(lc), uc)
    out_ctx = _fn_out(xrc, xic, zc, h_ctx, mod_ctx, gw, w_out, 1.0 / math.sqrt(lc * gd))
    return out_lat, out_ctx


def _delta_layer_final(h_lat, h_ctx, norm_g, mod_lat, mod_ctx, w_in, conv_w, a_log, dt_bias, dn_norm_g, w_out,
                       final_g):
    b, l, d = h_lat.shape
    lc = h_ctx.shape[1]
    main_w = 2 * DN_HEADS * DN_DK + 2 * DN_HEADS * DN_DV
    n_ab = 4 * DN_HEADS
    w_main = w_in.astype(BF16)
    w_ab = jnp.pad(w_in[:, main_w:], ((0, 0), (0, GB_LANES - n_ab))).astype(BF16)
    pad = jnp.zeros((GB_LANES - 2 * DN_HEADS,), F32)
    abp = jnp.stack([jnp.concatenate([a_log.reshape(-1).astype(F32), pad]),
                     jnp.concatenate([dt_bias.reshape(-1).astype(F32), pad])])
    qc, kc, vc, _, gbc = _proj_conv(h_ctx, norm_g, mod_ctx, w_main, w_ab, abp, conv_w, 1, lc)
    ql, kl, vl, zl, gbl = _proj_conv(h_lat, norm_g, mod_lat, w_main, w_ab, abp, conv_w, l // GRID_W, GRID_W)
    s0 = jnp.zeros((b, DN_HEADS, 2, DN_DK, DN_DV), F32)
    _, _, s_ctx = _delta(qc, kc, vc, gbc, s0)
    o_f, o_b, _ = _delta(ql, kl, vl, gbl, s_ctx)
    return _dn_out(o_f, o_b, zl, h_lat, mod_lat, dn_norm_g, w_out.astype(BF16), final_g)


def kernel(x, c, ctx, c_ctx, mod_w, mod_b, norm_g, final_g, fn_w_in, fn_w_grp, fn_w_out,
           dn_w_in, dn_conv, dn_a_log, dn_dt_bias, dn_norm_g, dn_w_out):
    b, l, d = x.shape
    cond = jnp.concatenate([c, c_ctx[None, :], jnp.zeros((8 - b - 1, d), c.dtype)], axis=0)
    mods = _mods(cond, mod_w, mod_b)
    mod_lat = [mods[i, :b].reshape(b, 1, 3 * d) for i in range(2)]
    mod_ctx = [mods[i, b:b + 1].reshape(1, 1, 3 * d) for i in range(2)]
    h_lat, h_ctx = _fourier_layer(x, ctx, norm_g[0], mod_lat[0], mod_ctx[0], fn_w_in[0], fn_w_grp[0], fn_w_out[0])
    return _delta_layer_final(h_lat, h_ctx, norm_g[1], mod_lat[1], mod_ctx[1], dn_w_in[0], dn_conv[0],
                              dn_a_log[0], dn_dt_bias[0], dn_norm_g[0], dn_w_out[0], final_g)
```

```python
import functools
import math

import numpy as np
import jax
import jax.numpy as jnp
from jax import lax
from jax.experimental import pallas as pl
from jax.experimental.pallas import tpu as pltpu

F32 = jnp.float32
BF16 = jnp.bfloat16
HI = lax.Precision.HIGHEST
EPS = 1e-6
LOG2E = 1.4426950408889634

GRID_W = 64
DN_HEADS = 8
DN_DK = 128
DN_DV = 256
SCAN_CHUNK = 128
DFT_N2 = 128
GB_LANES = 128
VMEM_LIMIT_BYTES = 48 * 1024 * 1024


def _cparams(*sem):
    return pltpu.CompilerParams(dimension_semantics=sem, vmem_limit_bytes=VMEM_LIMIT_BYTES)


def _const_spec(shape):
    zeros = (0,) * len(shape)
    return pl.BlockSpec(shape, lambda *_: zeros, pipeline_mode=pl.Buffered(1))


def _silu(v):
    return v * jax.nn.sigmoid(v)


def _mods_kernel(cond_ref, w_ref, b_ref, o_ref):
    s = _silu(cond_ref[...])
    o_ref[0] = jnp.dot(s, w_ref[0], preferred_element_type=F32, precision=HI) + b_ref[0]


def _mods(cond, mod_w, mod_b):
    depth, d, d3 = mod_w.shape
    rows = cond.shape[0]
    tn = 1024
    return pl.pallas_call(
        _mods_kernel,
        out_shape=jax.ShapeDtypeStruct((depth, rows, d3), F32),
        grid=(depth, d3 // tn),
        in_specs=[pl.BlockSpec((rows, d), lambda i, j: (0, 0)),
                  pl.BlockSpec((1, d, tn), lambda i, j: (i, 0, j)),
                  pl.BlockSpec((1, 1, tn), lambda i, j: (i, 0, j))],
        out_specs=pl.BlockSpec((1, rows, tn), lambda i, j: (i, 0, j)),
        compiler_params=_cparams("parallel", "parallel"), name="adaln_mods",
    )(cond, mod_w, mod_b.reshape(depth, 1, d3))


def _norm_mod(x, g_ref, mod_ref, d):
    y = x * lax.rsqrt(jnp.mean(x * x, axis=-1, keepdims=True) + EPS) * g_ref[...]
    return y * (1.0 + mod_ref[0, :, d:2 * d]) + mod_ref[0, :, 0:d]


def _nmm_kernel(x_ref, g_ref, mod_ref, w_ref, *outs, splits, d):
    n = _norm_mod(x_ref[0], g_ref, mod_ref, d).astype(BF16)
    c0 = 0
    for o_ref, width in zip(outs, splits):
        step = min(width, 1024)
        for s in range(0, width, step):
            r = jnp.dot(n, w_ref[:, c0 + s:c0 + s + step], preferred_element_type=F32)
            o_ref[0, :, s:s + step] = r.astype(o_ref.dtype)
        c0 += width


def _mod_map(mod):
    return (lambda i, j: (i, 0, 0)) if mod.shape[0] > 1 else (lambda i, j: (0, 0, 0))


def _nmm(x, g, mod, w, splits, dtypes, tm=1024):
    b, l, d = x.shape
    tm = min(tm, l)
    return pl.pallas_call(
        functools.partial(_nmm_kernel, splits=splits, d=d),
        out_shape=[jax.ShapeDtypeStruct((b, l, wd), dt) for wd, dt in zip(splits, dtypes)],
        grid=(b, l // tm),
        in_specs=[pl.BlockSpec((1, tm, d), lambda i, j: (i, j, 0)),
                  _const_spec((1, d)),
                  pl.BlockSpec((1, 1, 3 * d), _mod_map(mod)),
                  _const_spec(w.shape)],
        out_specs=[pl.BlockSpec((1, tm, wd), lambda i, j: (i, j, 0)) for wd in splits],
        compiler_params=_cparams("parallel", "parallel"), name="norm_proj",
    )(x, g.reshape(1, d), mod, w)


def _dft_cos_sin(n):
    k = np.arange(n)
    ang = 2.0 * np.pi * ((k[:, None] * k[None, :]) % n) / n
    return np.cos(ang), np.sin(ang)


def _dft1_kernel(m_ref, u_ref, cos_ref, sin_ref, o_ref, *, n1, tn2):
    ut = jnp.transpose(u_ref[0], (1, 0, 2))
    for j in range(tn2):
        r = jnp.dot(m_ref[...], ut[j], preferred_element_type=F32)
        yr = r[:n1]
        yi = r[n1:]
        cs = cos_ref[j]
        sn = sin_ref[j]
        o_ref[0, 0, j] = (yr * cs + yi * sn).astype(o_ref.dtype)
        o_ref[0, 1, j] = (yi * cs - yr * sn).astype(o_ref.dtype)


def _dft1(u, n1, n2, tn2=16, tc=2048):
    b, l, c = u.shape
    cm, sm = _dft_cos_sin(n1)
    m1 = jnp.asarray(np.concatenate([cm, -sm], axis=0), F32).astype(BF16)
    ang = 2.0 * np.pi * (np.arange(n2)[:, None] * np.arange(n1)[None, :]) / l
    tw_cos = jnp.asarray(np.cos(ang)[:, :, None], F32)
    tw_sin = jnp.asarray(np.sin(ang)[:, :, None], F32)
    return pl.pallas_call(
        functools.partial(_dft1_kernel, n1=n1, tn2=tn2),
        out_shape=jax.ShapeDtypeStruct((b, 2, n2, n1, c), BF16),
        grid=(b, n2 // tn2, c // tc),
        in_specs=[_const_spec((2 * n1, n1)),
                  pl.BlockSpec((1, n1, tn2, tc), lambda i, j, k: (i, 0, j, k)),
                  pl.BlockSpec((tn2, n1, 1), lambda i, j, k: (j, 0, 0)),
                  pl.BlockSpec((tn2, n1, 1), lambda i, j, k: (j, 0, 0))],
        out_specs=pl.BlockSpec((1, 2, tn2, n1, tc), lambda i, j, k: (i, 0, j, 0, k)),
        compiler_params=_cparams("parallel", "parallel", "parallel"), name="dft_stage1",
    )(m1, u.reshape(b, n1, n2, c), tw_cos, tw_sin)


def _dft2_lat_kernel(m_ref, y_ref, xr_ref, xi_ref, *, n2, tk1):
    yr = jnp.transpose(y_ref[0, 0], (1, 0, 2))
    yi = jnp.transpose(y_ref[0, 1], (1, 0, 2))
    xr, xi = [], []
    for j in range(tk1):
        r = jnp.dot(m_ref[...], jnp.concatenate([yr[j], yi[j]], axis=0), preferred_element_type=F32)
        xr.append(r[:n2].astype(xr_ref.dtype))
        xi.append(r[n2:].astype(xi_ref.dtype))
    xr_ref[0] = jnp.transpose(jnp.stack(xr), (1, 0, 2))
    xi_ref[0] = jnp.transpose(jnp.stack(xi), (1, 0, 2))


def _dft2_lat(m, y, tk1=16, tc=1024):
    b, _, n2, n1, c = y.shape
    tk1 = min(tk1, n1)
    return pl.pallas_call(
        functools.partial(_dft2_lat_kernel, n2=n2, tk1=tk1),
        out_shape=[jax.ShapeDtypeStruct((b, n2, n1, c), BF16)] * 2,
        grid=(b, n1 // tk1, c // tc),
        in_specs=[_const_spec(m.shape),
                  pl.BlockSpec((1, 2, n2, tk1, tc), lambda i, j, k: (i, 0, 0, j, k))],
        out_specs=[pl.BlockSpec((1, n2, tk1, tc), lambda i, j, k: (i, 0, j, k))] * 2,
        compiler_params=_cparams("parallel", "parallel", "parallel"), name="dft_stage2_lat",
    )(m, y)


def _dft2_kernel(m_ref, y_ref, xr_ref, xi_ref, *, half):
    r = jnp.dot(m_ref[...], y_ref[0], preferred_element_type=F32)
    xr_ref[0] = r[:half].astype(xr_ref.dtype)
    xi_ref[0] = r[half:].astype(xi_ref.dtype)


def _dft2(m, y, tc=2048):
    b, k, cols = y.shape
    half = m.shape[0] // 2
    return pl.pallas_call(
        functools.partial(_dft2_kernel, half=half),
        out_shape=[jax.ShapeDtypeStruct((b, half, cols), BF16)] * 2,
        grid=(b, cols // tc),
        in_specs=[_const_spec(m.shape),
                  pl.BlockSpec((1, k, tc), lambda i, j: (i, 0, j))],
        out_specs=[pl.BlockSpec((1, half, tc), lambda i, j: (i, 0, j))] * 2,
        compiler_params=_cparams("parallel", "parallel"), name="dft_stage2",
    )(m, y)


def _dft_real_matrix(n):
    cm, sm = _dft_cos_sin(n)
    return jnp.asarray(np.concatenate([cm, -sm], axis=0), F32).astype(BF16)


def _dft_complex_matrix(n):
    cm, sm = _dft_cos_sin(n)
    return jnp.asarray(np.block([[cm, sm], [-sm, cm]]), F32).astype(BF16)


def _grp_kernel(m_ref, w_ref, g_ref):
    g_ref[0] = jnp.dot(m_ref[...], w_ref[0], preferred_element_type=F32, precision=HI).astype(g_ref.dtype)


def _grp(w_grp):
    groups, n, _ = w_grp.shape
    cm, sm = _dft_cos_sin(n)
    m = jnp.asarray(np.concatenate([cm, sm], axis=0), F32)
    return pl.pallas_call(
        _grp_kernel,
        out_shape=jax.ShapeDtypeStruct((groups, 2 * n, n), BF16),
        grid=(groups,),
        in_specs=[_const_spec((2 * n, n)),
                  pl.BlockSpec((1, n, n), lambda g: (g, 0, 0))],
        out_specs=pl.BlockSpec((1, 2 * n, n), lambda g: (g, 0, 0)),
        compiler_params=_cparams("parallel"), name="group_weights",
    )(m, w_grp)


def _fn_out_kernel(xr_ref, xi_ref, z_ref, res_ref, mod_ref, gw_ref, w_ref, o_ref, *, groups, gd, d, scale):
    ys = []
    for g in range(groups):
        sl = slice(g * gd, (g + 1) * gd)
        f = jnp.dot(jnp.concatenate([xr_ref[0, :, sl], xi_ref[0, :, sl]], axis=1), gw_ref[g],
                    preferred_element_type=F32)
        ys.append((f * scale * _silu(z_ref[0, :, sl].astype(F32))).astype(BF16))
    acc = jnp.dot(jnp.concatenate(ys, axis=1), w_ref[...], preferred_element_type=F32)
    gate = mod_ref[0, :, 2 * d:3 * d]
    o_ref[0] = res_ref[0] + gate * acc


def _fn_out(xr, xi, z, res, mod, gw, w_out, scale, tm=512):
    b, l, c = xr.shape
    d = res.shape[-1]
    tm = min(tm, l)
    groups, _, gd = gw.shape
    tok = lambda wd: pl.BlockSpec((1, tm, wd), lambda i, j: (i, j, 0))
    return pl.pallas_call(
        functools.partial(_fn_out_kernel, groups=groups, gd=gd, d=d, scale=scale),
        out_shape=jax.ShapeDtypeStruct((b, l, d), F32),
        grid=(b, l // tm),
        in_specs=[tok(c), tok(c), tok(c), tok(d),
                  pl.BlockSpec((1, 1, 3 * d), _mod_map(mod)),
                  _const_spec(gw.shape), _const_spec(w_out.shape)],
        out_specs=tok(d),
        compiler_params=_cparams("parallel", "parallel"), name="fourier_out",
    )(xr, xi, z, res, mod, gw, w_out)


def _proj_conv_kernel(xp_ref, xm_ref, xn_ref, g_ref, mod_ref, w_ref, wab_ref, abp_ref, cw_ref, sh_ref,
                      q_ref, k_ref, v_ref, z_ref, gb_ref, *, d, cols, tr, rp, pr, halo, nheads, dk, dv, cb):
    i = pl.program_id(1)
    last = pl.num_programs(1) - 1
    if halo:
        n_prev = (_norm_mod(xp_ref[0], g_ref, mod_ref, d) * (i > 0).astype(F32)).astype(BF16)
    n_main = _norm_mod(xm_ref[0], g_ref, mod_ref, d).astype(BF16)
    if halo:
        n_next = (_norm_mod(xn_ref[0], g_ref, mod_ref, d) * (i < last).astype(F32)).astype(BF16)
        ne = jnp.concatenate([n_prev, n_main, n_next], axis=0)
    qk_w = nheads * dk
    v_w = nheads * dv
    z0 = 2 * qk_w + v_w
    n_in = (tr + 2) * cols // pr
    n_out = tr * cols // rp

    def project(c0, p):
        if halo:
            rows = ne[p * pr:(p + 1) * pr]
        elif 0 < p < n_in - 1:
            rows = n_main[(p - 1) * pr:p * pr]
        else:
            return jnp.zeros((pr, cb), F32)
        return jnp.dot(rows, w_ref[:, c0:c0 + cb], preferred_element_type=F32)

    pieces = [project(0, p) for p in range(n_in)]
    zstep = v_w * cb // z0
    for c0 in range(0, z0, cb):
        nxt = []
        for j in range(n_out):
            if c0 + cb < z0:
                nxt += [project(c0 + cb, p) for p in range(len(nxt), (j + 1) * n_in // n_out)]
            if j == n_out // 2:
                zs = c0 // cb * zstep
                z_ref[0, :, zs:zs + zstep] = jnp.dot(
                    n_main, w_ref[:, z0 + zs:z0 + zs + zstep], preferred_element_type=F32).astype(z_ref.dtype)
            lo, hi = j * rp, j * rp + rp + 2 * cols
            p_lo, p_hi = lo // pr, (hi - 1) // pr
            e = jnp.concatenate(pieces[p_lo:p_hi + 1], axis=0)[lo - p_lo * pr:hi - p_lo * pr]
            part = []
            for dc in range(3):
                acc = None
                for dr in range(3):
                    term = e[dr * cols:dr * cols + rp] * cw_ref[dr, dc:dc + 1, c0:c0 + cb]
                    acc = term if acc is None else acc + term
                part.append(acc)
            y = _silu(jnp.dot(sh_ref[...], jnp.concatenate(part, axis=0).astype(BF16), preferred_element_type=F32))
            rows = slice(j * rp, (j + 1) * rp)
            if c0 < 2 * qk_w:
                o_ref, scale, base = (q_ref, dk ** -0.5, c0) if c0 < qk_w else (k_ref, 1.0, c0 - qk_w)
                for h in range(cb // dk):
                    yh = y[:, h * dk:(h + 1) * dk]
                    yh = yh * lax.rsqrt(jnp.sum(yh * yh, axis=-1, keepdims=True) + EPS) * scale
                    o_ref[0, base // dk + h, rows, :] = yh.astype(o_ref.dtype)
            else:
                base = c0 - 2 * qk_w
                for h in range(cb // dv):
                    v_ref[0, base // dv + h, rows, :] = y[:, h * dv:(h + 1) * dv].astype(v_ref.dtype)
        pieces = nxt
    r = jnp.dot(n_main, wab_ref[...], preferred_element_type=F32)
    t = r + abp_ref[1:2, :]
    softplus = jnp.maximum(t, 0.0) + jnp.log(1.0 + jnp.exp(-jnp.abs(t)))
    lane = lax.broadcasted_iota(jnp.int32, r.shape, 1)
    gb_ref[0] = jnp.where(lane < 2 * nheads, (-LOG2E) * jnp.exp(abp_ref[0:1, :]) * softplus, jax.nn.sigmoid(r))


def _proj_conv(x, g, mod, w, w_ab, abp, conv_w, rows, cols, cb=512):
    b, l, d = x.shape
    tr = min(rows, max(1, 512 // cols))
    n = tr * cols
    rp = cols * min(tr, 2)
    halo = rows > tr or rp != cols
    half = (tr + 2) * cols // 2
    pr = half if halo and half % 16 == 0 else rp
    t = np.arange(rp)
    left = (t[:, None] - 1 == t[None, :]) & (t[:, None] % cols != 0)
    right = (t[:, None] + 1 == t[None, :]) & (t[:, None] % cols != cols - 1)
    shift = jnp.asarray(np.concatenate([left, np.eye(rp, dtype=bool), right], axis=1), F32).astype(BF16)
    heads = lambda wd: pl.BlockSpec((1, DN_HEADS, n, wd), lambda bi, i: (bi, 0, i, 0))
    tok = lambda wd: pl.BlockSpec((1, n, wd), lambda bi, i: (bi, i, 0))
    return pl.pallas_call(
        functools.partial(_proj_conv_kernel, d=d, cols=cols, tr=tr, rp=rp, pr=pr, halo=halo,
                          nheads=DN_HEADS, dk=DN_DK, dv=DN_DV, cb=cb),
        out_shape=[jax.ShapeDtypeStruct((b, DN_HEADS, l, DN_DK), BF16),
                   jax.ShapeDtypeStruct((b, DN_HEADS, l, DN_DK), BF16),
                   jax.ShapeDtypeStruct((b, DN_HEADS, l, DN_DV), BF16),
                   jax.ShapeDtypeStruct((b, l, DN_HEADS * DN_DV), BF16),
                   jax.ShapeDtypeStruct((b, l, GB_LANES), F32)],
        grid=(b, rows // tr),
        in_specs=[pl.BlockSpec((1, cols, d), lambda bi, i: (bi, jnp.maximum(i * tr - 1, 0), 0)),
                  pl.BlockSpec((1, n, d), lambda bi, i: (bi, i, 0)),
                  pl.BlockSpec((1, cols, d), lambda bi, i: (bi, jnp.minimum((i + 1) * tr, rows - 1), 0)),
                  _const_spec((1, d)),
                  pl.BlockSpec((1, 1, 3 * d), _mod_map(mod)),
                  _const_spec(w.shape), _const_spec(w_ab.shape), _const_spec(abp.shape), _const_spec(conv_w.shape),
                  _const_spec(shift.shape)],
        out_specs=[heads(DN_DK), heads(DN_DK), heads(DN_DV), tok(DN_HEADS * DN_DV), tok(GB_LANES)],
        compiler_params=_cparams("parallel", "parallel"), name="proj_conv",
    )(x, x, x, g.reshape(1, d), mod, w, w_ab, abp, conv_w, shift)


def _delta_kernel(qa_ref, ka_ref, va_ref, ga_ref, qb_ref, kb_ref, vb_ref, gb_ref, s0_ref,
                  of_ref, ob_ref, sfin_ref, s_scr, sb_scr, *, nheads, cs, nsub):
    n = pl.program_id(1)

    @pl.when(n == 0)
    def _():
        s_scr[...] = s0_ref[0]
        sb_scr[...] = s0_ref[0].astype(sb_scr.dtype)

    ri = lax.broadcasted_iota(jnp.int32, (cs, cs), 0)
    ci = lax.broadcasted_iota(jnp.int32, (cs, cs), 1)
    eye = ri == ci
    xor = ri ^ ci
    nt = (((1,), (1,)), ((), ()))
    levels = int(math.log2(cs))
    def chunk_pair(rows):
        chains = []
        for d, (q_ref, k_ref, v_ref, g_ref, o_ref) in enumerate(((qa_ref, ka_ref, va_ref, ga_ref, of_ref),
                                                                  (qb_ref, kb_ref, vb_ref, gb_ref, ob_ref))):
            incl = (ri >= ci) if d == 0 else (ri <= ci)
            strict = (ri > ci) if d == 0 else (ri < ci)
            gbv = g_ref[0, rows[d], :]
            hi = gbv.astype(BF16)
            rem = gbv - hi.astype(F32)
            mid = rem.astype(BF16)
            t01 = jnp.where(incl, 1.0, 0.0).astype(BF16)
            cum = sum(jnp.dot(t01, p, preferred_element_type=F32)
                      for p in (hi, mid, (rem - mid.astype(F32)).astype(BF16)))
            cum_t = cum.T
            last = cs - 1 if d == 0 else 0
            for hd in range(nheads):
                qb = q_ref[0, hd, rows[d], :]
                kb = k_ref[0, hd, rows[d], :]
                k = kb.astype(F32)
                v = v_ref[0, hd, rows[d], :].astype(F32)
                kk = lax.dot_general(kb, kb, nt, preferred_element_type=F32)
                qk = lax.dot_general(qb, kb, nt, preferred_element_type=F32)
                lg = d * nheads + hd
                lb = (2 + d) * nheads + hd
                gc_col = cum[:, lg:lg + 1]
                gc_row = cum_t[lg:lg + 1, :]
                g_tot = cum[last:last + 1, lg:lg + 1]
                b_col = gbv[:, lb:lb + 1]
                ex = jnp.exp2(gc_col - gc_row)
                egc = jnp.exp2(gc_col)
                chains.append(dict(
                    hd=hd, d=d, o_ref=o_ref, eg=jnp.exp2(g_tot),
                    qh=(qb.astype(F32) * egc).astype(BF16),
                    qkd=jnp.where(incl, qk * ex, 0.0).astype(BF16),
                    ktt=(k.T * jnp.exp2(g_tot - gc_row)).astype(BF16),
                    rhs=jnp.concatenate([v * b_col, k * (b_col * egc)], axis=1).astype(BF16),
                    neg_off=jnp.where(strict, kk * (-b_col) * ex, 0.0)))
        dmats = [jnp.where(eye, 1.0, jnp.where(xor == 1, ch["neg_off"], 0.0)) for ch in chains]
        neg_offs = [ch["neg_off"].astype(BF16) for ch in chains]
        for lv in range(1, levels):
            sel = jnp.where((xor >> lv) == 1, 1.0, 0.0).astype(BF16)
            dmbs = [dm.astype(BF16) for dm in dmats]
            des = [jnp.dot(db, no * sel, preferred_element_type=F32) for db, no in zip(dmbs, neg_offs)]
            dmats = [dm + jnp.dot(de.astype(BF16), db, preferred_element_type=F32)
                     for dm, de, db in zip(dmats, des, dmbs)]
        sols = [jnp.dot(dm.astype(BF16), ch["rhs"], preferred_element_type=F32) for dm, ch in zip(dmats, chains)]
        dv = of_ref.shape[-1]
        r1s = [jnp.dot(jnp.concatenate([sol[:, dv:].astype(BF16), ch["qh"]], axis=0), sb_scr[ch["hd"], ch["d"]],
                       preferred_element_type=F32) for ch, sol in zip(chains, sols)]
        vns = [(sol[:, :dv] - r1[:cs]).astype(BF16) for sol, r1 in zip(sols, r1s)]
        r2s = [jnp.dot(jnp.concatenate([ch["qkd"], ch["ktt"]], axis=0), vn, preferred_element_type=F32)
               for ch, vn in zip(chains, vns)]
        for ch, r1, r2 in zip(chains, r1s, r2s):
            hd, d = ch["hd"], ch["d"]
            ch["o_ref"][0, hd, rows[d], :] = (r1[cs:] + r2[:cs]).astype(of_ref.dtype)
            s_new = s_scr[hd, d] * ch["eg"] + r2[cs:]
            s_scr[hd, d] = s_new
            sb_scr[hd, d] = s_new.astype(sb_scr.dtype)

    for sub in range(nsub):
        chunk_pair((slice(sub * cs, (sub + 1) * cs), slice((nsub - 1 - sub) * cs, (nsub - sub) * cs)))

    @pl.when(n == pl.num_programs(1) - 1)
    def _():
        sfin_ref[0] = s_scr[...]


def _delta(q, k, v, gb, s0, cs=SCAN_CHUNK):
    b, nh, l, dk = q.shape
    dv = v.shape[-1]
    nsub = 2
    nc = l // (nsub * cs)
    fwd = lambda wd: pl.BlockSpec((1, nh, nsub * cs, wd), lambda bi, n: (bi, 0, n, 0))
    bwd = lambda wd: pl.BlockSpec((1, nh, nsub * cs, wd), lambda bi, n: (bi, 0, nc - 1 - n, 0))
    st = pl.BlockSpec((1, nh, 2, dk, dv), lambda bi, n: (bi, 0, 0, 0, 0))
    return pl.pallas_call(
        functools.partial(_delta_kernel, nheads=nh, cs=cs, nsub=nsub),
        out_shape=[jax.ShapeDtypeStruct((b, nh, l, dv), BF16),
                   jax.ShapeDtypeStruct((b, nh, l, dv), BF16),
                   jax.ShapeDtypeStruct((b, nh, 2, dk, dv), F32)],
        grid=(b, nc),
        in_specs=[fwd(dk), fwd(dk), fwd(dv), pl.BlockSpec((1, nsub * cs, GB_LANES), lambda bi, n: (bi, n, 0)),
                  bwd(dk), bwd(dk), bwd(dv),
                  pl.BlockSpec((1, nsub * cs, GB_LANES), lambda bi, n: (bi, nc - 1 - n, 0)),
                  st],
        out_specs=[fwd(dv), bwd(dv), st],
        scratch_shapes=[pltpu.VMEM((nh, 2, dk, dv), F32), pltpu.VMEM((nh, 2, dk, dv), BF16)],
        compiler_params=_cparams("parallel", "arbitrary"), name="delta_rule",
    )(q, k, v, gb, q, k, v, gb, s0)


def _dn_out_kernel(of_ref, ob_ref, z_ref, res_ref, mod_ref, ng_ref, w_ref, fg_ref, o_ref, *, nheads, dv, d):
    acc = None
    for hd in range(nheads):
        sl = slice(hd * dv, (hd + 1) * dv)
        o = of_ref[0, hd].astype(F32) + ob_ref[0, hd].astype(F32)
        o = o * lax.rsqrt(jnp.mean(o * o, axis=-1, keepdims=True) + EPS) * ng_ref[...]
        y = (o * _silu(z_ref[0, :, sl].astype(F32))).astype(BF16)
        p = jnp.dot(y, w_ref[sl, :], preferred_element_type=F32)
        acc = p if acc is None else acc + p
    gate = mod_ref[0, :, 2 * d:3 * d]
    hn = res_ref[0] + gate * acc
    o_ref[0] = hn * lax.rsqrt(jnp.mean(hn * hn, axis=-1, keepdims=True) + EPS) * fg_ref[...]


def _dn_out(o_f, o_b, z, res, mod, norm_g, w_out, final_g, tm=512):
    b, nh, l, dv = o_f.shape
    d = res.shape[-1]
    tm = min(tm, l)
    tok = lambda wd: pl.BlockSpec((1, tm, wd), lambda i, j: (i, j, 0))
    hs = pl.BlockSpec((1, nh, tm, dv), lambda i, j: (i, 0, j, 0))
    return pl.pallas_call(
        functools.partial(_dn_out_kernel, nheads=nh, dv=dv, d=d),
        out_shape=jax.ShapeDtypeStruct((b, l, d), F32),
        grid=(b, l // tm),
        in_specs=[hs, hs, tok(nh * dv), tok(d),
                  pl.BlockSpec((1, 1, 3 * d), lambda i, j: (i, 0, 0)),
                  _const_spec((1, dv)), _const_spec(w_out.shape), _const_spec((1, d))],
        out_specs=tok(d),
        compiler_params=_cparams("parallel", "parallel"), name="delta_out",
    )(o_f, o_b, z, res, mod, norm_g.reshape(1, dv), w_out, final_g.reshape(1, d))


def _fourier_layer(h_lat, h_ctx, norm_g, mod_lat, mod_ctx, w_in, w_grp, w_out):
    b, l, _ = h_lat.shape
    lc = h_ctx.shape[1]
    width = w_in.shape[1] // 2
    gd = w_grp.shape[-1]
    w_in = w_in.astype(BF16)
    w_out = w_out.astype(BF16)
    gw = _grp(w_grp)

    u, z = _nmm(h_lat, norm_g, mod_lat, w_in, (width, width), (BF16, BF16))
    n2 = DFT_N2
    n1 = l // n2
    y = _dft1(u, n1, n2)
    xr, xi = _dft2_lat(_dft_complex_matrix(n2), y)
    xr = xr.reshape(b, l, width)
    xi = xi.reshape(b, l, width)
    out_lat = _fn_out(xr, xi, z, h_lat, mod_lat, gw, w_out, 1.0 / math.sqrt(l * gd))

    uc, zc = _nmm(h_ctx, norm_g, mod_ctx, w_in, (width, width), (BF16, BF16))
    xrc, xic = _dft2(_dft_real_matrix(lc), uc)
    out_ctx = _fn_out(xrc, xic, zc, h_ctx, mod_ctx, gw, w_out, 1.0 / math.sqrt(lc * gd))
    return out_lat, out_ctx


def _delta_layer_final(h_lat, h_ctx, norm_g, mod_lat, mod_ctx, w_in, conv_w, a_log, dt_bias, dn_norm_g, w_out,
                       final_g):
    b, l, d = h_lat.shape
    lc = h_ctx.shape[1]
    main_w = 2 * DN_HEADS * DN_DK + 2 * DN_HEADS * DN_DV
    n_ab = 4 * DN_HEADS
    w_main = w_in.astype(BF16)
    w_ab = jnp.pad(w_in[:, main_w:], ((0, 0), (0, GB_LANES - n_ab))).astype(BF16)
    pad = jnp.zeros((GB_LANES - 2 * DN_HEADS,), F32)
    abp = jnp.stack([jnp.concatenate([a_log.reshape(-1).astype(F32), pad]),
                     jnp.concatenate([dt_bias.reshape(-1).astype(F32), pad])])
    qc, kc, vc, _, gbc = _proj_conv(h_ctx, norm_g, mod_ctx, w_main, w_ab, abp, conv_w, 1, lc)
    ql, kl, vl, zl, gbl = _proj_conv(h_lat, norm_g, mod_lat, w_main, w_ab, abp, conv_w, l // GRID_W, GRID_W)
    s0 = jnp.zeros((b, DN_HEADS, 2, DN_DK, DN_DV), F32)
    _, _, s_ctx = _delta(qc, kc, vc, gbc, s0)
    o_f, o_b, _ = _delta(ql, kl, vl, gbl, s_ctx)
    return _dn_out(o_f, o_b, zl, h_lat, mod_lat, dn_norm_g, w_out.astype(BF16), final_g)


def kernel(x, c, ctx, c_ctx, mod_w, mod_b, norm_g, final_g, fn_w_in, fn_w_grp, fn_w_out,
           dn_w_in, dn_conv, dn_a_log, dn_dt_bias, dn_norm_g, dn_w_out):
    b, l, d = x.shape
    cond = jnp.concatenate([c, c_ctx[None, :], jnp.zeros((8 - b - 1, d), c.dtype)], axis=0)
    mods = _mods(cond, mod_w, mod_b)
    mod_lat = [mods[i, :b].reshape(b, 1, 3 * d) for i in range(2)]
    mod_ctx = [mods[i, b:b + 1].reshape(1, 1, 3 * d) for i in range(2)]
    h_lat, h_ctx = _fourier_layer(x, ctx, norm_g[0], mod_lat[0], mod_ctx[0], fn_w_in[0], fn_w_grp[0], fn_w_out[0])
    return _delta_layer_final(h_lat, h_ctx, norm_g[1], mod_lat[1], mod_ctx[1], dn_w_in[0], dn_conv[0],
                              dn_a_log[0], dn_dt_bias[0], dn_norm_g[0], dn_w_out[0], final_g)
```

```python
import functools
import math

import numpy as np
import jax
import jax.numpy as jnp
from jax import lax
from jax.experimental import pallas as pl
from jax.experimental.pallas import tpu as pltpu

F32 = jnp.float32
BF16 = jnp.bfloat16
HI = lax.Precision.HIGHEST
EPS = 1e-6
LOG2E = 1.4426950408889634

GRID_W = 64
DN_HEADS = 8
DN_DK = 128
DN_DV = 256
SCAN_CHUNK = 128
DFT_N2 = 128
GB_LANES = 128
VMEM_LIMIT_BYTES = 48 * 1024 * 1024


def _cparams(*sem):
    return pltpu.CompilerParams(dimension_semantics=sem, vmem_limit_bytes=VMEM_LIMIT_BYTES)


def _const_spec(shape):
    zeros = (0,) * len(shape)
    return pl.BlockSpec(shape, lambda *_: zeros, pipeline_mode=pl.Buffered(1))


def _silu(v):
    return v * jax.nn.sigmoid(v)


def _mods_kernel(cond_ref, w_ref, b_ref, o_ref):
    s = _silu(cond_ref[...])
    o_ref[0] = jnp.dot(s, w_ref[0], preferred_element_type=F32, precision=HI) + b_ref[0]


def _mods(cond, mod_w, mod_b):
    depth, d, d3 = mod_w.shape
    rows = cond.shape[0]
    tn = 1024
    return pl.pallas_call(
        _mods_kernel,
        out_shape=jax.ShapeDtypeStruct((depth, rows, d3), F32),
        grid=(depth, d3 // tn),
        in_specs=[pl.BlockSpec((rows, d), lambda i, j: (0, 0)),
                  pl.BlockSpec((1, d, tn), lambda i, j: (i, 0, j)),
                  pl.BlockSpec((1, 1, tn), lambda i, j: (i, 0, j))],
        out_specs=pl.BlockSpec((1, rows, tn), lambda i, j: (i, 0, j)),
        compiler_params=_cparams("parallel", "parallel"), name="adaln_mods",
    )(cond, mod_w, mod_b.reshape(depth, 1, d3))


def _norm_mod(x, g_ref, mod_ref, d):
    y = x * lax.rsqrt(jnp.mean(x * x, axis=-1, keepdims=True) + EPS) * g_ref[...]
    return y * (1.0 + mod_ref[0, :, d:2 * d]) + mod_ref[0, :, 0:d]


def _nmm_kernel(x_ref, g_ref, mod_ref, w_ref, *outs, splits, d):
    n = _norm_mod(x_ref[0], g_ref, mod_ref, d).astype(BF16)
    c0 = 0
    for o_ref, width in zip(outs, splits):
        step = min(width, 1024)
        for s in range(0, width, step):
            r = jnp.dot(n, w_ref[:, c0 + s:c0 + s + step], preferred_element_type=F32)
            o_ref[0, :, s:s + step] = r.astype(o_ref.dtype)
        c0 += width


def _mod_map(mod):
    return (lambda i, j: (i, 0, 0)) if mod.shape[0] > 1 else (lambda i, j: (0, 0, 0))


def _nmm(x, g, mod, w, splits, dtypes, tm=1024):
    b, l, d = x.shape
    tm = min(tm, l)
    return pl.pallas_call(
        functools.partial(_nmm_kernel, splits=splits, d=d),
        out_shape=[jax.ShapeDtypeStruct((b, l, wd), dt) for wd, dt in zip(splits, dtypes)],
        grid=(b, l // tm),
        in_specs=[pl.BlockSpec((1, tm, d), lambda i, j: (i, j, 0)),
                  _const_spec((1, d)),
                  pl.BlockSpec((1, 1, 3 * d), _mod_map(mod)),
                  _const_spec(w.shape)],
        out_specs=[pl.BlockSpec((1, tm, wd), lambda i, j: (i, j, 0)) for wd in splits],
        compiler_params=_cparams("parallel", "parallel"), name="norm_proj",
    )(x, g.reshape(1, d), mod, w)


def _dft_cos_sin(n):
    k = np.arange(n)
    ang = 2.0 * np.pi * ((k[:, None] * k[None, :]) % n) / n
    return np.cos(ang), np.sin(ang)


def _dft1_kernel(m_ref, u_ref, cos_ref, sin_ref, o_ref, *, n1, tn2):
    ut = jnp.transpose(u_ref[0], (1, 0, 2))
    for j in range(tn2):
        r = jnp.dot(m_ref[...], ut[j], preferred_element_type=F32)
        yr = r[:n1]
        yi = r[n1:]
        cs = cos_ref[j]
        sn = sin_ref[j]
        o_ref[0, 0, j] = (yr * cs + yi * sn).astype(o_ref.dtype)
        o_ref[0, 1, j] = (yi * cs - yr * sn).astype(o_ref.dtype)


def _dft1(u, n1, n2, tn2=16, tc=2048):
    b, l, c = u.shape
    cm, sm = _dft_cos_sin(n1)
    m1 = jnp.asarray(np.concatenate([cm, -sm], axis=0), F32).astype(BF16)
    ang = 2.0 * np.pi * (np.arange(n2)[:, None] * np.arange(n1)[None, :]) / l
    tw_cos = jnp.asarray(np.cos(ang)[:, :, None], F32)
    tw_sin = jnp.asarray(np.sin(ang)[:, :, None], F32)
    return pl.pallas_call(
        functools.partial(_dft1_kernel, n1=n1, tn2=tn2),
        out_shape=jax.ShapeDtypeStruct((b, 2, n2, n1, c), BF16),
        grid=(b, n2 // tn2, c // tc),
        in_specs=[_const_spec((2 * n1, n1)),
                  pl.BlockSpec((1, n1, tn2, tc), lambda i, j, k: (i, 0, j, k)),
                  pl.BlockSpec((tn2, n1, 1), lambda i, j, k: (j, 0, 0)),
                  pl.BlockSpec((tn2, n1, 1), lambda i, j, k: (j, 0, 0))],
        out_specs=pl.BlockSpec((1, 2, tn2, n1, tc), lambda i, j, k: (i, 0, j, 0, k)),
        compiler_params=_cparams("parallel", "parallel", "parallel"), name="dft_stage1",
    )(m1, u.reshape(b, n1, n2, c), tw_cos, tw_sin)


def _dft2_lat_kernel(m_ref, y_ref, xr_ref, xi_ref, *, n2, tk1):
    yr = jnp.transpose(y_ref[0, 0], (1, 0, 2))
    yi = jnp.transpose(y_ref[0, 1], (1, 0, 2))
    xr, xi = [], []
    for j in range(tk1):
        r = jnp.dot(m_ref[...], jnp.concatenate([yr[j], yi[j]], axis=0), preferred_element_type=F32)
        xr.append(r[:n2].astype(xr_ref.dtype))
        xi.append(r[n2:].astype(xi_ref.dtype))
    xr_ref[0] = jnp.transpose(jnp.stack(xr), (1, 0, 2))
    xi_ref[0] = jnp.transpose(jnp.stack(xi), (1, 0, 2))


def _dft2_lat(m, y, tk1=16, tc=1024):
    b, _, n2, n1, c = y.shape
    tk1 = min(tk1, n1)
    return pl.pallas_call(
        functools.partial(_dft2_lat_kernel, n2=n2, tk1=tk1),
        out_shape=[jax.ShapeDtypeStruct((b, n2, n1, c), BF16)] * 2,
        grid=(b, n1 // tk1, c // tc),
        in_specs=[_const_spec(m.shape),
                  pl.BlockSpec((1, 2, n2, tk1, tc), lambda i, j, k: (i, 0, 0, j, k))],
        out_specs=[pl.BlockSpec((1, n2, tk1, tc), lambda i, j, k: (i, 0, j, k))] * 2,
        compiler_params=_cparams("parallel", "parallel", "parallel"), name="dft_stage2_lat",
    )(m, y)


def _dft2_kernel(m_ref, y_ref, xr_ref, xi_ref, *, half):
    r = jnp.dot(m_ref[...], y_ref[0], preferred_element_type=F32)
    xr_ref[0] = r[:half].astype(xr_ref.dtype)
    xi_ref[0] = r[half:].astype(xi_ref.dtype)


def _dft2(m, y, tc=2048):
    b, k, cols = y.shape
    half = m.shape[0] // 2
    return pl.pallas_call(
        functools.partial(_dft2_kernel, half=half),
        out_shape=[jax.ShapeDtypeStruct((b, half, cols), BF16)] * 2,
        grid=(b, cols // tc),
        in_specs=[_const_spec(m.shape),
                  pl.BlockSpec((1, k, tc), lambda i, j: (i, 0, j))],
        out_specs=[pl.BlockSpec((1, half, tc), lambda i, j: (i, 0, j))] * 2,
        compiler_params=_cparams("parallel", "parallel"), name="dft_stage2",
    )(m, y)


def _dft_real_matrix(n):
    cm, sm = _dft_cos_sin(n)
    return jnp.asarray(np.concatenate([cm, -sm], axis=0), F32).astype(BF16)


def _dft_complex_matrix(n):
    cm, sm = _dft_cos_sin(n)
    return jnp.asarray(np.block([[cm, sm], [-sm, cm]]), F32).astype(BF16)


def _grp_kernel(m_ref, w_ref, g_ref):
    g_ref[0] = jnp.dot(m_ref[...], w_ref[0], preferred_element_type=F32, precision=HI).astype(g_ref.dtype)


def _grp(w_grp):
    groups, n, _ = w_grp.shape
    cm, sm = _dft_cos_sin(n)
    m = jnp.asarray(np.concatenate([cm, sm], axis=0), F32)
    return pl.pallas_call(
        _grp_kernel,
        out_shape=jax.ShapeDtypeStruct((groups, 2 * n, n), BF16),
        grid=(groups,),
        in_specs=[_const_spec((2 * n, n)),
                  pl.BlockSpec((1, n, n), lambda g: (g, 0, 0))],
        out_specs=pl.BlockSpec((1, 2 * n, n), lambda g: (g, 0, 0)),
        compiler_params=_cparams("parallel"), name="group_weights",
    )(m, w_grp)


def _fn_out_kernel(xr_ref, xi_ref, z_ref, res_ref, mod_ref, gw_ref, w_ref, o_ref, *, groups, gd, d, scale):
    ys = []
    for g in range(groups):
        sl = slice(g * gd, (g + 1) * gd)
        f = jnp.dot(jnp.concatenate([xr_ref[0, :, sl], xi_ref[0, :, sl]], axis=1), gw_ref[g],
                    preferred_element_type=F32)
        ys.append((f * scale * _silu(z_ref[0, :, sl].astype(F32))).astype(BF16))
    acc = jnp.dot(jnp.concatenate(ys, axis=1), w_ref[...], preferred_element_type=F32)
    gate = mod_ref[0, :, 2 * d:3 * d]
    o_ref[0] = res_ref[0] + gate * acc


def _fn_out(xr, xi, z, res, mod, gw, w_out, scale, tm=512):
    b, l, c = xr.shape
    d = res.shape[-1]
    tm = min(tm, l)
    groups, _, gd = gw.shape
    tok = lambda wd: pl.BlockSpec((1, tm, wd), lambda i, j: (i, j, 0))
    return pl.pallas_call(
        functools.partial(_fn_out_kernel, groups=groups, gd=gd, d=d, scale=scale),
        out_shape=jax.ShapeDtypeStruct((b, l, d), F32),
        grid=(b, l // tm),
        in_specs=[tok(c), tok(c), tok(c), tok(d),
                  pl.BlockSpec((1, 1, 3 * d), _mod_map(mod)),
                  _const_spec(gw.shape), _const_spec(w_out.shape)],
        out_specs=tok(d),
        compiler_params=_cparams("parallel", "parallel"), name="fourier_out",
    )(xr, xi, z, res, mod, gw, w_out)


def _proj_conv_kernel(xp_ref, xm_ref, xn_ref, g_ref, mod_ref, w_ref, wab_ref, abp_ref, cw_ref, sh_ref,
                      q_ref, k_ref, v_ref, z_ref, gb_ref, *, d, cols, tr, rp, pr, halo, nheads, dk, dv, cb):
    i = pl.program_id(1)
    last = pl.num_programs(1) - 1
    if halo:
        n_prev = (_norm_mod(xp_ref[0], g_ref, mod_ref, d) * (i > 0).astype(F32)).astype(BF16)
    n_main = _norm_mod(xm_ref[0], g_ref, mod_ref, d).astype(BF16)
    if halo:
        n_next = (_norm_mod(xn_ref[0], g_ref, mod_ref, d) * (i < last).astype(F32)).astype(BF16)
        ne = jnp.concatenate([n_prev, n_main, n_next], axis=0)
    qk_w = nheads * dk
    v_w = nheads * dv
    z0 = 2 * qk_w + v_w
    n_in = (tr + 2) * cols // pr
    n_out = tr * cols // rp

    def project(c0, p):
        if halo:
            rows = ne[p * pr:(p + 1) * pr]
        elif 0 < p < n_in - 1:
            rows = n_main[(p - 1) * pr:p * pr]
        else:
            return jnp.zeros((pr, cb), F32)
        return jnp.dot(rows, w_ref[:, c0:c0 + cb], preferred_element_type=F32)

    pieces = [project(0, p) for p in range(n_in)]
    zstep = v_w * cb // z0
    for c0 in range(0, z0, cb):
        nxt = []
        for j in range(n_out):
            if c0 + cb < z0:
                nxt += [project(c0 + cb, p) for p in range(len(nxt), (j + 1) * n_in // n_out)]
            if j == n_out // 2:
                zs = c0 // cb * zstep
                z_ref[0, :, zs:zs + zstep] = jnp.dot(
                    n_main, w_ref[:, z0 + zs:z0 + zs + zstep], preferred_element_type=F32).astype(z_ref.dtype)
            lo, hi = j * rp, j * rp + rp + 2 * cols
            p_lo, p_hi = lo // pr, (hi - 1) // pr
            e = jnp.concatenate(pieces[p_lo:p_hi + 1], axis=0)[lo - p_lo * pr:hi - p_lo * pr]
            part = []
            for dc in range(3):
                acc = None
                for dr in range(3):
                    term = e[dr * cols:dr * cols + rp] * cw_ref[dr, dc:dc + 1, c0:c0 + cb]
                    acc = term if acc is None else acc + term
                part.append(acc)
            y = _silu(jnp.dot(sh_ref[...], jnp.concatenate(part, axis=0).astype(BF16), preferred_element_type=F32))
            rows = slice(j * rp, (j + 1) * rp)
            if c0 < 2 * qk_w:
                o_ref, scale, base = (q_ref, dk ** -0.5, c0) if c0 < qk_w else (k_ref, 1.0, c0 - qk_w)
                for h in range(cb // dk):
                    yh = y[:, h * dk:(h + 1) * dk]
                    inv = lax.rsqrt(jnp.sum(yh * yh, axis=-1, keepdims=True) + EPS)
                    yh = yh * (inv if scale == 1.0 else inv * scale)
                    o_ref[0, base // dk + h, rows, :] = yh.astype(o_ref.dtype)
            else:
                base = c0 - 2 * qk_w
                for h in range(cb // dv):
                    v_ref[0, base // dv + h, rows, :] = y[:, h * dv:(h + 1) * dv].astype(v_ref.dtype)
        pieces = nxt
    r = jnp.dot(n_main, wab_ref[...], preferred_element_type=F32)
    t = r + abp_ref[1:2, :]
    softplus = jnp.maximum(t, 0.0) + jnp.log(1.0 + jnp.exp(-jnp.abs(t)))
    lane = lax.broadcasted_iota(jnp.int32, r.shape, 1)
    gb_ref[0] = jnp.where(lane < 2 * nheads, (-LOG2E) * jnp.exp(abp_ref[0:1, :]) * softplus, jax.nn.sigmoid(r))


def _proj_conv(x, g, mod, w, w_ab, abp, conv_w, rows, cols, cb=512):
    b, l, d = x.shape
    tr = min(rows, max(1, 512 // cols))
    n = tr * cols
    rp = cols * min(tr, 2)
    halo = rows > tr or rp != cols
    half = (tr + 2) * cols // 2
    pr = half if halo and half % 16 == 0 else rp
    t = np.arange(rp)
    left = (t[:, None] - 1 == t[None, :]) & (t[:, None] % cols != 0)
    right = (t[:, None] + 1 == t[None, :]) & (t[:, None] % cols != cols - 1)
    shift = jnp.asarray(np.concatenate([left, np.eye(rp, dtype=bool), right], axis=1), F32).astype(BF16)
    heads = lambda wd: pl.BlockSpec((1, DN_HEADS, n, wd), lambda bi, i: (bi, 0, i, 0))
    tok = lambda wd: pl.BlockSpec((1, n, wd), lambda bi, i: (bi, i, 0))
    return pl.pallas_call(
        functools.partial(_proj_conv_kernel, d=d, cols=cols, tr=tr, rp=rp, pr=pr, halo=halo,
                          nheads=DN_HEADS, dk=DN_DK, dv=DN_DV, cb=cb),
        out_shape=[jax.ShapeDtypeStruct((b, DN_HEADS, l, DN_DK), BF16),
                   jax.ShapeDtypeStruct((b, DN_HEADS, l, DN_DK), BF16),
                   jax.ShapeDtypeStruct((b, DN_HEADS, l, DN_DV), BF16),
                   jax.ShapeDtypeStruct((b, l, DN_HEADS * DN_DV), BF16),
                   jax.ShapeDtypeStruct((b, l, GB_LANES), F32)],
        grid=(b, rows // tr),
        in_specs=[pl.BlockSpec((1, cols, d), lambda bi, i: (bi, jnp.maximum(i * tr - 1, 0), 0)),
                  pl.BlockSpec((1, n, d), lambda bi, i: (bi, i, 0)),
                  pl.BlockSpec((1, cols, d), lambda bi, i: (bi, jnp.minimum((i + 1) * tr, rows - 1), 0)),
                  _const_spec((1, d)),
                  pl.BlockSpec((1, 1, 3 * d), _mod_map(mod)),
                  _const_spec(w.shape), _const_spec(w_ab.shape), _const_spec(abp.shape), _const_spec(conv_w.shape),
                  _const_spec(shift.shape)],
        out_specs=[heads(DN_DK), heads(DN_DK), heads(DN_DV), tok(DN_HEADS * DN_DV), tok(GB_LANES)],
        compiler_params=_cparams("parallel", "parallel"), name="proj_conv",
    )(x, x, x, g.reshape(1, d), mod, w, w_ab, abp, conv_w, shift)


def _delta_kernel(qa_ref, ka_ref, va_ref, ga_ref, qb_ref, kb_ref, vb_ref, gb_ref, s0_ref,
                  of_ref, ob_ref, sfin_ref, s_scr, sb_scr, *, nheads, cs, nsub):
    n = pl.program_id(1)

    @pl.when(n == 0)
    def _():
        s_scr[...] = s0_ref[0]
        sb_scr[...] = s0_ref[0].astype(sb_scr.dtype)

    ri = lax.broadcasted_iota(jnp.int32, (cs, cs), 0)
    ci = lax.broadcasted_iota(jnp.int32, (cs, cs), 1)
    eye = ri == ci
    xor = ri ^ ci
    nt = (((1,), (1,)), ((), ()))
    levels = int(math.log2(cs))
    def chunk_pair(rows):
        chains = []
        for d, (q_ref, k_ref, v_ref, g_ref, o_ref) in enumerate(((qa_ref, ka_ref, va_ref, ga_ref, of_ref),
                                                                  (qb_ref, kb_ref, vb_ref, gb_ref, ob_ref))):
            incl = (ri >= ci) if d == 0 else (ri <= ci)
            strict = (ri > ci) if d == 0 else (ri < ci)
            gbv = g_ref[0, rows[d], :]
            hi = gbv.astype(BF16)
            rem = gbv - hi.astype(F32)
            mid = rem.astype(BF16)
            t01 = jnp.where(incl, 1.0, 0.0).astype(BF16)
            cum = sum(jnp.dot(t01, p, preferred_element_type=F32)
                      for p in (hi, mid, (rem - mid.astype(F32)).astype(BF16)))
            cum_t = cum.T
            last = cs - 1 if d == 0 else 0
            for hd in range(nheads):
                qb = q_ref[0, hd, rows[d], :]
                kb = k_ref[0, hd, rows[d], :]
                k = kb.astype(F32)
                v = v_ref[0, hd, rows[d], :].astype(F32)
                kk = lax.dot_general(kb, kb, nt, preferred_element_type=F32)
                qk = lax.dot_general(qb, kb, nt, preferred_element_type=F32)
                lg = d * nheads + hd
                lb = (2 + d) * nheads + hd
                gc_col = cum[:, lg:lg + 1]
                gc_row = cum_t[lg:lg + 1, :]
                g_tot = cum[last:last + 1, lg:lg + 1]
                b_col = gbv[:, lb:lb + 1]
                ex = jnp.exp2(gc_col - gc_row)
                egc = jnp.exp2(gc_col)
                chains.append(dict(
                    hd=hd, d=d, o_ref=o_ref, eg=jnp.exp2(g_tot),
                    qh=(qb.astype(F32) * egc).astype(BF16),
                    qkd=jnp.where(incl, qk * ex, 0.0).astype(BF16),
                    ktt=(k.T * jnp.exp2(g_tot - gc_row)).astype(BF16),
                    rhs=jnp.concatenate([v * b_col, k * (b_col * egc)], axis=1).astype(BF16),
                    neg_off=jnp.where(strict, kk * (-b_col) * ex, 0.0)))
        dmats = [jnp.where(eye, 1.0, jnp.where(xor == 1, ch["neg_off"], 0.0)) for ch in chains]
        neg_offs = [ch["neg_off"].astype(BF16) for ch in chains]
        for lv in range(1, levels):
            sel = jnp.where((xor >> lv) == 1, 1.0, 0.0).astype(BF16)
            dmbs = [dm.astype(BF16) for dm in dmats]
            des = [jnp.dot(db, no * sel, preferred_element_type=F32) for db, no in zip(dmbs, neg_offs)]
            dmats = [dm + jnp.dot(de.astype(BF16), db, preferred_element_type=F32)
                     for dm, de, db in zip(dmats, des, dmbs)]
        sols = [jnp.dot(dm.astype(BF16), ch["rhs"], preferred_element_type=F32) for dm, ch in zip(dmats, chains)]
        dv = of_ref.shape[-1]
        r1s = [jnp.dot(jnp.concatenate([sol[:, dv:].astype(BF16), ch["qh"]], axis=0), sb_scr[ch["hd"], ch["d"]],
                       preferred_element_type=F32) for ch, sol in zip(chains, sols)]
        vns = [(sol[:, :dv] - r1[:cs]).astype(BF16) for sol, r1 in zip(sols, r1s)]
        r2s = [jnp.dot(jnp.concatenate([ch["qkd"], ch["ktt"]], axis=0), vn, preferred_element_type=F32)
               for ch, vn in zip(chains, vns)]
        for ch, r1, r2 in zip(chains, r1s, r2s):
            hd, d = ch["hd"], ch["d"]
            ch["o_ref"][0, hd, rows[d], :] = (r1[cs:] + r2[:cs]).astype(of_ref.dtype)
            s_new = s_scr[hd, d] * ch["eg"] + r2[cs:]
            s_scr[hd, d] = s_new
            sb_scr[hd, d] = s_new.astype(sb_scr.dtype)

    for sub in range(nsub):
        chunk_pair((slice(sub * cs, (sub + 1) * cs), slice((nsub - 1 - sub) * cs, (nsub - sub) * cs)))

    @pl.when(n == pl.num_programs(1) - 1)
    def _():
        sfin_ref[0] = s_scr[...]


def _delta(q, k, v, gb, s0, cs=SCAN_CHUNK):
    b, nh, l, dk = q.shape
    dv = v.shape[-1]
    nsub = min(4, l // cs)
    nc = l // (nsub * cs)
    fwd = lambda wd: pl.BlockSpec((1, nh, nsub * cs, wd), lambda bi, n: (bi, 0, n, 0))
    bwd = lambda wd: pl.BlockSpec((1, nh, nsub * cs, wd), lambda bi, n: (bi, 0, nc - 1 - n, 0))
    st = pl.BlockSpec((1, nh, 2, dk, dv), lambda bi, n: (bi, 0, 0, 0, 0))
    return pl.pallas_call(
        functools.partial(_delta_kernel, nheads=nh, cs=cs, nsub=nsub),
        out_shape=[jax.ShapeDtypeStruct((b, nh, l, dv), BF16),
                   jax.ShapeDtypeStruct((b, nh, l, dv), BF16),
                   jax.ShapeDtypeStruct((b, nh, 2, dk, dv), F32)],
        grid=(b, nc),
        in_specs=[fwd(dk), fwd(dk), fwd(dv), pl.BlockSpec((1, nsub * cs, GB_LANES), lambda bi, n: (bi, n, 0)),
                  bwd(dk), bwd(dk), bwd(dv),
                  pl.BlockSpec((1, nsub * cs, GB_LANES), lambda bi, n: (bi, nc - 1 - n, 0)),
                  st],
        out_specs=[fwd(dv), bwd(dv), st],
        scratch_shapes=[pltpu.VMEM((nh, 2, dk, dv), F32), pltpu.VMEM((nh, 2, dk, dv), BF16)],
        compiler_params=_cparams("parallel", "arbitrary"), name="delta_rule",
    )(q, k, v, gb, q, k, v, gb, s0)


def _dn_out_kernel(of_ref, ob_ref, z_ref, res_ref, mod_ref, ng_ref, w_ref, fg_ref, o_ref, *, nheads, dv, d):
    acc = None
    for hd in range(nheads):
        sl = slice(hd * dv, (hd + 1) * dv)
        o = of_ref[0, hd].astype(F32) + ob_ref[0, hd].astype(F32)
        o = o * lax.rsqrt(jnp.mean(o * o, axis=-1, keepdims=True) + EPS) * ng_ref[...]
        y = (o * _silu(z_ref[0, :, sl].astype(F32))).astype(BF16)
        p = jnp.dot(y, w_ref[sl, :], preferred_element_type=F32)
        acc = p if acc is None else acc + p
    gate = mod_ref[0, :, 2 * d:3 * d]
    hn = res_ref[0] + gate * acc
    o_ref[0] = hn * lax.rsqrt(jnp.mean(hn * hn, axis=-1, keepdims=True) + EPS) * fg_ref[...]


def _dn_out(o_f, o_b, z, res, mod, norm_g, w_out, final_g, tm=512):
    b, nh, l, dv = o_f.shape
    d = res.shape[-1]
    tm = min(tm, l)
    tok = lambda wd: pl.BlockSpec((1, tm, wd), lambda i, j: (i, j, 0))
    hs = pl.BlockSpec((1, nh, tm, dv), lambda i, j: (i, 0, j, 0))
    return pl.pallas_call(
        functools.partial(_dn_out_kernel, nheads=nh, dv=dv, d=d),
        out_shape=jax.ShapeDtypeStruct((b, l, d), F32),
        grid=(b, l // tm),
        in_specs=[hs, hs, tok(nh * dv), tok(d),
                  pl.BlockSpec((1, 1, 3 * d), lambda i, j: (i, 0, 0)),
                  _const_spec((1, dv)), _const_spec(w_out.shape), _const_spec((1, d))],
        out_specs=tok(d),
        compiler_params=_cparams("parallel", "parallel"), name="delta_out",
    )(o_f, o_b, z, res, mod, norm_g.reshape(1, dv), w_out, final_g.reshape(1, d))


def _fourier_layer(h_lat, h_ctx, norm_g, mod_lat, mod_ctx, w_in, w_grp, w_out):
    b, l, _ = h_lat.shape
    lc = h_ctx.shape[1]
    width = w_in.shape[1] // 2
    gd = w_grp.shape[-1]
    w_in = w_in.astype(BF16)
    w_out = w_out.astype(BF16)
    gw = _grp(w_grp)

    u, z = _nmm(h_lat, norm_g, mod_lat, w_in, (width, width), (BF16, BF16))
    n2 = DFT_N2
    n1 = l // n2
    y = _dft1(u, n1, n2)
    xr, xi = _dft2_lat(_dft_complex_matrix(n2), y)
    xr = xr.reshape(b, l, width)
    xi = xi.reshape(b, l, width)
    out_lat = _fn_out(xr, xi, z, h_lat, mod_lat, gw, w_out, 1.0 / math.sqrt(l * gd))

    uc, zc = _nmm(h_ctx, norm_g, mod_ctx, w_in, (width, width), (BF16, BF16))
    xrc, xic = _dft2(_dft_real_matrix(lc), uc)
    out_ctx = _fn_out(xrc, xic, zc, h_ctx, mod_ctx, gw, w_out, 1.0 / math.sqrt(lc * gd))
    return out_lat, out_ctx


def _delta_layer_final(h_lat, h_ctx, norm_g, mod_lat, mod_ctx, w_in, conv_w, a_log, dt_bias, dn_norm_g, w_out,
                       final_g):
    b, l, d = h_lat.shape
    lc = h_ctx.shape[1]
    main_w = 2 * DN_HEADS * DN_DK + 2 * DN_HEADS * DN_DV
    n_ab = 4 * DN_HEADS
    w_main = w_in.astype(BF16)
    w_ab = jnp.pad(w_in[:, main_w:], ((0, 0), (0, GB_LANES - n_ab))).astype(BF16)
    pad = jnp.zeros((GB_LANES - 2 * DN_HEADS,), F32)
    abp = jnp.stack([jnp.concatenate([a_log.reshape(-1).astype(F32), pad]),
                     jnp.concatenate([dt_bias.reshape(-1).astype(F32), pad])])
    qc, kc, vc, _, gbc = _proj_conv(h_ctx, norm_g, mod_ctx, w_main, w_ab, abp, conv_w, 1, lc)
    ql, kl, vl, zl, gbl = _proj_conv(h_lat, norm_g, mod_lat, w_main, w_ab, abp, conv_w, l // GRID_W, GRID_W)
    s0 = jnp.zeros((b, DN_HEADS, 2, DN_DK, DN_DV), F32)
    _, _, s_ctx = _delta(qc, kc, vc, gbc, s0)
    o_f, o_b, _ = _delta(ql, kl, vl, gbl, s_ctx)
    return _dn_out(o_f, o_b, zl, h_lat, mod_lat, dn_norm_g, w_out.astype(BF16), final_g)


def kernel(x, c, ctx, c_ctx, mod_w, mod_b, norm_g, final_g, fn_w_in, fn_w_grp, fn_w_out,
           dn_w_in, dn_conv, dn_a_log, dn_dt_bias, dn_norm_g, dn_w_out):
    b, l, d = x.shape
    cond = jnp.concatenate([c, c_ctx[None, :], jnp.zeros((8 - b - 1, d), c.dtype)], axis=0)
    mods = _mods(cond, mod_w, mod_b)
    mod_lat = [mods[i, :b].reshape(b, 1, 3 * d) for i in range(2)]
    mod_ctx = [mods[i, b:b + 1].reshape(1, 1, 3 * d) for i in range(2)]
    h_lat, h_ctx = _fourier_layer(x, ctx, norm_g[0], mod_lat[0], mod_ctx[0], fn_w_in[0], fn_w_grp[0], fn_w_out[0])
    return _delta_layer_final(h_lat, h_ctx, norm_g[1], mod_lat[1], mod_ctx[1], dn_w_in[0], dn_conv[0],
                              dn_a_log[0], dn_dt_bias[0], dn_norm_g[0], dn_w_out[0], final_g)
```

```python
import functools
import math

import numpy as np
import jax
import jax.numpy as jnp
from jax import lax
from jax.experimental import pallas as pl
from jax.experimental.pallas import tpu as pltpu

F32 = jnp.float32
BF16 = jnp.bfloat16
HI = lax.Precision.HIGHEST
EPS = 1e-6
LOG2E = 1.4426950408889634

GRID_W = 64
DN_HEADS = 8
DN_DK = 128
DN_DV = 256
SCAN_CHUNK = 128
DFT_N2 = 128
GB_LANES = 128
VMEM_LIMIT_BYTES = 48 * 1024 * 1024


def _cparams(*sem):
    return pltpu.CompilerParams(dimension_semantics=sem, vmem_limit_bytes=VMEM_LIMIT_BYTES)


def _const_spec(shape):
    zeros = (0,) * len(shape)
    return pl.BlockSpec(shape, lambda *_: zeros, pipeline_mode=pl.Buffered(1))


def _silu(v):
    return v * jax.nn.sigmoid(v)


def _mods_kernel(cond_ref, w_ref, b_ref, o_ref):
    s = _silu(cond_ref[...])
    o_ref[0] = jnp.dot(s, w_ref[0], preferred_element_type=F32, precision=HI) + b_ref[0]


def _mods(cond, mod_w, mod_b):
    depth, d, d3 = mod_w.shape
    rows = cond.shape[0]
    tn = 1024
    assert d3 % tn == 0, (d3, tn)
    return pl.pallas_call(
        _mods_kernel,
        out_shape=jax.ShapeDtypeStruct((depth, rows, d3), F32),
        grid=(depth, d3 // tn),
        in_specs=[pl.BlockSpec((rows, d), lambda i, j: (0, 0)),
                  pl.BlockSpec((1, d, tn), lambda i, j: (i, 0, j)),
                  pl.BlockSpec((1, 1, tn), lambda i, j: (i, 0, j))],
        out_specs=pl.BlockSpec((1, rows, tn), lambda i, j: (i, 0, j)),
        compiler_params=_cparams("parallel", "parallel"), name="adaln_mods",
    )(cond, mod_w, mod_b.reshape(depth, 1, d3))


def _norm_mod(x, g_ref, mod_ref, d):
    y = x * lax.rsqrt(jnp.mean(x * x, axis=-1, keepdims=True) + EPS) * g_ref[...]
    return y * (1.0 + mod_ref[0, :, d:2 * d]) + mod_ref[0, :, 0:d]


def _nmm_kernel(x_ref, g_ref, mod_ref, w_ref, *outs, splits, d):
    n = _norm_mod(x_ref[0], g_ref, mod_ref, d).astype(BF16)
    c0 = 0
    for o_ref, width in zip(outs, splits):
        step = min(width, 1024)
        for s in range(0, width, step):
            r = jnp.dot(n, w_ref[:, c0 + s:c0 + s + step], preferred_element_type=F32)
            o_ref[0, :, s:s + step] = r.astype(o_ref.dtype)
        c0 += width


def _mod_map(mod):
    return (lambda i, j: (i, 0, 0)) if mod.shape[0] > 1 else (lambda i, j: (0, 0, 0))


def _nmm(x, g, mod, w, splits, dtypes, tm=1024):
    b, l, d = x.shape
    tm = min(tm, l)
    assert l % tm == 0 and sum(splits) == w.shape[1], (l, tm, splits, w.shape)
    return pl.pallas_call(
        functools.partial(_nmm_kernel, splits=splits, d=d),
        out_shape=[jax.ShapeDtypeStruct((b, l, wd), dt) for wd, dt in zip(splits, dtypes)],
        grid=(b, l // tm),
        in_specs=[pl.BlockSpec((1, tm, d), lambda i, j: (i, j, 0)),
                  _const_spec((1, d)),
                  pl.BlockSpec((1, 1, 3 * d), _mod_map(mod)),
                  _const_spec(w.shape)],
        out_specs=[pl.BlockSpec((1, tm, wd), lambda i, j: (i, j, 0)) for wd in splits],
        compiler_params=_cparams("parallel", "parallel"), name="norm_proj",
    )(x, g.reshape(1, d), mod, w)


def _dft_cos_sin(n):
    k = np.arange(n)
    ang = 2.0 * np.pi * ((k[:, None] * k[None, :]) % n) / n
    return np.cos(ang), np.sin(ang)


def _dft1_kernel(m_ref, u_ref, cos_ref, sin_ref, o_ref, *, n1, tn2):
    ut = jnp.transpose(u_ref[0], (1, 0, 2))
    for j in range(tn2):
        r = jnp.dot(m_ref[...], ut[j], preferred_element_type=F32)
        yr = r[:n1]
        yi = r[n1:]
        cs = cos_ref[j]
        sn = sin_ref[j]
        o_ref[0, 0, j] = (yr * cs + yi * sn).astype(o_ref.dtype)
        o_ref[0, 1, j] = (yi * cs - yr * sn).astype(o_ref.dtype)


def _dft1(u, n1, n2, tn2=16, tc=2048):
    b, l, c = u.shape
    assert l == n1 * n2 and n2 % tn2 == 0 and c % tc == 0, (l, n1, n2, tn2, c, tc)
    cm, sm = _dft_cos_sin(n1)
    m1 = jnp.asarray(np.concatenate([cm, -sm], axis=0), F32).astype(BF16)
    ang = 2.0 * np.pi * (np.arange(n2)[:, None] * np.arange(n1)[None, :]) / l
    tw_cos = jnp.asarray(np.cos(ang)[:, :, None], F32)
    tw_sin = jnp.asarray(np.sin(ang)[:, :, None], F32)
    return pl.pallas_call(
        functools.partial(_dft1_kernel, n1=n1, tn2=tn2),
        out_shape=jax.ShapeDtypeStruct((b, 2, n2, n1, c), BF16),
        grid=(b, n2 // tn2, c // tc),
        in_specs=[_const_spec((2 * n1, n1)),
                  pl.BlockSpec((1, n1, tn2, tc), lambda i, j, k: (i, 0, j, k)),
                  pl.BlockSpec((tn2, n1, 1), lambda i, j, k: (j, 0, 0)),
                  pl.BlockSpec((tn2, n1, 1), lambda i, j, k: (j, 0, 0))],
        out_specs=pl.BlockSpec((1, 2, tn2, n1, tc), lambda i, j, k: (i, 0, j, 0, k)),
        compiler_params=_cparams("parallel", "parallel", "parallel"), name="dft_stage1",
    )(m1, u.reshape(b, n1, n2, c), tw_cos, tw_sin)


def _dft2_lat_kernel(m_ref, y_ref, xr_ref, xi_ref, *, n2, tk1):
    yr = jnp.transpose(y_ref[0, 0], (1, 0, 2))
    yi = jnp.transpose(y_ref[0, 1], (1, 0, 2))
    xr, xi = [], []
    for j in range(tk1):
        r = jnp.dot(m_ref[...], jnp.concatenate([yr[j], yi[j]], axis=0), preferred_element_type=F32)
        xr.append(r[:n2].astype(xr_ref.dtype))
        xi.append(r[n2:].astype(xi_ref.dtype))
    xr_ref[0] = jnp.transpose(jnp.stack(xr), (1, 0, 2))
    xi_ref[0] = jnp.transpose(jnp.stack(xi), (1, 0, 2))


def _dft2_lat(m, y, tk1=16, tc=1024):
    b, _, n2, n1, c = y.shape
    tk1 = min(tk1, n1)
    assert n1 % tk1 == 0 and c % tc == 0, (n1, tk1, c, tc)
    return pl.pallas_call(
        functools.partial(_dft2_lat_kernel, n2=n2, tk1=tk1),
        out_shape=[jax.ShapeDtypeStruct((b, n2, n1, c), BF16)] * 2,
        grid=(b, n1 // tk1, c // tc),
        in_specs=[_const_spec(m.shape),
                  pl.BlockSpec((1, 2, n2, tk1, tc), lambda i, j, k: (i, 0, 0, j, k))],
        out_specs=[pl.BlockSpec((1, n2, tk1, tc), lambda i, j, k: (i, 0, j, k))] * 2,
        compiler_params=_cparams("parallel", "parallel", "parallel"), name="dft_stage2_lat",
    )(m, y)


def _dft2_kernel(m_ref, y_ref, xr_ref, xi_ref, *, half):
    r = jnp.dot(m_ref[...], y_ref[0], preferred_element_type=F32)
    xr_ref[0] = r[:half].astype(xr_ref.dtype)
    xi_ref[0] = r[half:].astype(xi_ref.dtype)


def _dft2(m, y, tc=2048):
    b, k, cols = y.shape
    half = m.shape[0] // 2
    assert cols % tc == 0, (cols, tc)
    return pl.pallas_call(
        functools.partial(_dft2_kernel, half=half),
        out_shape=[jax.ShapeDtypeStruct((b, half, cols), BF16)] * 2,
        grid=(b, cols // tc),
        in_specs=[_const_spec(m.shape),
                  pl.BlockSpec((1, k, tc), lambda i, j: (i, 0, j))],
        out_specs=[pl.BlockSpec((1, half, tc), lambda i, j: (i, 0, j))] * 2,
        compiler_params=_cparams("parallel", "parallel"), name="dft_stage2",
    )(m, y)


def _dft_real_matrix(n):
    cm, sm = _dft_cos_sin(n)
    return jnp.asarray(np.concatenate([cm, -sm], axis=0), F32).astype(BF16)


def _dft_complex_matrix(n):
    cm, sm = _dft_cos_sin(n)
    return jnp.asarray(np.block([[cm, sm], [-sm, cm]]), F32).astype(BF16)


def _grp_kernel(m_ref, w_ref, g_ref):
    g_ref[0] = jnp.dot(m_ref[...], w_ref[0], preferred_element_type=F32, precision=HI).astype(g_ref.dtype)


def _grp(w_grp):
    groups, n, _ = w_grp.shape
    cm, sm = _dft_cos_sin(n)
    m = jnp.asarray(np.concatenate([cm, sm], axis=0), F32)
    return pl.pallas_call(
        _grp_kernel,
        out_shape=jax.ShapeDtypeStruct((groups, 2 * n, n), BF16),
        grid=(groups,),
        in_specs=[_const_spec((2 * n, n)),
                  pl.BlockSpec((1, n, n), lambda g: (g, 0, 0))],
        out_specs=pl.BlockSpec((1, 2 * n, n), lambda g: (g, 0, 0)),
        compiler_params=_cparams("parallel"), name="group_weights",
    )(m, w_grp)


def _fn_out_kernel(xr_ref, xi_ref, z_ref, res_ref, mod_ref, gw_ref, w_ref, o_ref, *, groups, gd, d, scale):
    ys = []
    for g in range(groups):
        sl = slice(g * gd, (g + 1) * gd)
        f = jnp.dot(jnp.concatenate([xr_ref[0, :, sl], xi_ref[0, :, sl]], axis=1), gw_ref[g],
                    preferred_element_type=F32)
        ys.append((f * scale * _silu(z_ref[0, :, sl].astype(F32))).astype(BF16))
    acc = jnp.dot(jnp.concatenate(ys, axis=1), w_ref[...], preferred_element_type=F32)
    gate = mod_ref[0, :, 2 * d:3 * d]
    o_ref[0] = res_ref[0] + gate * acc


def _fn_out(xr, xi, z, res, mod, gw, w_out, scale, tm=512):
    b, l, c = xr.shape
    d = res.shape[-1]
    tm = min(tm, l)
    assert l % tm == 0, (l, tm)
    groups, _, gd = gw.shape
    tok = lambda wd: pl.BlockSpec((1, tm, wd), lambda i, j: (i, j, 0))
    return pl.pallas_call(
        functools.partial(_fn_out_kernel, groups=groups, gd=gd, d=d, scale=scale),
        out_shape=jax.ShapeDtypeStruct((b, l, d), F32),
        grid=(b, l // tm),
        in_specs=[tok(c), tok(c), tok(c), tok(d),
                  pl.BlockSpec((1, 1, 3 * d), _mod_map(mod)),
                  _const_spec(gw.shape), _const_spec(w_out.shape)],
        out_specs=tok(d),
        compiler_params=_cparams("parallel", "parallel"), name="fourier_out",
    )(xr, xi, z, res, mod, gw, w_out)


def _proj_conv_kernel(xp_ref, xm_ref, xn_ref, g_ref, mod_ref, w_ref, wab_ref, abp_ref, cw_ref, sh_ref,
                      q_ref, k_ref, v_ref, z_ref, gb_ref, *, d, cols, tr, rp, pr, halo, nheads, dk, dv, cb):
    i = pl.program_id(1)
    last = pl.num_programs(1) - 1
    if halo:
        n_prev = (_norm_mod(xp_ref[0], g_ref, mod_ref, d) * (i > 0).astype(F32)).astype(BF16)
    n_main = _norm_mod(xm_ref[0], g_ref, mod_ref, d).astype(BF16)
    if halo:
        n_next = (_norm_mod(xn_ref[0], g_ref, mod_ref, d) * (i < last).astype(F32)).astype(BF16)
        ne = jnp.concatenate([n_prev, n_main, n_next], axis=0)
    qk_w = nheads * dk
    v_w = nheads * dv
    z0 = 2 * qk_w + v_w
    n_in = (tr + 2) * cols // pr
    n_out = tr * cols // rp

    def project(c0, p):
        if halo:
            rows = ne[p * pr:(p + 1) * pr]
        elif 0 < p < n_in - 1:
            rows = n_main[(p - 1) * pr:p * pr]
        else:
            return jnp.zeros((pr, cb), F32)
        return jnp.dot(rows, w_ref[:, c0:c0 + cb], preferred_element_type=F32)

    pieces = [project(0, p) for p in range(n_in)]
    zstep = v_w * cb // z0
    for c0 in range(0, z0, cb):
        nxt = []
        for j in range(n_out):
            if c0 + cb < z0:
                nxt += [project(c0 + cb, p) for p in range(len(nxt), (j + 1) * n_in // n_out)]
            if j == n_out // 2:
                zs = c0 // cb * zstep
                z_ref[0, :, zs:zs + zstep] = jnp.dot(
                    n_main, w_ref[:, z0 + zs:z0 + zs + zstep], preferred_element_type=F32).astype(z_ref.dtype)
            lo, hi = j * rp, j * rp + rp + 2 * cols
            p_lo, p_hi = lo // pr, (hi - 1) // pr
            e = jnp.concatenate(pieces[p_lo:p_hi + 1], axis=0)[lo - p_lo * pr:hi - p_lo * pr]
            part = []
            for dc in range(3):
                acc = None
                for dr in range(3):
                    term = e[dr * cols:dr * cols + rp] * cw_ref[dr, dc:dc + 1, c0:c0 + cb]
                    acc = term if acc is None else acc + term
                part.append(acc)
            y = _silu(jnp.dot(sh_ref[...], jnp.concatenate(part, axis=0).astype(BF16), preferred_element_type=F32))
            rows = slice(j * rp, (j + 1) * rp)
            if c0 < 2 * qk_w:
                o_ref, scale, base = (q_ref, dk ** -0.5, c0) if c0 < qk_w else (k_ref, 1.0, c0 - qk_w)
                for h in range(cb // dk):
                    yh = y[:, h * dk:(h + 1) * dk]
                    yh = yh * lax.rsqrt(jnp.sum(yh * yh, axis=-1, keepdims=True) + EPS) * scale
                    o_ref[0, base // dk + h, rows, :] = yh.astype(o_ref.dtype)
            else:
                base = c0 - 2 * qk_w
                for h in range(cb // dv):
                    v_ref[0, base // dv + h, rows, :] = y[:, h * dv:(h + 1) * dv].astype(v_ref.dtype)
        pieces = nxt
    r = jnp.dot(n_main, wab_ref[...], preferred_element_type=F32)
    t = r + abp_ref[1:2, :]
    softplus = jnp.maximum(t, 0.0) + jnp.log(1.0 + jnp.exp(-jnp.abs(t)))
    lane = lax.broadcasted_iota(jnp.int32, r.shape, 1)
    gb_ref[0] = jnp.where(lane < 2 * nheads, (-LOG2E) * jnp.exp(abp_ref[0:1, :]) * softplus, jax.nn.sigmoid(r))


def _proj_conv(x, g, mod, w, w_ab, abp, conv_w, rows, cols, cb=512):
    b, l, d = x.shape
    tr = min(rows, max(1, 512 // cols))
    assert l == rows * cols and rows % tr == 0, (l, rows, cols, tr)
    n = tr * cols
    rp = cols * min(tr, 2)
    halo = rows > tr or rp != cols
    half = (tr + 2) * cols // 2
    pr = half if halo and half % 16 == 0 else rp
    t = np.arange(rp)
    left = (t[:, None] - 1 == t[None, :]) & (t[:, None] % cols != 0)
    right = (t[:, None] + 1 == t[None, :]) & (t[:, None] % cols != cols - 1)
    shift = jnp.asarray(np.concatenate([left, np.eye(rp, dtype=bool), right], axis=1), F32).astype(BF16)
    heads = lambda wd: pl.BlockSpec((1, DN_HEADS, n, wd), lambda bi, i: (bi, 0, i, 0))
    tok = lambda wd: pl.BlockSpec((1, n, wd), lambda bi, i: (bi, i, 0))
    return pl.pallas_call(
        functools.partial(_proj_conv_kernel, d=d, cols=cols, tr=tr, rp=rp, pr=pr, halo=halo,
                          nheads=DN_HEADS, dk=DN_DK, dv=DN_DV, cb=cb),
        out_shape=[jax.ShapeDtypeStruct((b, DN_HEADS, l, DN_DK), BF16),
                   jax.ShapeDtypeStruct((b, DN_HEADS, l, DN_DK), BF16),
                   jax.ShapeDtypeStruct((b, DN_HEADS, l, DN_DV), BF16),
                   jax.ShapeDtypeStruct((b, l, DN_HEADS * DN_DV), BF16),
                   jax.ShapeDtypeStruct((b, l, GB_LANES), F32)],
        grid=(b, rows // tr),
        in_specs=[pl.BlockSpec((1, cols, d), lambda bi, i: (bi, jnp.maximum(i * tr - 1, 0), 0)),
                  pl.BlockSpec((1, n, d), lambda bi, i: (bi, i, 0)),
                  pl.BlockSpec((1, cols, d), lambda bi, i: (bi, jnp.minimum((i + 1) * tr, rows - 1), 0)),
                  _const_spec((1, d)),
                  pl.BlockSpec((1, 1, 3 * d), _mod_map(mod)),
                  _const_spec(w.shape), _const_spec(w_ab.shape), _const_spec(abp.shape), _const_spec(conv_w.shape),
                  _const_spec(shift.shape)],
        out_specs=[heads(DN_DK), heads(DN_DK), heads(DN_DV), tok(DN_HEADS * DN_DV), tok(GB_LANES)],
        compiler_params=_cparams("parallel", "parallel"), name="proj_conv",
    )(x, x, x, g.reshape(1, d), mod, w, w_ab, abp, conv_w, shift)


def _delta_kernel(qa_ref, ka_ref, va_ref, ga_ref, qb_ref, kb_ref, vb_ref, gb_ref, s0_ref,
                  of_ref, ob_ref, sfin_ref, s_scr, sb_scr, *, nheads, cs, nsub):
    n = pl.program_id(1)

    @pl.when(n == 0)
    def _():
        s_scr[...] = s0_ref[0]
        sb_scr[...] = s0_ref[0].astype(sb_scr.dtype)

    ri = lax.broadcasted_iota(jnp.int32, (cs, cs), 0)
    ci = lax.broadcasted_iota(jnp.int32, (cs, cs), 1)
    eye = ri == ci
    xor = ri ^ ci
    nt = (((1,), (1,)), ((), ()))
    levels = int(math.log2(cs))
    def chunk_pair(rows):
        chains = []
        for d, (q_ref, k_ref, v_ref, g_ref, o_ref) in enumerate(((qa_ref, ka_ref, va_ref, ga_ref, of_ref),
                                                                  (qb_ref, kb_ref, vb_ref, gb_ref, ob_ref))):
            incl = (ri >= ci) if d == 0 else (ri <= ci)
            strict = (ri > ci) if d == 0 else (ri < ci)
            gbv = g_ref[0, rows[d], :]
            hi = gbv.astype(BF16)
            rem = gbv - hi.astype(F32)
            mid = rem.astype(BF16)
            t01 = jnp.where(incl, 1.0, 0.0).astype(BF16)
            cum = sum(jnp.dot(t01, p, preferred_element_type=F32)
                      for p in (hi, mid, (rem - mid.astype(F32)).astype(BF16)))
            cum_t = cum.T
            last = cs - 1 if d == 0 else 0
            for hd in range(nheads):
                qb = q_ref[0, hd, rows[d], :]
                kb = k_ref[0, hd, rows[d], :]
                k = kb.astype(F32)
                v = v_ref[0, hd, rows[d], :].astype(F32)
                kk = lax.dot_general(kb, kb, nt, preferred_element_type=F32)
                qk = lax.dot_general(qb, kb, nt, preferred_element_type=F32)
                lg = d * nheads + hd
                lb = (2 + d) * nheads + hd
                gc_col = cum[:, lg:lg + 1]
                gc_row = cum_t[lg:lg + 1, :]
                g_tot = cum[last:last + 1, lg:lg + 1]
                b_col = gbv[:, lb:lb + 1]
                ex = jnp.exp2(gc_col - gc_row)
                egc = jnp.exp2(gc_col)
                chains.append(dict(
                    hd=hd, d=d, o_ref=o_ref, eg=jnp.exp2(g_tot),
                    qh=(qb.astype(F32) * egc).astype(BF16),
                    qkd=jnp.where(incl, qk * ex, 0.0).astype(BF16),
                    ktt=(k.T * jnp.exp2(g_tot - gc_row)).astype(BF16),
                    rhs=jnp.concatenate([v * b_col, k * (b_col * egc)], axis=1).astype(BF16),
                    neg_off=jnp.where(strict, kk * (-b_col) * ex, 0.0)))
        dmats = [jnp.where(eye, 1.0, jnp.where(xor == 1, ch["neg_off"], 0.0)) for ch in chains]
        neg_offs = [ch["neg_off"].astype(BF16) for ch in chains]
        for lv in range(1, levels):
            sel = jnp.where((xor >> lv) == 1, 1.0, 0.0).astype(BF16)
            dmbs = [dm.astype(BF16) for dm in dmats]
            des = [jnp.dot(db, no * sel, preferred_element_type=F32) for db, no in zip(dmbs, neg_offs)]
            dmats = [dm + jnp.dot(de.astype(BF16), db, preferred_element_type=F32)
                     for dm, de, db in zip(dmats, des, dmbs)]
        sols = [jnp.dot(dm.astype(BF16), ch["rhs"], preferred_element_type=F32) for dm, ch in zip(dmats, chains)]
        dv = of_ref.shape[-1]
        r1s = [jnp.dot(jnp.concatenate([sol[:, dv:].astype(BF16), ch["qh"]], axis=0), sb_scr[ch["hd"], ch["d"]],
                       preferred_element_type=F32) for ch, sol in zip(chains, sols)]
        vns = [(sol[:, :dv] - r1[:cs]).astype(BF16) for sol, r1 in zip(sols, r1s)]
        r2s = [jnp.dot(jnp.concatenate([ch["qkd"], ch["ktt"]], axis=0), vn, preferred_element_type=F32)
               for ch, vn in zip(chains, vns)]
        for ch, r1, r2 in zip(chains, r1s, r2s):
            hd, d = ch["hd"], ch["d"]
            ch["o_ref"][0, hd, rows[d], :] = (r1[cs:] + r2[:cs]).astype(of_ref.dtype)
            s_new = s_scr[hd, d] * ch["eg"] + r2[cs:]
            s_scr[hd, d] = s_new
            sb_scr[hd, d] = s_new.astype(sb_scr.dtype)

    for sub in range(nsub):
        chunk_pair((slice(sub * cs, (sub + 1) * cs), slice((nsub - 1 - sub) * cs, (nsub - sub) * cs)))

    @pl.when(n == pl.num_programs(1) - 1)
    def _():
        sfin_ref[0] = s_scr[...]


def _delta(q, k, v, gb, s0, cs=SCAN_CHUNK):
    b, nh, l, dk = q.shape
    dv = v.shape[-1]
    nsub = 2
    assert l % (nsub * cs) == 0 and cs == dk, (l, cs, dk)
    nc = l // (nsub * cs)
    fwd = lambda wd: pl.BlockSpec((1, nh, nsub * cs, wd), lambda bi, n: (bi, 0, n, 0))
    bwd = lambda wd: pl.BlockSpec((1, nh, nsub * cs, wd), lambda bi, n: (bi, 0, nc - 1 - n, 0))
    st = pl.BlockSpec((1, nh, 2, dk, dv), lambda bi, n: (bi, 0, 0, 0, 0))
    return pl.pallas_call(
        functools.partial(_delta_kernel, nheads=nh, cs=cs, nsub=nsub),
        out_shape=[jax.ShapeDtypeStruct((b, nh, l, dv), BF16),
                   jax.ShapeDtypeStruct((b, nh, l, dv), BF16),
                   jax.ShapeDtypeStruct((b, nh, 2, dk, dv), F32)],
        grid=(b, nc),
        in_specs=[fwd(dk), fwd(dk), fwd(dv), pl.BlockSpec((1, nsub * cs, GB_LANES), lambda bi, n: (bi, n, 0)),
                  bwd(dk), bwd(dk), bwd(dv),
                  pl.BlockSpec((1, nsub * cs, GB_LANES), lambda bi, n: (bi, nc - 1 - n, 0)),
                  st],
        out_specs=[fwd(dv), bwd(dv), st],
        scratch_shapes=[pltpu.VMEM((nh, 2, dk, dv), F32), pltpu.VMEM((nh, 2, dk, dv), BF16)],
        compiler_params=_cparams("parallel", "arbitrary"), name="delta_rule",
    )(q, k, v, gb, q, k, v, gb, s0)


def _dn_out_kernel(of_ref, ob_ref, z_ref, res_ref, mod_ref, ng_ref, w_ref, fg_ref, o_ref, *, nheads, dv, d):
    acc = None
    for hd in range(nheads):
        sl = slice(hd * dv, (hd + 1) * dv)
        o = of_ref[0, hd].astype(F32) + ob_ref[0, hd].astype(F32)
        o = o * lax.rsqrt(jnp.mean(o * o, axis=-1, keepdims=True) + EPS) * ng_ref[...]
        y = (o * _silu(z_ref[0, :, sl].astype(F32))).astype(BF16)
        p = jnp.dot(y, w_ref[sl, :], preferred_element_type=F32)
        acc = p if acc is None else acc + p
    gate = mod_ref[0, :, 2 * d:3 * d]
    hn = res_ref[0] + gate * acc
    o_ref[0] = hn * lax.rsqrt(jnp.mean(hn * hn, axis=-1, keepdims=True) + EPS) * fg_ref[...]


def _dn_out(o_f, o_b, z, res, mod, norm_g, w_out, final_g, tm=512):
    b, nh, l, dv = o_f.shape
    d = res.shape[-1]
    tm = min(tm, l)
    assert l % tm == 0, (l, tm)
    tok = lambda wd: pl.BlockSpec((1, tm, wd), lambda i, j: (i, j, 0))
    hs = pl.BlockSpec((1, nh, tm, dv), lambda i, j: (i, 0, j, 0))
    return pl.pallas_call(
        functools.partial(_dn_out_kernel, nheads=nh, dv=dv, d=d),
        out_shape=jax.ShapeDtypeStruct((b, l, d), F32),
        grid=(b, l // tm),
        in_specs=[hs, hs, tok(nh * dv), tok(d),
                  pl.BlockSpec((1, 1, 3 * d), lambda i, j: (i, 0, 0)),
                  _const_spec((1, dv)), _const_spec(w_out.shape), _const_spec((1, d))],
        out_specs=tok(d),
        compiler_params=_cparams("parallel", "parallel"), name="delta_out",
    )(o_f, o_b, z, res, mod, norm_g.reshape(1, dv), w_out, final_g.reshape(1, d))


def _fourier_layer(h_lat, h_ctx, norm_g, mod_lat, mod_ctx, w_in, w_grp, w_out):
    b, l, _ = h_lat.shape
    lc = h_ctx.shape[1]
    width = w_in.shape[1] // 2
    gd = w_grp.shape[-1]
    w_in = w_in.astype(BF16)
    w_out = w_out.astype(BF16)
    gw = _grp(w_grp)

    u, z = _nmm(h_lat, norm_g, mod_lat, w_in, (width, width), (BF16, BF16))
    n2 = DFT_N2
    n1 = l // n2
    y = _dft1(u, n1, n2)
    xr, xi = _dft2_lat(_dft_complex_matrix(n2), y)
    xr = xr.reshape(b, l, width)
    xi = xi.reshape(b, l, width)
    out_lat = _fn_out(xr, xi, z, h_lat, mod_lat, gw, w_out, 1.0 / math.sqrt(l * gd))

    uc, zc = _nmm(h_ctx, norm_g, mod_ctx, w_in, (width, width), (BF16, BF16))
    xrc, xic = _dft2(_dft_real_matrix(lc), uc)
    out_ctx = _fn_out(xrc, xic, zc, h_ctx, mod_ctx, gw, w_out, 1.0 / math.sqrt(lc * gd))
    return out_lat, out_ctx


def _delta_layer_final(h_lat, h_ctx, norm_g, mod_lat, mod_ctx, w_in, conv_w, a_log, dt_bias, dn_norm_g, w_out,
                       final_g):
    b, l, d = h_lat.shape
    lc = h_ctx.shape[1]
    main_w = 2 * DN_HEADS * DN_DK + 2 * DN_HEADS * DN_DV
    n_ab = 4 * DN_HEADS
    w_main = w_in.astype(BF16)
    w_ab = jnp.pad(w_in[:, main_w:], ((0, 0), (0, GB_LANES - n_ab))).astype(BF16)
    pad = jnp.zeros((GB_LANES - 2 * DN_HEADS,), F32)
    abp = jnp.stack([jnp.concatenate([a_log.reshape(-1).astype(F32), pad]),
                     jnp.concatenate([dt_bias.reshape(-1).astype(F32), pad])])
    qc, kc, vc, _, gbc = _proj_conv(h_ctx, norm_g, mod_ctx, w_main, w_ab, abp, conv_w, 1, lc)
    ql, kl, vl, zl, gbl = _proj_conv(h_lat, norm_g, mod_lat, w_main, w_ab, abp, conv_w, l // GRID_W, GRID_W)
    s0 = jnp.zeros((b, DN_HEADS, 2, DN_DK, DN_DV), F32)
    _, _, s_ctx = _delta(qc, kc, vc, gbc, s0)
    o_f, o_b, _ = _delta(ql, kl, vl, gbl, s_ctx)
    return _dn_out(o_f, o_b, zl, h_lat, mod_lat, dn_norm_g, w_out.astype(BF16), final_g)


def kernel(x, c, ctx, c_ctx, mod_w, mod_b, norm_g, final_g, fn_w_in, fn_w_grp, fn_w_out,
           dn_w_in, dn_conv, dn_a_log, dn_dt_bias, dn_norm_g, dn_w_out):
    b, l, d = x.shape
    cond = jnp.concatenate([c, c_ctx[None, :], jnp.zeros((8 - b - 1, d), c.dtype)], axis=0)
    mods = _mods(cond, mod_w, mod_b)
    mod_lat = [mods[i, :b].reshape(b, 1, 3 * d) for i in range(2)]
    mod_ctx = [mods[i, b:b + 1].reshape(1, 1, 3 * d) for i in range(2)]
    h_lat, h_ctx = _fourier_layer(x, ctx, norm_g[0], mod_lat[0], mod_ctx[0], fn_w_in[0], fn_w_grp[0], fn_w_out[0])
    return _delta_layer_final(h_lat, h_ctx, norm_g[1], mod_lat[1], mod_ctx[1], dn_w_in[0], dn_conv[0],
                              dn_a_log[0], dn_dt_bias[0], dn_norm_g[0], dn_w_out[0], final_g)
```

```python
import functools
import math

import numpy as np
import jax
import jax.numpy as jnp
from jax import lax
from jax.experimental import pallas as pl
from jax.experimental.pallas import tpu as pltpu

F32 = jnp.float32
BF16 = jnp.bfloat16
HI = lax.Precision.HIGHEST
EPS = 1e-6
LOG2E = 1.4426950408889634

GRID_W = 64
DN_HEADS = 8
DN_DK = 128
DN_DV = 256
SCAN_CHUNK = 128
DFT_N2 = 128
GB_LANES = 128
VMEM_LIMIT_BYTES = 48 * 1024 * 1024


def _cparams(*sem):
    return pltpu.CompilerParams(dimension_semantics=sem, vmem_limit_bytes=VMEM_LIMIT_BYTES)


def _const_spec(shape):
    zeros = (0,) * len(shape)
    return pl.BlockSpec(shape, lambda *_: zeros, pipeline_mode=pl.Buffered(1))


def _silu(v):
    return v * jax.nn.sigmoid(v)


def _mods_kernel(cond_ref, w_ref, b_ref, o_ref):
    s = _silu(cond_ref[...])
    o_ref[0] = jnp.dot(s, w_ref[0], preferred_element_type=F32, precision=HI) + b_ref[0]


def _mods(cond, mod_w, mod_b):
    depth, d, d3 = mod_w.shape
    rows = cond.shape[0]
    tn = 1024
    assert d3 % tn == 0, (d3, tn)
    return pl.pallas_call(
        _mods_kernel,
        out_shape=jax.ShapeDtypeStruct((depth, rows, d3), F32),
        grid=(depth, d3 // tn),
        in_specs=[pl.BlockSpec((rows, d), lambda i, j: (0, 0)),
                  pl.BlockSpec((1, d, tn), lambda i, j: (i, 0, j)),
                  pl.BlockSpec((1, 1, tn), lambda i, j: (i, 0, j))],
        out_specs=pl.BlockSpec((1, rows, tn), lambda i, j: (i, 0, j)),
        compiler_params=_cparams("parallel", "parallel"), name="adaln_mods",
    )(cond, mod_w, mod_b.reshape(depth, 1, d3))


def _norm_mod(x, g_ref, mod_ref, d):
    y = x * lax.rsqrt(jnp.mean(x * x, axis=-1, keepdims=True) + EPS) * g_ref[...]
    return y * (1.0 + mod_ref[0, :, d:2 * d]) + mod_ref[0, :, 0:d]


def _nmm_kernel(x_ref, g_ref, mod_ref, w_ref, *outs, splits, d):
    n = _norm_mod(x_ref[0], g_ref, mod_ref, d).astype(BF16)
    c0 = 0
    for o_ref, width in zip(outs, splits):
        step = min(width, 1024)
        for s in range(0, width, step):
            r = jnp.dot(n, w_ref[:, c0 + s:c0 + s + step], preferred_element_type=F32)
            o_ref[0, :, s:s + step] = r.astype(o_ref.dtype)
        c0 += width


def _mod_map(mod):
    return (lambda i, j: (i, 0, 0)) if mod.shape[0] > 1 else (lambda i, j: (0, 0, 0))


def _nmm(x, g, mod, w, splits, dtypes, tm=1024):
    b, l, d = x.shape
    tm = min(tm, l)
    assert l % tm == 0 and sum(splits) == w.shape[1], (l, tm, splits, w.shape)
    return pl.pallas_call(
        functools.partial(_nmm_kernel, splits=splits, d=d),
        out_shape=[jax.ShapeDtypeStruct((b, l, wd), dt) for wd, dt in zip(splits, dtypes)],
        grid=(b, l // tm),
        in_specs=[pl.BlockSpec((1, tm, d), lambda i, j: (i, j, 0)),
                  _const_spec((1, d)),
                  pl.BlockSpec((1, 1, 3 * d), _mod_map(mod)),
                  _const_spec(w.shape)],
        out_specs=[pl.BlockSpec((1, tm, wd), lambda i, j: (i, j, 0)) for wd in splits],
        compiler_params=_cparams("parallel", "parallel"), name="norm_proj",
    )(x, g.reshape(1, d), mod, w)


def _dft_cos_sin(n):
    k = np.arange(n)
    ang = 2.0 * np.pi * ((k[:, None] * k[None, :]) % n) / n
    return np.cos(ang), np.sin(ang)


def _dft1_kernel(m_ref, u_ref, cos_ref, sin_ref, o_ref, *, n1, tn2):
    ut = jnp.transpose(u_ref[0], (1, 0, 2))
    for j in range(tn2):
        r = jnp.dot(m_ref[...], ut[j], preferred_element_type=F32)
        yr = r[:n1]
        yi = r[n1:]
        cs = cos_ref[j]
        sn = sin_ref[j]
        o_ref[0, 0, j] = (yr * cs + yi * sn).astype(o_ref.dtype)
        o_ref[0, 1, j] = (yi * cs - yr * sn).astype(o_ref.dtype)


def _dft1(u, n1, n2, tn2=16, tc=2048):
    b, l, c = u.shape
    assert l == n1 * n2 and n2 % tn2 == 0 and c % tc == 0, (l, n1, n2, tn2, c, tc)
    cm, sm = _dft_cos_sin(n1)
    m1 = jnp.asarray(np.concatenate([cm, -sm], axis=0), F32).astype(BF16)
    ang = 2.0 * np.pi * (np.arange(n2)[:, None] * np.arange(n1)[None, :]) / l
    tw_cos = jnp.asarray(np.cos(ang)[:, :, None], F32)
    tw_sin = jnp.asarray(np.sin(ang)[:, :, None], F32)
    return pl.pallas_call(
        functools.partial(_dft1_kernel, n1=n1, tn2=tn2),
        out_shape=jax.ShapeDtypeStruct((b, 2, n2, n1, c), BF16),
        grid=(b, n2 // tn2, c // tc),
        in_specs=[_const_spec((2 * n1, n1)),
                  pl.BlockSpec((1, n1, tn2, tc), lambda i, j, k: (i, 0, j, k)),
                  pl.BlockSpec((tn2, n1, 1), lambda i, j, k: (j, 0, 0)),
                  pl.BlockSpec((tn2, n1, 1), lambda i, j, k: (j, 0, 0))],
        out_specs=pl.BlockSpec((1, 2, tn2, n1, tc), lambda i, j, k: (i, 0, j, 0, k)),
        compiler_params=_cparams("parallel", "parallel", "parallel"), name="dft_stage1",
    )(m1, u.reshape(b, n1, n2, c), tw_cos, tw_sin)


def _dft2_lat_kernel(m_ref, y_ref, xr_ref, xi_ref, *, n2, tk1):
    yr = jnp.transpose(y_ref[0, 0], (1, 0, 2))
    yi = jnp.transpose(y_ref[0, 1], (1, 0, 2))
    xr, xi = [], []
    for j in range(tk1):
        r = jnp.dot(m_ref[...], jnp.concatenate([yr[j], yi[j]], axis=0), preferred_element_type=F32)
        xr.append(r[:n2].astype(xr_ref.dtype))
        xi.append(r[n2:].astype(xi_ref.dtype))
    xr_ref[0] = jnp.transpose(jnp.stack(xr), (1, 0, 2))
    xi_ref[0] = jnp.transpose(jnp.stack(xi), (1, 0, 2))


def _dft2_lat(m, y, tk1=16, tc=1024):
    b, _, n2, n1, c = y.shape
    tk1 = min(tk1, n1)
    assert n1 % tk1 == 0 and c % tc == 0, (n1, tk1, c, tc)
    return pl.pallas_call(
        functools.partial(_dft2_lat_kernel, n2=n2, tk1=tk1),
        out_shape=[jax.ShapeDtypeStruct((b, n2, n1, c), BF16)] * 2,
        grid=(b, n1 // tk1, c // tc),
        in_specs=[_const_spec(m.shape),
                  pl.BlockSpec((1, 2, n2, tk1, tc), lambda i, j, k: (i, 0, 0, j, k))],
        out_specs=[pl.BlockSpec((1, n2, tk1, tc), lambda i, j, k: (i, 0, j, k))] * 2,
        compiler_params=_cparams("parallel", "parallel", "parallel"), name="dft_stage2_lat",
    )(m, y)


def _dft2_kernel(m_ref, y_ref, xr_ref, xi_ref, *, half):
    r = jnp.dot(m_ref[...], y_ref[0], preferred_element_type=F32)
    xr_ref[0] = r[:half].astype(xr_ref.dtype)
    xi_ref[0] = r[half:].astype(xi_ref.dtype)


def _dft2(m, y, tc=2048):
    b, k, cols = y.shape
    half = m.shape[0] // 2
    assert cols % tc == 0, (cols, tc)
    return pl.pallas_call(
        functools.partial(_dft2_kernel, half=half),
        out_shape=[jax.ShapeDtypeStruct((b, half, cols), BF16)] * 2,
        grid=(b, cols // tc),
        in_specs=[_const_spec(m.shape),
                  pl.BlockSpec((1, k, tc), lambda i, j: (i, 0, j))],
        out_specs=[pl.BlockSpec((1, half, tc), lambda i, j: (i, 0, j))] * 2,
        compiler_params=_cparams("parallel", "parallel"), name="dft_stage2",
    )(m, y)


def _dft_real_matrix(n):
    cm, sm = _dft_cos_sin(n)
    return jnp.asarray(np.concatenate([cm, -sm], axis=0), F32).astype(BF16)


def _dft_complex_matrix(n):
    cm, sm = _dft_cos_sin(n)
    return jnp.asarray(np.block([[cm, sm], [-sm, cm]]), F32).astype(BF16)


def _grp_kernel(m_ref, w_ref, g_ref):
    g_ref[0] = jnp.dot(m_ref[...], w_ref[0], preferred_element_type=F32, precision=HI).astype(g_ref.dtype)


def _grp(w_grp):
    groups, n, _ = w_grp.shape
    cm, sm = _dft_cos_sin(n)
    m = jnp.asarray(np.concatenate([cm, sm], axis=0), F32)
    return pl.pallas_call(
        _grp_kernel,
        out_shape=jax.ShapeDtypeStruct((groups, 2 * n, n), BF16),
        grid=(groups,),
        in_specs=[_const_spec((2 * n, n)),
                  pl.BlockSpec((1, n, n), lambda g: (g, 0, 0))],
        out_specs=pl.BlockSpec((1, 2 * n, n), lambda g: (g, 0, 0)),
        compiler_params=_cparams("parallel"), name="group_weights",
    )(m, w_grp)


def _fn_out_kernel(xr_ref, xi_ref, z_ref, res_ref, mod_ref, gw_ref, w_ref, o_ref, *, groups, gd, d, scale):
    ys = []
    for g in range(groups):
        sl = slice(g * gd, (g + 1) * gd)
        f = jnp.dot(jnp.concatenate([xr_ref[0, :, sl], xi_ref[0, :, sl]], axis=1), gw_ref[g],
                    preferred_element_type=F32)
        ys.append((f * scale * _silu(z_ref[0, :, sl].astype(F32))).astype(BF16))
    acc = jnp.dot(jnp.concatenate(ys, axis=1), w_ref[...], preferred_element_type=F32)
    gate = mod_ref[0, :, 2 * d:3 * d]
    o_ref[0] = res_ref[0] + gate * acc


def _fn_out(xr, xi, z, res, mod, gw, w_out, scale, tm=512):
    b, l, c = xr.shape
    d = res.shape[-1]
    tm = min(tm, l)
    assert l % tm == 0, (l, tm)
    groups, _, gd = gw.shape
    tok = lambda wd: pl.BlockSpec((1, tm, wd), lambda i, j: (i, j, 0))
    return pl.pallas_call(
        functools.partial(_fn_out_kernel, groups=groups, gd=gd, d=d, scale=scale),
        out_shape=jax.ShapeDtypeStruct((b, l, d), F32),
        grid=(b, l // tm),
        in_specs=[tok(c), tok(c), tok(c), tok(d),
                  pl.BlockSpec((1, 1, 3 * d), _mod_map(mod)),
                  _const_spec(gw.shape), _const_spec(w_out.shape)],
        out_specs=tok(d),
        compiler_params=_cparams("parallel", "parallel"), name="fourier_out",
    )(xr, xi, z, res, mod, gw, w_out)


def _proj_conv_kernel(xp_ref, xm_ref, xn_ref, g_ref, mod_ref, w_ref, wab_ref, abp_ref, cw_ref, sh_ref,
                      q_ref, k_ref, v_ref, z_ref, gb_ref, *, d, cols, tr, rp, pr, halo, nheads, dk, dv, cb):
    i = pl.program_id(1)
    last = pl.num_programs(1) - 1
    if halo:
        n_prev = (_norm_mod(xp_ref[0], g_ref, mod_ref, d) * (i > 0).astype(F32)).astype(BF16)
    n_main = _norm_mod(xm_ref[0], g_ref, mod_ref, d).astype(BF16)
    if halo:
        n_next = (_norm_mod(xn_ref[0], g_ref, mod_ref, d) * (i < last).astype(F32)).astype(BF16)
        ne = jnp.concatenate([n_prev, n_main, n_next], axis=0)
    qk_w = nheads * dk
    v_w = nheads * dv
    z0 = 2 * qk_w + v_w
    n_in = (tr + 2) * cols // pr
    n_out = tr * cols // rp

    def project(c0, p):
        if halo:
            rows = ne[p * pr:(p + 1) * pr]
        elif 0 < p < n_in - 1:
            rows = n_main[(p - 1) * pr:p * pr]
        else:
            return jnp.zeros((pr, cb), F32)
        return jnp.dot(rows, w_ref[:, c0:c0 + cb], preferred_element_type=F32)

    pieces = [project(0, p) for p in range(n_in)]
    zstep = v_w * cb // z0
    for c0 in range(0, z0, cb):
        nxt = []
        for j in range(n_out):
            if c0 + cb < z0:
                nxt += [project(c0 + cb, p) for p in range(len(nxt), (j + 1) * n_in // n_out)]
            if j == n_out // 2:
                zs = c0 // cb * zstep
                z_ref[0, :, zs:zs + zstep] = jnp.dot(
                    n_main, w_ref[:, z0 + zs:z0 + zs + zstep], preferred_element_type=F32).astype(z_ref.dtype)
            lo, hi = j * rp, j * rp + rp + 2 * cols
            p_lo, p_hi = lo // pr, (hi - 1) // pr
            e = jnp.concatenate(pieces[p_lo:p_hi + 1], axis=0)[lo - p_lo * pr:hi - p_lo * pr]
            part = []
            for dc in range(3):
                acc = None
                for dr in range(3):
                    term = e[dr * cols:dr * cols + rp] * cw_ref[dr, dc:dc + 1, c0:c0 + cb]
                    acc = term if acc is None else acc + term
                part.append(acc)
            y = _silu(jnp.dot(sh_ref[...], jnp.concatenate(part, axis=0).astype(BF16), preferred_element_type=F32))
            rows = slice(j * rp, (j + 1) * rp)
            if c0 < 2 * qk_w:
                o_ref, scale, base = (q_ref, dk ** -0.5, c0) if c0 < qk_w else (k_ref, 1.0, c0 - qk_w)
                for h in range(cb // dk):
                    yh = y[:, h * dk:(h + 1) * dk]
                    yh = yh * lax.rsqrt(jnp.sum(yh * yh, axis=-1, keepdims=True) + EPS) * scale
                    o_ref[0, base // dk + h, rows, :] = yh.astype(o_ref.dtype)
            else:
                base = c0 - 2 * qk_w
                for h in range(cb // dv):
                    v_ref[0, base // dv + h, rows, :] = y[:, h * dv:(h + 1) * dv].astype(v_ref.dtype)
        pieces = nxt
    r = jnp.dot(n_main, wab_ref[...], preferred_element_type=F32)
    t = r + abp_ref[1:2, :]
    softplus = jnp.maximum(t, 0.0) + jnp.log(1.0 + jnp.exp(-jnp.abs(t)))
    lane = lax.broadcasted_iota(jnp.int32, r.shape, 1)
    gb_ref[0] = jnp.where(lane < 2 * nheads, (-LOG2E) * jnp.exp(abp_ref[0:1, :]) * softplus, jax.nn.sigmoid(r))


def _proj_conv(x, g, mod, w, w_ab, abp, conv_w, rows, cols, cb=512):
    b, l, d = x.shape
    tr = min(rows, max(1, 512 // cols))
    assert l == rows * cols and rows % tr == 0, (l, rows, cols, tr)
    n = tr * cols
    rp = cols * min(tr, 2)
    halo = rows > tr or rp != cols
    half = (tr + 2) * cols // 2
    pr = half if halo and half % 16 == 0 else rp
    t = np.arange(rp)
    left = (t[:, None] - 1 == t[None, :]) & (t[:, None] % cols != 0)
    right = (t[:, None] + 1 == t[None, :]) & (t[:, None] % cols != cols - 1)
    shift = jnp.asarray(np.concatenate([left, np.eye(rp, dtype=bool), right], axis=1), F32).astype(BF16)
    heads = lambda wd: pl.BlockSpec((1, DN_HEADS, n, wd), lambda bi, i: (bi, 0, i, 0))
    tok = lambda wd: pl.BlockSpec((1, n, wd), lambda bi, i: (bi, i, 0))
    return pl.pallas_call(
        functools.partial(_proj_conv_kernel, d=d, cols=cols, tr=tr, rp=rp, pr=pr, halo=halo,
                          nheads=DN_HEADS, dk=DN_DK, dv=DN_DV, cb=cb),
        out_shape=[jax.ShapeDtypeStruct((b, DN_HEADS, l, DN_DK), BF16),
                   jax.ShapeDtypeStruct((b, DN_HEADS, l, DN_DK), BF16),
                   jax.ShapeDtypeStruct((b, DN_HEADS, l, DN_DV), BF16),
                   jax.ShapeDtypeStruct((b, l, DN_HEADS * DN_DV), BF16),
                   jax.ShapeDtypeStruct((b, l, GB_LANES), F32)],
        grid=(b, rows // tr),
        in_specs=[pl.BlockSpec((1, cols, d), lambda bi, i: (bi, jnp.maximum(i * tr - 1, 0), 0)),
                  pl.BlockSpec((1, n, d), lambda bi, i: (bi, i, 0)),
                  pl.BlockSpec((1, cols, d), lambda bi, i: (bi, jnp.minimum((i + 1) * tr, rows - 1), 0)),
                  _const_spec((1, d)),
                  pl.BlockSpec((1, 1, 3 * d), _mod_map(mod)),
                  _const_spec(w.shape), _const_spec(w_ab.shape), _const_spec(abp.shape), _const_spec(conv_w.shape),
                  _const_spec(shift.shape)],
        out_specs=[heads(DN_DK), heads(DN_DK), heads(DN_DV), tok(DN_HEADS * DN_DV), tok(GB_LANES)],
        compiler_params=_cparams("parallel", "parallel"), name="proj_conv",
    )(x, x, x, g.reshape(1, d), mod, w, w_ab, abp, conv_w, shift)


def _delta_kernel(qa_ref, ka_ref, va_ref, ga_ref, qb_ref, kb_ref, vb_ref, gb_ref, s0_ref,
                  of_ref, ob_ref, sfin_ref, s_scr, sb_scr, *, nheads, cs, nsub):
    n = pl.program_id(1)

    @pl.when(n == 0)
    def _():
        s_scr[...] = s0_ref[0]
        sb_scr[...] = s0_ref[0].astype(sb_scr.dtype)

    ri = lax.broadcasted_iota(jnp.int32, (cs, cs), 0)
    ci = lax.broadcasted_iota(jnp.int32, (cs, cs), 1)
    eye = ri == ci
    xor = ri ^ ci
    nt = (((1,), (1,)), ((), ()))
    levels = int(math.log2(cs))
    def chunk_pair(rows):
        chains = []
        for d, (q_ref, k_ref, v_ref, g_ref, o_ref) in enumerate(((qa_ref, ka_ref, va_ref, ga_ref, of_ref),
                                                                  (qb_ref, kb_ref, vb_ref, gb_ref, ob_ref))):
            incl = (ri >= ci) if d == 0 else (ri <= ci)
            strict = (ri > ci) if d == 0 else (ri < ci)
            gbv = g_ref[0, rows[d], :]
            hi = gbv.astype(BF16)
            rem = gbv - hi.astype(F32)
            mid = rem.astype(BF16)
            t01 = jnp.where(incl, 1.0, 0.0).astype(BF16)
            cum = sum(jnp.dot(t01, p, preferred_element_type=F32)
                      for p in (hi, mid, (rem - mid.astype(F32)).astype(BF16)))
            cum_t = cum.T
            last = cs - 1 if d == 0 else 0
            for hd in range(nheads):
                qb = q_ref[0, hd, rows[d], :]
                kb = k_ref[0, hd, rows[d], :]
                k = kb.astype(F32)
                v = v_ref[0, hd, rows[d], :].astype(F32)
                kk = lax.dot_general(kb, kb, nt, preferred_element_type=F32)
                qk = lax.dot_general(qb, kb, nt, preferred_element_type=F32)
                lg = d * nheads + hd
                lb = (2 + d) * nheads + hd
                gc_col = cum[:, lg:lg + 1]
                gc_row = cum_t[lg:lg + 1, :]
                g_tot = cum[last:last + 1, lg:lg + 1]
                b_col = gbv[:, lb:lb + 1]
                ex = jnp.exp2(gc_col - gc_row)
                egc = jnp.exp2(gc_col)
                chains.append(dict(
                    hd=hd, d=d, o_ref=o_ref, eg=jnp.exp2(g_tot),
                    qkd=jnp.where(incl, qk * ex, 0.0).astype(BF16),
                    ktt=(k.T * jnp.exp2(g_tot - gc_row)).astype(BF16),
                    vb=v * b_col, wq=jnp.concatenate([(k * (b_col * egc)).astype(BF16), (qb.astype(F32) * egc).astype(BF16)],
                                                     axis=0),
                    neg_off=jnp.where(strict, kk * (-b_col) * ex, 0.0)))
        dmats = [jnp.where(eye, 1.0, jnp.where(xor == 1, ch["neg_off"], 0.0)) for ch in chains]
        neg_offs = [ch["neg_off"].astype(BF16) for ch in chains]
        for lv in range(1, levels):
            sel = jnp.where((xor >> lv) == 1, 1.0, 0.0).astype(BF16)
            dmbs = [dm.astype(BF16) for dm in dmats]
            des = [jnp.dot(db, no * sel, preferred_element_type=F32) for db, no in zip(dmbs, neg_offs)]
            dmats = [dm + jnp.dot(de.astype(BF16), db, preferred_element_type=F32)
                     for dm, de, db in zip(dmats, des, dmbs)]
        r1s = [jnp.dot(ch["wq"], sb_scr[ch["hd"], ch["d"]], preferred_element_type=F32) for ch in chains]
        vns = [jnp.dot(dm.astype(BF16), (ch["vb"] - r1[:cs]).astype(BF16), preferred_element_type=F32).astype(BF16)
               for dm, ch, r1 in zip(dmats, chains, r1s)]
        r2s = [jnp.dot(jnp.concatenate([ch["qkd"], ch["ktt"]], axis=0), vn, preferred_element_type=F32)
               for ch, vn in zip(chains, vns)]
        for ch, r1, r2 in zip(chains, r1s, r2s):
            hd, d = ch["hd"], ch["d"]
            ch["o_ref"][0, hd, rows[d], :] = (r1[cs:] + r2[:cs]).astype(of_ref.dtype)
            s_new = s_scr[hd, d] * ch["eg"] + r2[cs:]
            s_scr[hd, d] = s_new
            sb_scr[hd, d] = s_new.astype(sb_scr.dtype)

    for sub in range(nsub):
        chunk_pair((slice(sub * cs, (sub + 1) * cs), slice((nsub - 1 - sub) * cs, (nsub - sub) * cs)))

    @pl.when(n == pl.num_programs(1) - 1)
    def _():
        sfin_ref[0] = s_scr[...]


def _delta(q, k, v, gb, s0, cs=SCAN_CHUNK):
    b, nh, l, dk = q.shape
    dv = v.shape[-1]
    nsub = 2
    assert l % (nsub * cs) == 0 and cs == dk, (l, cs, dk)
    nc = l // (nsub * cs)
    fwd = lambda wd: pl.BlockSpec((1, nh, nsub * cs, wd), lambda bi, n: (bi, 0, n, 0))
    bwd = lambda wd: pl.BlockSpec((1, nh, nsub * cs, wd), lambda bi, n: (bi, 0, nc - 1 - n, 0))
    st = pl.BlockSpec((1, nh, 2, dk, dv), lambda bi, n: (bi, 0, 0, 0, 0))
    return pl.pallas_call(
        functools.partial(_delta_kernel, nheads=nh, cs=cs, nsub=nsub),
        out_shape=[jax.ShapeDtypeStruct((b, nh, l, dv), BF16),
                   jax.ShapeDtypeStruct((b, nh, l, dv), BF16),
                   jax.ShapeDtypeStruct((b, nh, 2, dk, dv), F32)],
        grid=(b, nc),
        in_specs=[fwd(dk), fwd(dk), fwd(dv), pl.BlockSpec((1, nsub * cs, GB_LANES), lambda bi, n: (bi, n, 0)),
                  bwd(dk), bwd(dk), bwd(dv),
                  pl.BlockSpec((1, nsub * cs, GB_LANES), lambda bi, n: (bi, nc - 1 - n, 0)),
                  st],
        out_specs=[fwd(dv), bwd(dv), st],
        scratch_shapes=[pltpu.VMEM((nh, 2, dk, dv), F32), pltpu.VMEM((nh, 2, dk, dv), BF16)],
        compiler_params=_cparams("parallel", "arbitrary"), name="delta_rule",
    )(q, k, v, gb, q, k, v, gb, s0)


def _dn_out_kernel(of_ref, ob_ref, z_ref, res_ref, mod_ref, ng_ref, w_ref, fg_ref, o_ref, *, nheads, dv, d):
    acc = None
    for hd in range(nheads):
        sl = slice(hd * dv, (hd + 1) * dv)
        o = of_ref[0, hd].astype(F32) + ob_ref[0, hd].astype(F32)
        o = o * lax.rsqrt(jnp.mean(o * o, axis=-1, keepdims=True) + EPS) * ng_ref[...]
        y = (o * _silu(z_ref[0, :, sl].astype(F32))).astype(BF16)
        p = jnp.dot(y, w_ref[sl, :], preferred_element_type=F32)
        acc = p if acc is None else acc + p
    gate = mod_ref[0, :, 2 * d:3 * d]
    hn = res_ref[0] + gate * acc
    o_ref[0] = hn * lax.rsqrt(jnp.mean(hn * hn, axis=-1, keepdims=True) + EPS) * fg_ref[...]


def _dn_out(o_f, o_b, z, res, mod, norm_g, w_out, final_g, tm=512):
    b, nh, l, dv = o_f.shape
    d = res.shape[-1]
    tm = min(tm, l)
    assert l % tm == 0, (l, tm)
    tok = lambda wd: pl.BlockSpec((1, tm, wd), lambda i, j: (i, j, 0))
    hs = pl.BlockSpec((1, nh, tm, dv), lambda i, j: (i, 0, j, 0))
    return pl.pallas_call(
        functools.partial(_dn_out_kernel, nheads=nh, dv=dv, d=d),
        out_shape=jax.ShapeDtypeStruct((b, l, d), F32),
        grid=(b, l // tm),
        in_specs=[hs, hs, tok(nh * dv), tok(d),
                  pl.BlockSpec((1, 1, 3 * d), lambda i, j: (i, 0, 0)),
                  _const_spec((1, dv)), _const_spec(w_out.shape), _const_spec((1, d))],
        out_specs=tok(d),
        compiler_params=_cparams("parallel", "parallel"), name="delta_out",
    )(o_f, o_b, z, res, mod, norm_g.reshape(1, dv), w_out, final_g.reshape(1, d))


def _fourier_layer(h_lat, h_ctx, norm_g, mod_lat, mod_ctx, w_in, w_grp, w_out):
    b, l, _ = h_lat.shape
    lc = h_ctx.shape[1]
    width = w_in.shape[1] // 2
    gd = w_grp.shape[-1]
    w_in = w_in.astype(BF16)
    w_out = w_out.astype(BF16)
    gw = _grp(w_grp)

    u, z = _nmm(h_lat, norm_g, mod_lat, w_in, (width, width), (BF16, BF16))
    n2 = DFT_N2
    n1 = l // n2
    y = _dft1(u, n1, n2)
    xr, xi = _dft2_lat(_dft_complex_matrix(n2), y)
    xr = xr.reshape(b, l, width)
    xi = xi.reshape(b, l, width)
    out_lat = _fn_out(xr, xi, z, h_lat, mod_lat, gw, w_out, 1.0 / math.sqrt(l * gd))

    uc, zc = _nmm(h_ctx, norm_g, mod_ctx, w_in, (width, width), (BF16, BF16))
    xrc, xic = _dft2(_dft_real_matrix(lc), uc)
    out_ctx = _fn_out(xrc, xic, zc, h_ctx, mod_ctx, gw, w_out, 1.0 / math.sqrt(lc * gd))
    return out_lat, out_ctx


def _delta_layer_final(h_lat, h_ctx, norm_g, mod_lat, mod_ctx, w_in, conv_w, a_log, dt_bias, dn_norm_g, w_out,
                       final_g):
    b, l, d = h_lat.shape
    lc = h_ctx.shape[1]
    main_w = 2 * DN_HEADS * DN_DK + 2 * DN_HEADS * DN_DV
    n_ab = 4 * DN_HEADS
    w_main = w_in.astype(BF16)
    w_ab = jnp.pad(w_in[:, main_w:], ((0, 0), (0, GB_LANES - n_ab))).astype(BF16)
    pad = jnp.zeros((GB_LANES - 2 * DN_HEADS,), F32)
    abp = jnp.stack([jnp.concatenate([a_log.reshape(-1).astype(F32), pad]),
                     jnp.concatenate([dt_bias.reshape(-1).astype(F32), pad])])
    qc, kc, vc, _, gbc = _proj_conv(h_ctx, norm_g, mod_ctx, w_main, w_ab, abp, conv_w, 1, lc)
    ql, kl, vl, zl, gbl = _proj_conv(h_lat, norm_g, mod_lat, w_main, w_ab, abp, conv_w, l // GRID_W, GRID_W)
    s0 = jnp.zeros((b, DN_HEADS, 2, DN_DK, DN_DV), F32)
    _, _, s_ctx = _delta(qc, kc, vc, gbc, s0)
    o_f, o_b, _ = _delta(ql, kl, vl, gbl, s_ctx)
    return _dn_out(o_f, o_b, zl, h_lat, mod_lat, dn_norm_g, w_out.astype(BF16), final_g)


def kernel(x, c, ctx, c_ctx, mod_w, mod_b, norm_g, final_g, fn_w_in, fn_w_grp, fn_w_out,
           dn_w_in, dn_conv, dn_a_log, dn_dt_bias, dn_norm_g, dn_w_out):
    b, l, d = x.shape
    cond = jnp.concatenate([c, c_ctx[None, :], jnp.zeros((8 - b - 1, d), c.dtype)], axis=0)
    mods = _mods(cond, mod_w, mod_b)
    mod_lat = [mods[i, :b].reshape(b, 1, 3 * d) for i in range(2)]
    mod_ctx = [mods[i, b:b + 1].reshape(1, 1, 3 * d) for i in range(2)]
    h_lat, h_ctx = _fourier_layer(x, ctx, norm_g[0], mod_lat[0], mod_ctx[0], fn_w_in[0], fn_w_grp[0], fn_w_out[0])
    return _delta_layer_final(h_lat, h_ctx, norm_g[1], mod_lat[1], mod_ctx[1], dn_w_in[0], dn_conv[0],
                              dn_a_log[0], dn_dt_bias[0], dn_norm_g[0], dn_w_out[0], final_g)
```
